```python
import jax, jax.numpy as jnp
from jax import lax
import numpy as np

D_MODEL = 2048
BATCH = 8
SEQ = 2048
DEPTH = 2
DEC_BATCH = 128
DEC_SEQ = 4
PAST_LEN = 2048
PAGE_SIZE = 128

CONV_CH = D_MODEL // 2
CONV_WIDTH = 3
HEAD_DIM = 128
N_HEADS = (D_MODEL - CONV_CH) // HEAD_DIM
N_KV = 2
GQA = N_HEADS // N_KV
ATT_W = N_HEADS * HEAD_DIM
KV_W = N_KV * HEAD_DIM
CMP_LEN = 32
CMP_STRIDE = 16
SEL_BLOCK = 64
N_SELECT = 8
WINDOW = 512
Q_BLOCK = 128
ROPE_THETA = 10000.0
EPS = 1e-6
NEG = -1e30
FORCE_SCORE = 1e3
N_BRANCH_KV = 4

_IN_SIZES = (CONV_CH, CONV_CH, CONV_CH, CONV_CH, ATT_W, KV_W, KV_W, KV_W, KV_W, KV_W, KV_W, 3 * N_HEADS, ATT_W)
_IN_OFFS = tuple(sum(_IN_SIZES[:i]) for i in range(len(_IN_SIZES) + 1))
N_IN = _IN_OFFS[-1]

kernel_name = 'nsa_shortconv_parallel_hybrid_step'


def rmsnorm(x, g):
    xf = x.astype(jnp.float32)
    y = xf * lax.rsqrt(jnp.mean(xf * xf, -1, keepdims=True) + EPS)
    return (y * g.astype(jnp.float32)).astype(x.dtype)


def rope(x, pos):
    half = HEAD_DIM // 2
    inv = jnp.power(ROPE_THETA, -jnp.arange(half, dtype=jnp.float32) / half)
    ang = pos.astype(jnp.float32)[:, None] * inv
    shp = (1, ang.shape[0]) + (1,) * (x.ndim - 3) + (half,)
    cos, sin = jnp.cos(ang).reshape(shp), jnp.sin(ang).reshape(shp)
    xf = x.astype(jnp.float32)
    x1, x2 = xf[..., :half], xf[..., half:]
    return jnp.concatenate([x1 * cos - x2 * sin, x2 * cos + x1 * sin], -1).astype(x.dtype)


def _masked_softmax(s, mask):
    s = jnp.where(mask, s.astype(jnp.float32), NEG)
    e = jnp.where(mask, jnp.exp(s - jnp.max(s, -1, keepdims=True)), 0.0)
    return e / jnp.maximum(jnp.sum(e, -1, keepdims=True), 1e-30)


def _modulated_projection(x, c, w_ada, b_ada, g_pre, w_in):
    mod = jax.nn.silu(c) @ w_ada + b_ada
    shift, scale, gate = jnp.split(mod, 3, axis=-1)
    h = rmsnorm(x, g_pre) * (1.0 + scale[:, None]) + shift[:, None]
    return h @ w_in, gate


def _split(u):
    return [u[..., _IN_OFFS[i]:_IN_OFFS[i + 1]] for i in range(len(_IN_SIZES))]


def _short_conv(b_g, c_g, x_in, z, buf, conv_w):
    u = c_g * x_in
    t = u.shape[1]
    up = jnp.concatenate([buf.astype(u.dtype), u], axis=1)
    y = sum(conv_w[i] * up[:, i:i + t] for i in range(CONV_WIDTH))
    return jax.nn.silu(z) * (b_g * y), up[:, -(CONV_WIDTH - 1):]


def _attn_streams(parts, pos):
    q, kc, vc, ks, vs, kw, vw, g, z = parts[4:]
    bsz, t = q.shape[:2]
    q = rope(q.reshape(bsz, t, N_KV, GQA, HEAD_DIM), pos)
    kv = lambda a: a.reshape(bsz, t, N_KV, HEAD_DIM)
    kc, ks, kw = rope(kv(kc), pos), rope(kv(ks), pos), rope(kv(kw), pos)
    gates = jax.nn.sigmoid(g.reshape(bsz, t, N_KV, GQA, 3))
    return q, kc, kv(vc), ks, kv(vs), kw, kv(vw), gates, z


def _compress(k, pe, w1, w2):
    bsz, length = k.shape[:2]
    nsub = -(-length // CMP_STRIDE)
    r = CMP_LEN // CMP_STRIDE
    k = jnp.pad(k, ((0, 0), (0, nsub * CMP_STRIDE - length), (0, 0), (0, 0)))
    sub = k.reshape(bsz, nsub, CMP_STRIDE, N_KV, HEAD_DIM)
    nc = nsub - r + 1
    blocks = jnp.concatenate([sub[:, i:i + nc] for i in range(r)], axis=2)
    blocks = blocks + pe[:, None, :]
    flat = blocks.transpose(0, 1, 3, 2, 4).reshape(bsz, nc, N_KV, CMP_LEN * HEAD_DIM)
    return jax.nn.silu(flat @ w1) @ w2


def _n_sel_blocks(length):
    return max(-(-length // SEL_BLOCK), N_SELECT)


def _sel_blocks(k, nb):
    bsz, length = k.shape[:2]
    k = jnp.pad(k, ((0, 0), (0, nb * SEL_BLOCK - length), (0, 0), (0, 0)))
    return k.reshape(bsz, nb, SEL_BLOCK, N_KV, HEAD_DIM).transpose(0, 3, 1, 2, 4)


def _nsa_attend(q, qpos, kc, vc, kb, vb, kw, vw, kw_pos, gates):
    scale = HEAD_DIM ** -0.5
    bsz, tq = q.shape[:2]
    nc, nb = kc.shape[1], kb.shape[2]
    c_end = jnp.arange(nc) * CMP_STRIDE + (CMP_LEN - 1)
    s = jnp.einsum('btgrd,bcgd->bgrtc', q, kc) * scale
    p_cmp = _masked_softmax(s, c_end[None, :] <= qpos[:, None])
    o_cmp = jnp.einsum('bgrtc,bcgd->btgrd', p_cmp.astype(vc.dtype), vc)
    per = SEL_BLOCK // CMP_STRIDE
    imp = jnp.pad(p_cmp.sum(2), ((0, 0), (0, 0), (0, 0), (0, nb * per - nc)))
    imp = imp.reshape(bsz, N_KV, tq, nb, per).sum(-1)
    blk = jnp.arange(nb)
    cur = (qpos // SEL_BLOCK)[:, None]
    forced = (blk == 0) | (blk == cur) | (blk == cur - 1)
    score = jnp.where(blk <= cur, jnp.where(forced, FORCE_SCORE, imp), NEG)
    top_s, top_i = lax.top_k(score, N_SELECT)
    take = jax.vmap(jax.vmap(lambda a, i: a[i]))
    k_sel, v_sel = take(kb, top_i), take(vb, top_i)
    k_pos = top_i[..., None] * SEL_BLOCK + jnp.arange(SEL_BLOCK)
    m_sel = (top_s > 0.5 * NEG)[..., None] & (k_pos <= qpos[:, None, None])
    s = jnp.einsum('btgrd,bgtnkd->bgrtnk', q, k_sel) * scale
    p = _masked_softmax(s.reshape(s.shape[:4] + (-1,)), m_sel.reshape(bsz, N_KV, 1, tq, -1))
    o_slc = jnp.einsum('bgrtj,bgtjd->btgrd', p.astype(v_sel.dtype),
                       v_sel.reshape(bsz, N_KV, tq, -1, HEAD_DIM))
    s = jnp.einsum('btgrd,bwgd->bgrtw', q, kw) * scale
    dist = qpos[:, None] - kw_pos[None, :]
    p = _masked_softmax(s, (dist >= 0) & (dist < WINDOW) & (kw_pos[None, :] >= 0))
    o_win = jnp.einsum('bgrtw,bwgd->btgrd', p.astype(vw.dtype), vw)
    return gates[..., 0:1] * o_cmp + gates[..., 1:2] * o_slc + gates[..., 2:3] * o_win


def _prompt_nsa(q, kc, vc, kb, vb, kw, vw, gates):
    bsz, t = q.shape[:2]
    pad = ((0, 0), (WINDOW, 0), (0, 0), (0, 0))
    kw_pad, vw_pad = jnp.pad(kw, pad), jnp.pad(vw, pad)

    def block(i):
        s0 = i * Q_BLOCK
        sl = lambda a, n: lax.dynamic_slice_in_dim(a, s0, n, axis=1)
        qpos = s0 + jnp.arange(Q_BLOCK)
        kpos = s0 - WINDOW + jnp.arange(WINDOW + Q_BLOCK)
        return _nsa_attend(sl(q, Q_BLOCK), qpos, kc, vc, kb, vb,
                           sl(kw_pad, WINDOW + Q_BLOCK), sl(vw_pad, WINDOW + Q_BLOCK), kpos,
                           sl(gates, Q_BLOCK))

    o = lax.map(block, jnp.arange(t // Q_BLOCK))
    return jnp.moveaxis(o, 0, 1).reshape(bsz, t, N_KV, GQA, HEAD_DIM)


def _merge(x, ya, yb, gate, g_grp, w_out, g_post):
    y = jnp.concatenate([rmsnorm(ya, g_grp[:CONV_CH]), rmsnorm(yb, g_grp[CONV_CH:])], -1) @ w_out
    return x + gate[:, None] * rmsnorm(y, g_post)


def setup_inputs(seed: int = 0) -> dict:
    key = jax.random.key(seed)
    ks = jax.random.split(key, 19)
    n_pages = PAST_LEN // PAGE_SIZE
    n_used = DEC_BATCH * n_pages
    n_pool = n_used + n_used // 4
    wbuf = min(WINDOW, PAST_LEN)
    nrm = lambda k, shape, s: s * jax.random.normal(k, shape, jnp.float32)
    page_table = jax.random.permutation(ks[0], n_pool)[:n_used].reshape(DEC_BATCH, n_pages).astype(jnp.int32)
    return {
        'x_prompt': nrm(ks[1], (BATCH, SEQ, D_MODEL), 1.0),
        'x_sample': nrm(ks[2], (DEC_BATCH, DEC_SEQ, D_MODEL), 1.0),
        'cache_kv': nrm(ks[3], (DEPTH, n_pool, PAGE_SIZE, N_BRANCH_KV, N_KV, HEAD_DIM), 1.0),
        'cache_win': nrm(ks[4], (DEPTH, DEC_BATCH, wbuf, 2, N_KV, HEAD_DIM), 1.0),
        'state_conv': nrm(ks[5], (DEPTH, DEC_BATCH, CONV_WIDTH - 1, CONV_CH), 1.0),
        'page_table': page_table,
        'c_prompt': nrm(ks[6], (BATCH, D_MODEL), 1.0),
        'c_sample': nrm(ks[7], (DEC_BATCH, D_MODEL), 1.0),
        'w_ada': nrm(ks[8], (DEPTH, D_MODEL, 3 * D_MODEL), D_MODEL ** -0.5),
        'b_ada': nrm(ks[9], (DEPTH, 3 * D_MODEL), 0.01),
        'g_pre': 1.0 + nrm(ks[10], (DEPTH, D_MODEL), 0.1),
        'w_in': nrm(ks[11], (DEPTH, D_MODEL, N_IN), D_MODEL ** -0.5),
        'conv_w': nrm(ks[12], (DEPTH, CONV_WIDTH, CONV_CH), CONV_WIDTH ** -0.5),
        'cmp_pe': nrm(ks[13], (DEPTH, 2, CMP_LEN, HEAD_DIM), 0.1),
        'cmp_w1': nrm(ks[14], (DEPTH, 2, CMP_LEN * HEAD_DIM, HEAD_DIM), (CMP_LEN * HEAD_DIM) ** -0.5),
        'cmp_w2': nrm(ks[15], (DEPTH, 2, HEAD_DIM, HEAD_DIM), HEAD_DIM ** -0.5),
        'g_grp': 1.0 + nrm(ks[16], (DEPTH, D_MODEL), 0.1),
        'w_out': nrm(ks[17], (DEPTH, CONV_CH + ATT_W, D_MODEL), D_MODEL ** -0.5),
        'g_post': 1.0 + nrm(ks[18], (DEPTH, D_MODEL), 0.1),
    }


def reference(x_prompt, x_sample, cache_kv, cache_win, state_conv, page_table, c_prompt, c_sample,
              w_ada, b_ada, g_pre, w_in, conv_w, cmp_pe, cmp_w1, cmp_w2, g_grp, w_out, g_post):
    bp, tp = x_prompt.shape[:2]
    bs, ts = x_sample.shape[:2]
    past = page_table.shape[1] * cache_kv.shape[2]
    wbuf = cache_win.shape[2]
    pos_p = jnp.arange(tp)
    pos_s = past + jnp.arange(ts)
    kw_pos_s = past - wbuf + jnp.arange(wbuf + ts)
    nb_p, nb_s = _n_sel_blocks(tp), _n_sel_blocks(past + ts)
    xp, xs = x_prompt, x_sample
    kv_p, kv_s, win_p, win_s, conv_p, conv_s = [], [], [], [], [], []
    for l in range(DEPTH):
        comp = lambda k, i: _compress(k, cmp_pe[l, i], cmp_w1[l, i], cmp_w2[l, i])
        u, gate = _modulated_projection(xp, c_prompt, w_ada[l], b_ada[l], g_pre[l], w_in[l])
        parts = _split(u)
        ya, cbuf = _short_conv(*parts[:4], jnp.zeros((bp, CONV_WIDTH - 1, CONV_CH), xp.dtype), conv_w[l])
        q, kc, vc, ksl, vsl, kw, vw, gates, z = _attn_streams(parts, pos_p)
        o = _prompt_nsa(q, comp(kc, 0), comp(vc, 1), _sel_blocks(ksl, nb_p), _sel_blocks(vsl, nb_p), kw, vw, gates)
        yb = o.reshape(bp, tp, ATT_W) * jax.nn.silu(z)
        xp = _merge(xp, ya, yb, gate, g_grp[l], w_out[l], g_post[l])
        kv_p.append(jnp.stack([kc, vc, ksl, vsl], axis=2))
        win_p.append(jnp.stack([kw, vw], axis=2)[:, -min(WINDOW, tp):])
        conv_p.append(cbuf)
        u, gate = _modulated_projection(xs, c_sample, w_ada[l], b_ada[l], g_pre[l], w_in[l])
        parts = _split(u)
        ya, cbuf = _short_conv(*parts[:4], state_conv[l], conv_w[l])
        q, kc, vc, ksl, vsl, kw, vw, gates, z = _attn_streams(parts, pos_s)
        new_kv = jnp.stack([kc, vc, ksl, vsl], axis=2)
        past_kv = cache_kv[l, page_table].reshape(bs, past, N_BRANCH_KV, N_KV, HEAD_DIM)
        full = jnp.concatenate([past_kv.astype(new_kv.dtype), new_kv], axis=1)
        win = jnp.concatenate([cache_win[l].astype(kw.dtype), jnp.stack([kw, vw], axis=2)], axis=1)
        o = _nsa_attend(q, pos_s, comp(full[:, :, 0], 0), comp(full[:, :, 1], 1),
                        _sel_blocks(full[:, :, 2], nb_s), _sel_blocks(full[:, :, 3], nb_s),
                        win[:, :, 0], win[:, :, 1], kw_pos_s, gates)
        yb = o.reshape(bs, ts, ATT_W) * jax.nn.silu(z)
        xs = _merge(xs, ya, yb, gate, g_grp[l], w_out[l], g_post[l])
        kv_s.append(new_kv)
        win_s.append(win[:, -wbuf:])
        conv_s.append(cbuf)
    return (xp, xs, jnp.stack(kv_p), jnp.stack(kv_s), jnp.stack(win_p), jnp.stack(win_s),
            jnp.stack(conv_p), jnp.stack(conv_s))
```

```python
import functools

import jax
import jax.numpy as jnp
from jax import lax
from jax.experimental import pallas as pl
from jax.experimental.pallas import tpu as pltpu

F32 = jnp.float32
BF16 = jnp.bfloat16

HEAD_DIM = 128
N_KV = 2
CONV_WIDTH = 3
CMP_LEN = 32
CMP_STRIDE = 16
SEL_BLOCK = 64
N_SELECT = 8
WINDOW = 512
ROPE_THETA = 10000.0
EPS = 1e-6
NEG = -1e30
FORCE_SCORE = 1e3
KV_W = N_KV * HEAD_DIM
LANES = 128
VMEM_LIMIT = 52 * 1024 * 1024

TQ = 128
KC_SEL = 512
KC_WIN = 128
T_PAD = 8


def _cparams(sem):
    return pltpu.CompilerParams(dimension_semantics=sem, vmem_limit_bytes=VMEM_LIMIT)


def _silu(x):
    return x / (1.0 + jnp.exp(-x))


def _sigmoid(x):
    return 1.0 / (1.0 + jnp.exp(-x))


def _dot(a, b):
    return jnp.dot(a, b, preferred_element_type=F32)


def _dot_nt(a, b):
    return lax.dot_general(a, b, (((1,), (1,)), ((), ())), preferred_element_type=F32)


def _mod_kernel(c_ref, w_ref, b_ref, o_ref):
    a = _silu(c_ref[...]).astype(BF16)
    o_ref[...] = _dot(a, w_ref[...].astype(BF16)) + b_ref[...]


def _modulation(c_all, w_ada, b_ada):
    depth, d, n = w_ada.shape
    rows = c_all.shape[0]
    tn = 768
    return pl.pallas_call(
        _mod_kernel,
        grid=(depth, n // tn),
        in_specs=[
            pl.BlockSpec((rows, d), lambda l, j: (0, 0)),
            pl.BlockSpec((None, d, tn), lambda l, j: (l, 0, j)),
            pl.BlockSpec((None, 1, tn), lambda l, j: (l, 0, j)),
        ],
        out_specs=pl.BlockSpec((None, rows, tn), lambda l, j: (l, 0, j)),
        out_shape=jax.ShapeDtypeStruct((depth, rows, n), F32),
        compiler_params=_cparams(("parallel", "parallel")),
        name="mod",
    )(c_all, w_ada, b_ada.reshape(depth, 1, n))


def _rope_cols(v, cos, sin):
    outs = []
    for h in range(v.shape[1] // HEAD_DIM):
        xh = v[:, h * HEAD_DIM:(h + 1) * HEAD_DIM]
        outs.append(xh * cos + pltpu.roll(xh, HEAD_DIM // 2, 1) * sin)
    return outs[0] if len(outs) == 1 else jnp.concatenate(outs, axis=1)


def _inproj_kernel(x_ref, scale_ref, shift_ref, g_ref, cos_ref, sin_ref, w_ref, wg_ref,
                   u_ref, gates_ref, h_ref, *, full_lo, full_hi, half_tiles):
    j = pl.program_id(1)

    @pl.when(j == 0)
    def _():
        x = x_ref[...]
        y = x * lax.rsqrt(jnp.mean(x * x, axis=-1, keepdims=True) + EPS)
        h = (y * g_ref[...]) * (1.0 + scale_ref[...]) + shift_ref[...]
        hb = h.astype(BF16)
        h_ref[...] = hb
        gates_ref[...] = _sigmoid(_dot(hb, wg_ref[...]))

    acc = _dot(h_ref[...], w_ref[...])
    is_full = (j >= full_lo) & (j < full_hi)
    is_half = functools.reduce(lambda a, b: a | b, [j == t for t in half_tiles])

    @pl.when(is_full)
    def _():
        u_ref[...] = _rope_cols(acc, cos_ref[...], sin_ref[...])

    @pl.when(is_half)
    def _():
        u_ref[...] = jnp.concatenate(
            [_rope_cols(acc[:, :KV_W], cos_ref[...], sin_ref[...]), acc[:, KV_W:]], axis=1)

    @pl.when(jnp.logical_not(is_full | is_half))
    def _():
        u_ref[...] = acc


def _in_projection(x2d, scale, shift, g_pre, cos, sin, w_main, w_gate, tm, conv_ch, att_w):
    r, d = x2d.shape
    nm = w_main.shape[1]
    tn = 2 * KV_W
    tiles_per_scale = (r // tm) // scale.shape[0]
    s_rows = scale.shape[1]
    pos_tiles = cos.shape[0] // tm
    q0 = 4 * conv_ch // tn
    q1 = q0 + att_w // tn
    half_tiles = (q1, q1 + 1, q1 + 2 + att_w // tn)
    kern = functools.partial(_inproj_kernel, full_lo=q0, full_hi=q1, half_tiles=half_tiles)
    return pl.pallas_call(
        kern,
        grid=(r // tm, nm // tn),
        in_specs=[
            pl.BlockSpec((tm, d), lambda i, j: (i, 0)),
            pl.BlockSpec((None, s_rows, d), lambda i, j: (i // tiles_per_scale, 0, 0)),
            pl.BlockSpec((None, s_rows, d), lambda i, j: (i // tiles_per_scale, 0, 0)),
            pl.BlockSpec((1, d), lambda i, j: (0, 0)),
            pl.BlockSpec((tm, HEAD_DIM), lambda i, j: (i % pos_tiles, 0)),
            pl.BlockSpec((tm, HEAD_DIM), lambda i, j: (i % pos_tiles, 0)),
            pl.BlockSpec((d, tn), lambda i, j: (0, j)),
            pl.BlockSpec((d, LANES), lambda i, j: (0, 0)),
        ],
        out_specs=[
            pl.BlockSpec((tm, tn), lambda i, j: (i, j)),
            pl.BlockSpec((tm, LANES), lambda i, j: (i, 0)),
        ],
        out_shape=[jax.ShapeDtypeStruct((r, nm), F32), jax.ShapeDtypeStruct((r, LANES), F32)],
        scratch_shapes=[pltpu.VMEM((tm, d), BF16)],
        compiler_params=_cparams(("parallel", "arbitrary")),
        name="inproj",
    )(x2d, scale, shift, g_pre, cos, sin, w_main, w_gate)


def _conv_kernel(b_ref, c_ref, x_ref, z_ref, init_ref, w_ref, g_ref, ya_ref, st_ref, up_ref, *, tr):
    i = pl.program_id(1)
    pad = 8

    @pl.when(i == 0)
    def _():
        up_ref[pl.ds(pad - 2, 2), :] = init_ref[...]

    up_ref[pl.ds(pad, tr), :] = c_ref[...] * x_ref[...]
    w = w_ref[...]
    y = (w[0:1, :] * up_ref[pl.ds(pad - 2, tr), :]
         + w[1:2, :] * up_ref[pl.ds(pad - 1, tr), :]
         + w[2:3, :] * up_ref[pl.ds(pad, tr), :])
    ya = _silu(z_ref[...]) * (b_ref[...] * y)
    yn = ya * lax.rsqrt(jnp.mean(ya * ya, axis=-1, keepdims=True) + EPS)
    ya_ref[...] = (yn * g_ref[...]).astype(ya_ref.dtype)
    last = up_ref[pl.ds(pad + tr - 2, 2), :]
    st_ref[...] = last
    up_ref[pl.ds(pad - 2, 2), :] = last


def _short_conv(u3, init_state, conv_w, g_a, tr, out_dtype):
    b, t, _ = u3.shape
    c = conv_w.shape[1]
    col = lambda k: pl.BlockSpec((None, tr, c), lambda bi, i, k=k: (bi, i, k))
    return pl.pallas_call(
        functools.partial(_conv_kernel, tr=tr),
        grid=(b, t // tr),
        in_specs=[
            col(0), col(1), col(2), col(3),
            pl.BlockSpec((None, CONV_WIDTH - 1, c), lambda bi, i: (bi, 0, 0)),
            pl.BlockSpec((CONV_WIDTH, c), lambda bi, i: (0, 0)),
            pl.BlockSpec((1, c), lambda bi, i: (0, 0)),
        ],
        out_specs=[
            pl.BlockSpec((None, tr, c), lambda bi, i: (bi, i, 0)),
            pl.BlockSpec((None, CONV_WIDTH - 1, c), lambda bi, i: (bi, 0, 0)),
        ],
        out_shape=[jax.ShapeDtypeStruct((b, t, c), out_dtype),
                   jax.ShapeDtypeStruct((b, CONV_WIDTH - 1, c), F32)],
        scratch_shapes=[pltpu.VMEM((tr + 8, c), F32)],
        compiler_params=_cparams(("parallel", "arbitrary")),
        name="conv",
    )(u3, u3, u3, u3, init_state, conv_w, g_a)


def _compress_block(load_sub, pe, w1, w2, nsp):
    subs = [load_sub(r) for r in range(CMP_STRIDE)]
    a_lo = jnp.concatenate([(subs[r] + pe[r:r + 1, :]).astype(BF16) for r in range(CMP_STRIDE)], axis=1)
    a_hi = jnp.concatenate([(subs[r] + pe[CMP_STRIDE + r:CMP_STRIDE + r + 1, :]).astype(BF16)
                            for r in range(CMP_STRIDE)], axis=1)
    half = CMP_STRIDE * HEAD_DIM
    p_lo = _dot(a_lo, w1[:half, :])
    p_hi = _dot(a_hi, w1[half:, :])
    pre = p_lo + pltpu.roll(p_hi, nsp - 1, 0)
    return _dot(_silu(pre).astype(BF16), w2)


def _compress_kernel(k_ref, pe_ref, w1_ref, w2_ref, o_ref, *, nsp):
    load_sub = lambda r: k_ref[pl.ds(r, nsp, stride=CMP_STRIDE), :]
    o_ref[...] = _compress_block(load_sub, pe_ref[...], w1_ref[...], w2_ref[...], nsp)


def _compress_prompt(u3, kv_col0, cmp_pe, cmp_w1, cmp_w2):
    b, t, _ = u3.shape
    nsp = t // CMP_STRIDE
    blk0 = kv_col0 // HEAD_DIM
    return pl.pallas_call(
        functools.partial(_compress_kernel, nsp=nsp),
        grid=(b, 2 * N_KV),
        in_specs=[
            pl.BlockSpec((None, t, HEAD_DIM), lambda bi, s: (bi, 0, blk0 + s)),
            pl.BlockSpec((None, CMP_LEN, HEAD_DIM), lambda bi, s: (s // N_KV, 0, 0)),
            pl.BlockSpec((None, CMP_LEN * HEAD_DIM, HEAD_DIM), lambda bi, s: (s // N_KV, 0, 0)),
            pl.BlockSpec((None, HEAD_DIM, HEAD_DIM), lambda bi, s: (s // N_KV, 0, 0)),
        ],
        out_specs=pl.BlockSpec((None, None, nsp, HEAD_DIM), lambda bi, s: (bi, s, 0, 0)),
        out_shape=jax.ShapeDtypeStruct((b, 2 * N_KV, nsp, HEAD_DIM), F32),
        compiler_params=_cparams(("parallel", "parallel")),
        name="compress",
    )(u3, cmp_pe, cmp_w1, cmp_w2)


def _masked_softmax(s, mask):
    s = jnp.where(mask, s, NEG)
    e = jnp.where(mask, jnp.exp(s - jnp.max(s, axis=-1, keepdims=True)), 0.0)
    return e / jnp.maximum(jnp.sum(e, axis=-1, keepdims=True), 1e-30)


def _online_update(carry, s, mask, v):
    m, l, acc = carry
    s = jnp.where(mask, s, NEG)
    m_new = jnp.maximum(m, jnp.max(s, axis=-1, keepdims=True))
    alpha = jnp.exp(m - m_new)
    e = jnp.where(mask, jnp.exp(s - m_new), 0.0)
    l = alpha * l + jnp.sum(e, axis=-1, keepdims=True)
    acc = alpha * acc + _dot(e.astype(BF16), v)
    return m_new, l, acc


def _online_init(rows):
    return (jnp.full((rows, 1), NEG, F32), jnp.zeros((rows, 1), F32), jnp.zeros((rows, HEAD_DIM), F32))


def _online_finish(carry):
    _, l, acc = carry
    return acc / jnp.maximum(l, 1e-30)


def _compressed_branch(qg, kc, vc, qpos_rows, nc, n_rep, scale):
    ncp = min(kc.shape[0], -(-nc // LANES) * LANES)
    assert ncp >= nc
    kc, vc = kc[:ncp], vc[:ncp]
    s = _dot_nt(qg, kc.astype(BF16)) * scale
    cidx = lax.broadcasted_iota(jnp.int32, (1, ncp), 1)
    mask = ((cidx * CMP_STRIDE + (CMP_LEN - 1)) <= qpos_rows) & (cidx < nc)
    p = _masked_softmax(s, mask)
    o = _dot(p.astype(BF16), vc.astype(BF16))
    rt = qg.shape[0] // n_rep
    p_tok = p[0:rt]
    for r in range(1, n_rep):
        p_tok = p_tok + p[r * rt:(r + 1) * rt]
    per = SEL_BLOCK // CMP_STRIDE
    pool = jnp.where(
        (lax.broadcasted_iota(jnp.int32, (ncp, LANES), 0) // per)
        == lax.broadcasted_iota(jnp.int32, (ncp, LANES), 1), 1.0, 0.0).astype(F32)
    imp = jnp.dot(p_tok, pool, preferred_element_type=F32, precision=lax.Precision.HIGHEST)
    return o, imp


def _select_blocks(imp, qpos_tok, nb):
    lane = lax.broadcasted_iota(jnp.int32, imp.shape, 1)
    cur = qpos_tok // SEL_BLOCK
    forced = (lane == 0) | (lane == cur) | (lane == cur - 1)
    score = jnp.where(lane <= cur, jnp.where(forced, FORCE_SCORE, imp), NEG)
    rank = jnp.zeros(imp.shape, jnp.int32)
    for j in range(nb):
        col = score[:, j:j + 1]
        beats = (col > score) | ((col == score) & (lane > j))
        rank = rank + jnp.where(beats, 1, 0)
    sel = (rank < N_SELECT) & (score > 0.5 * NEG)
    return jnp.where(sel, 1.0, 0.0).astype(BF16)


def _gated_heads(g, o_cmp, o_slc, o_win, gate_col, z_of, n_rep, rt):
    outs = []
    for r in range(n_rep):
        sl = slice(r * rt, (r + 1) * rt)
        o = gate_col(g, r, 0) * o_cmp[sl] + gate_col(g, r, 1) * o_slc[sl] + gate_col(g, r, 2) * o_win[sl]
        outs.append(o * _silu(z_of(g, r)))
    return outs


def _attn_prompt_kernel(q_ref, kvs_ref, kvw_ref, cmp_ref, gates_ref, z_ref, ggrp_ref, exp_ref, o_ref,
                        *, nc, nb, gqa, att_w):
    i = pl.program_id(1)
    s0 = i * TQ
    scale = HEAD_DIM ** -0.5
    tok = lax.broadcasted_iota(jnp.int32, (TQ, 1), 0) + s0
    qpos_rows = jnp.concatenate([tok] * gqa, axis=0)
    rows = TQ * gqa
    heads = []
    for g in range(N_KV):
        qg = jnp.concatenate(
            [q_ref[:, (g * gqa + r) * HEAD_DIM:(g * gqa + r + 1) * HEAD_DIM] for r in range(gqa)],
            axis=0).astype(BF16)
        o_cmp, imp = _compressed_branch(qg, cmp_ref[g], cmp_ref[N_KV + g], qpos_rows, nc, gqa, scale)
        sel = _select_blocks(imp, tok, nb)
        kcol = slice(g * HEAD_DIM, (g + 1) * HEAD_DIM)
        vcol = slice(KV_W + g * HEAD_DIM, KV_W + (g + 1) * HEAD_DIM)

        def sel_body(ci, carry):
            k0 = pl.multiple_of(ci * KC_SEL, KC_SEL)
            k = kvs_ref[pl.ds(k0, KC_SEL), kcol].astype(BF16)
            v = kvs_ref[pl.ds(k0, KC_SEL), vcol].astype(BF16)
            s = _dot_nt(qg, k) * scale
            kpos = k0 + lax.broadcasted_iota(jnp.int32, (1, KC_SEL), 1)
            in_sel = jnp.where(kpos <= tok, _dot(sel, exp_ref[:, pl.ds(k0, KC_SEL)]), 0.0)
            return _online_update(carry, s, jnp.concatenate([in_sel] * gqa, axis=0) > 0.5, v)

        n_sel = (s0 + TQ + KC_SEL - 1) // KC_SEL
        o_slc = _online_finish(lax.fori_loop(0, n_sel, sel_body, _online_init(rows)))

        def win_body(cj, carry):
            k0 = pl.multiple_of(cj * KC_WIN, KC_WIN)
            k = kvw_ref[pl.ds(k0, KC_WIN), kcol].astype(BF16)
            v = kvw_ref[pl.ds(k0, KC_WIN), vcol].astype(BF16)
            s = _dot_nt(qg, k) * scale
            kpos = k0 + lax.broadcasted_iota(jnp.int32, (1, KC_WIN), 1)
            dist = qpos_rows - kpos
            return _online_update(carry, s, (dist >= 0) & (dist < WINDOW), v)

        lo = jnp.maximum(i - WINDOW // KC_WIN, 0)
        o_win = _online_finish(lax.fori_loop(lo, i + 1, win_body, _online_init(rows)))

        gate_col = lambda g_, r, b: gates_ref[:, ((g_ * gqa + r) * 3 + b):((g_ * gqa + r) * 3 + b + 1)]
        z_of = lambda g_, r: z_ref[:, (g_ * gqa + r) * HEAD_DIM:(g_ * gqa + r + 1) * HEAD_DIM]
        heads += _gated_heads(g, o_cmp, o_slc, o_win, gate_col, z_of, gqa, TQ)

    ss = jnp.sum(heads[0] * heads[0], axis=-1, keepdims=True)
    for y in heads[1:]:
        ss = ss + jnp.sum(y * y, axis=-1, keepdims=True)
    inv = lax.rsqrt(ss / att_w + EPS)
    o_ref[...] = jnp.concatenate(
        [(y * inv) * ggrp_ref[:, h * HEAD_DIM:(h + 1) * HEAD_DIM] for h, y in enumerate(heads)],
        axis=1).astype(o_ref.dtype)


def _attention_prompt(u3, gates3, kcmp, g_b, expand, cols, att_w):
    b, t, _ = u3.shape
    gqa = att_w // HEAD_DIM // N_KV
    nsub = t // CMP_STRIDE
    nc = nsub - CMP_LEN // CMP_STRIDE + 1
    nb = max(-(-t // SEL_BLOCK), N_SELECT)
    kern = functools.partial(_attn_prompt_kernel, nc=nc, nb=nb, gqa=gqa, att_w=att_w)
    two_kv = 2 * KV_W
    return pl.pallas_call(
        kern,
        grid=(b, t // TQ),
        in_specs=[
            pl.BlockSpec((None, TQ, att_w), lambda bi, i: (bi, i, cols["q"] // att_w)),
            pl.BlockSpec((None, t, two_kv), lambda bi, i: (bi, 0, cols["kvs"] // two_kv)),
            pl.BlockSpec((None, t, two_kv), lambda bi, i: (bi, 0, cols["kvw"] // two_kv)),
            pl.BlockSpec((None, 2 * N_KV, kcmp.shape[2], HEAD_DIM), lambda bi, i: (bi, 0, 0, 0)),
            pl.BlockSpec((None, TQ, LANES), lambda bi, i: (bi, i, 0)),
            pl.BlockSpec((None, TQ, att_w), lambda bi, i: (bi, i, cols["zb"] // att_w)),
            pl.BlockSpec((1, att_w), lambda bi, i: (0, 0)),
            pl.BlockSpec((LANES, t), lambda bi, i: (0, 0)),
        ],
        out_specs=pl.BlockSpec((None, TQ, att_w), lambda bi, i: (bi, i, 0)),
        out_shape=jax.ShapeDtypeStruct((b, t, att_w), BF16),
        compiler_params=_cparams(("parallel", "arbitrary")),
        name="attn_prompt",
    )(u3, u3, u3, kcmp, gates3, u3, g_b, expand)


def _attn_sample_kernel(pt_ref, q_ref, gcol_ref, z_ref, ggrp_ref, newkv_ref, neww_ref, cw_ref,
                        pe_ref, w1_ref, w2_ref, exp_ref, *rest,
                        n_pages, page, ts, wbuf, nsp, nc, nb, gqa, att_w):
    page_refs = rest[:n_pages]
    o_ref, win_ref, full_ref, kwin_ref = rest[n_pages:]
    del pt_ref
    past = n_pages * page
    ktot = nsp * CMP_STRIDE
    scale = HEAD_DIM ** -0.5
    rows = gqa * T_PAD

    for p in range(n_pages):
        for s in range(4 * N_KV):
            full_ref[s, pl.ds(p * page, page), :] = page_refs[p][:, s * HEAD_DIM:(s + 1) * HEAD_DIM]
    for s in range(4 * N_KV):
        full_ref[s, pl.ds(past, ts), :] = newkv_ref[:, s * HEAD_DIM:(s + 1) * HEAD_DIM]
        full_ref[s, pl.ds(past + ts, ktot - past - ts), :] = jnp.zeros((ktot - past - ts, HEAD_DIM), F32)
    kwin_ref[pl.ds(0, wbuf), :] = cw_ref[...]
    kwin_ref[pl.ds(wbuf, ts), :] = neww_ref[...]
    kwin_ref[pl.ds(wbuf + ts, LANES - ts), :] = jnp.zeros((LANES - ts, 2 * KV_W), F32)
    win_ref[...] = kwin_ref[pl.ds(ts, wbuf), :]

    tok = past + lax.broadcasted_iota(jnp.int32, (T_PAD, 1), 0)
    qpos_rows = jnp.concatenate([tok] * gqa, axis=0)
    kpos = lax.broadcasted_iota(jnp.int32, (1, ktot), 1)
    widx = lax.broadcasted_iota(jnp.int32, (1, wbuf + LANES), 1)
    wpos = past - wbuf + widx
    dist = qpos_rows - wpos
    wmask = (dist >= 0) & (dist < WINDOW) & (wpos >= 0) & (widx < wbuf + ts)

    ys = []
    for g in range(N_KV):
        cmp_kv = []
        for br in range(2):
            s = br * N_KV + g
            load_sub = lambda r, s=s: full_ref[s, pl.ds(r, nsp, stride=CMP_STRIDE), :]
            cmp_kv.append(_compress_block(load_sub, pe_ref[br], w1_ref[br], w2_ref[br], nsp))
        qg = q_ref[g].astype(BF16)
        o_cmp, imp = _compressed_branch(qg, cmp_kv[0], cmp_kv[1], qpos_rows, nc, gqa, scale)
        sel = _select_blocks(imp, tok, nb)
        in_sel = jnp.where(kpos <= tok, _dot(sel, exp_ref[...]), 0.0)
        mk = jnp.concatenate([in_sel] * gqa, axis=0) > 0.5
        ks = full_ref[2 * N_KV + g].astype(BF16)
        vs = full_ref[3 * N_KV + g].astype(BF16)
        p = _masked_softmax(_dot_nt(qg, ks) * scale, mk)
        o_slc = _dot(p.astype(BF16), vs)
        kw = kwin_ref[:, g * HEAD_DIM:(g + 1) * HEAD_DIM].astype(BF16)
        vw = kwin_ref[:, KV_W + g * HEAD_DIM:KV_W + (g + 1) * HEAD_DIM].astype(BF16)
        p = _masked_softmax(_dot_nt(qg, kw) * scale, wmask)
        o_win = _dot(p.astype(BF16), vw)
        gcol = gcol_ref[g]
        o = gcol[:, 0:1] * o_cmp + gcol[:, 1:2] * o_slc + gcol[:, 2:3] * o_win
        ys.append(o * _silu(z_ref[g]))

    ss = None
    for y in ys:
        rs = jnp.sum(y * y, axis=-1, keepdims=True)
        for r in range(gqa):
            part = rs[r * T_PAD:(r + 1) * T_PAD]
            ss = part if ss is None else ss + part
    inv = jnp.concatenate([lax.rsqrt(ss / att_w + EPS)] * gqa, axis=0)
    for g in range(N_KV):
        o_ref[g] = (ys[g] * inv) * ggrp_ref[g]


def _attention_sample(layer, page_table, cache4, cache_win4, u3, q_r, gcol_r, z_r, ggrp_r,
                      pe, w1, w2, expand, cols, att_w):
    db, ts, _ = u3.shape
    n_pages = page_table.shape[1]
    page = cache4.shape[2]
    past = n_pages * page
    wbuf = cache_win4.shape[2]
    gqa = att_w // HEAD_DIM // N_KV
    nsub = -(-(past + ts) // CMP_STRIDE)
    nc = nsub - CMP_LEN // CMP_STRIDE + 1
    nsp = -(-nsub // 8) * 8
    nb = max(-(-(past + ts) // SEL_BLOCK), N_SELECT)
    rows = gqa * T_PAD
    kern = functools.partial(_attn_sample_kernel, n_pages=n_pages, page=page, ts=ts, wbuf=wbuf,
                             nsp=nsp, nc=nc, nb=nb, gqa=gqa, att_w=att_w)
    four_kv, two_kv = 4 * KV_W, 2 * KV_W
    per_b = lambda shape: pl.BlockSpec((None,) + shape, lambda b, pt: (b,) + (0,) * len(shape))
    const = lambda shape: pl.BlockSpec(shape, lambda b, pt: (0,) * len(shape))
    page_spec = lambda p: pl.BlockSpec((None, None, page, four_kv),
                                       lambda b, pt, p=p: (layer, pt[b * n_pages + p], 0, 0))
    in_specs = [
        per_b((N_KV, rows, HEAD_DIM)),
        per_b((N_KV, rows, HEAD_DIM)),
        per_b((N_KV, rows, HEAD_DIM)),
        const((N_KV, rows, HEAD_DIM)),
        pl.BlockSpec((None, ts, four_kv), lambda b, pt: (b, 0, cols["kv"] // four_kv)),
        pl.BlockSpec((None, ts, two_kv), lambda b, pt: (b, 0, cols["kvw"] // two_kv)),
        pl.BlockSpec((None, None, wbuf, two_kv), lambda b, pt: (layer, b, 0, 0)),
        const((2, CMP_LEN, HEAD_DIM)),
        const((2, CMP_LEN * HEAD_DIM, HEAD_DIM)),
        const((2, HEAD_DIM, HEAD_DIM)),
        const((LANES, nsp * CMP_STRIDE)),
    ] + [page_spec(p) for p in range(n_pages)]
    grid_spec = pltpu.PrefetchScalarGridSpec(
        num_scalar_prefetch=1,
        grid=(db,),
        in_specs=in_specs,
        out_specs=[per_b((N_KV, rows, HEAD_DIM)), per_b((wbuf, two_kv))],
        scratch_shapes=[pltpu.VMEM((4 * N_KV, nsp * CMP_STRIDE, HEAD_DIM), F32),
                        pltpu.VMEM((wbuf + LANES, two_kv), F32)],
    )
    return pl.pallas_call(
        kern,
        grid_spec=grid_spec,
        out_shape=[jax.ShapeDtypeStruct((db, N_KV, rows, HEAD_DIM), F32),
                   jax.ShapeDtypeStruct((db, wbuf, two_kv), F32)],
        compiler_params=_cparams(("arbitrary",)),
        name="attn_sample",
    )(page_table.reshape(-1), q_r, gcol_r, z_r, ggrp_r, u3, u3, cache_win4, pe, w1, w2, expand,
      *([cache4] * n_pages))


def _merge_kernel(ya_ref, yb_ref, x_ref, gate_ref, w_ref, g_ref, o_ref, *, conv_ch):
    y = (_dot(ya_ref[...].astype(BF16), w_ref[:conv_ch, :])
         + _dot(yb_ref[...].astype(BF16), w_ref[conv_ch:, :]))
    yn = y * lax.rsqrt(jnp.mean(y * y, axis=-1, keepdims=True) + EPS)
    o_ref[...] = x_ref[...] + gate_ref[...] * (yn * g_ref[...])


def _merge(ya, yb, x2d, gate, w_out, g_post, tm):
    r, d = x2d.shape
    conv_ch = ya.shape[1]
    tiles_per_gate = (r // tm) // gate.shape[0]
    s_rows = gate.shape[1]
    return pl.pallas_call(
        functools.partial(_merge_kernel, conv_ch=conv_ch),
        grid=(r // tm,),
        in_specs=[
            pl.BlockSpec((tm, conv_ch), lambda i: (i, 0)),
            pl.BlockSpec((tm, d - conv_ch), lambda i: (i, 0)),
            pl.BlockSpec((tm, d), lambda i: (i, 0)),
            pl.BlockSpec((None, s_rows, d), lambda i: (i // tiles_per_gate, 0, 0)),
            pl.BlockSpec((d, d), lambda i: (0, 0)),
            pl.BlockSpec((1, d), lambda i: (0, 0)),
        ],
        out_specs=pl.BlockSpec((tm, d), lambda i: (i, 0)),
        out_shape=jax.ShapeDtypeStruct((r, d), F32),
        compiler_params=_cparams(("parallel",)),
        name="merge",
    )(ya, yb, x2d, gate, w_out, g_post)


def _rope_tables(pos):
    half = HEAD_DIM // 2
    inv = jnp.power(ROPE_THETA, -jnp.arange(half, dtype=F32) / half)
    ang = pos.astype(F32)[:, None] * inv
    cos, sin = jnp.cos(ang), jnp.sin(ang)
    return jnp.concatenate([cos, cos], axis=-1), jnp.concatenate([-sin, sin], axis=-1)


def _expand_matrix(n_keys):
    blk = jnp.arange(n_keys, dtype=jnp.int32) // SEL_BLOCK
    return (blk[None, :] == jnp.arange(LANES, dtype=jnp.int32)[:, None]).astype(BF16)


def kernel(x_prompt, x_sample, cache_kv, cache_win, state_conv, page_table, c_prompt, c_sample,
           w_ada, b_ada, g_pre, w_in, conv_w, cmp_pe, cmp_w1, cmp_w2, g_grp, w_out, g_post):
    bp, tp, d = x_prompt.shape
    bs, ts, _ = x_sample.shape
    depth = w_in.shape[0]
    conv_ch = conv_w.shape[2]
    att_w = d - conv_ch
    n_heads = att_w // HEAD_DIM
    gqa = n_heads // N_KV
    n_pages, page = page_table.shape[1], cache_kv.shape[2]
    past = n_pages * page
    wbuf = cache_win.shape[2]
    assert tp % TQ == 0 and tp % KC_SEL == 0 and ts <= T_PAD
    assert att_w == 4 * KV_W and 3 * n_heads <= LANES

    c_q = 4 * conv_ch
    c_kv = c_q + att_w
    c_zb = c_kv + 4 * KV_W
    c_kvw = c_zb + att_w
    cols = {"q": c_q, "kv": c_kv, "kvs": c_kv + 2 * KV_W, "zb": c_zb, "kvw": c_kvw}
    o_kw = c_kv + 4 * KV_W
    o_g = o_kw + 2 * KV_W
    o_zb = o_g + 3 * n_heads
    w_main = jnp.concatenate([w_in[:, :, :o_kw], w_in[:, :, o_zb:], w_in[:, :, o_kw:o_g]], axis=-1).astype(BF16)
    w_gate = jnp.pad(w_in[:, :, o_g:o_zb], ((0, 0), (0, 0), (0, LANES - 3 * n_heads))).astype(BF16)
    w_out_b = w_out.astype(BF16)
    w1_b = cmp_w1.astype(BF16)
    w2_b = cmp_w2.astype(BF16)

    mod = _modulation(jnp.concatenate([c_prompt, c_sample], axis=0), w_ada, b_ada)

    cos_p, sin_p = _rope_tables(jnp.arange(tp))
    pos_s = past + jnp.arange(ts)
    cos_s, sin_s = (jnp.tile(a, (bs, 1)) for a in _rope_tables(pos_s))
    expand_p = _expand_matrix(tp)
    nsub_s = -(-(past + ts) // CMP_STRIDE)
    nsp_s = -(-nsub_s // 8) * 8
    expand_s = _expand_matrix(nsp_s * CMP_STRIDE)

    tm_p = min(tp, 1024)
    rows_s = bs * ts
    cache4 = cache_kv.reshape(depth, cache_kv.shape[1], page, 4 * KV_W)
    cache_win4 = cache_win.reshape(depth, bs, wbuf, 2 * KV_W)

    def to_heads(a):
        a = a.reshape(bs, ts, N_KV, gqa, HEAD_DIM).transpose(0, 2, 3, 1, 4)
        a = jnp.pad(a, ((0, 0), (0, 0), (0, 0), (0, T_PAD - ts), (0, 0)))
        return a.reshape(bs, N_KV, gqa * T_PAD, HEAD_DIM)

    xp = x_prompt.reshape(bp * tp, d)
    xs = x_sample.reshape(rows_s, d)
    kv_p, kv_s, win_p, win_s, conv_p, conv_s = [], [], [], [], [], []
    for l in range(depth):
        shift, scale, gate = mod[l, :, :d], mod[l, :, d:2 * d], mod[l, :, 2 * d:]
        g_a, g_b = g_grp[l, None, :conv_ch], g_grp[l, None, conv_ch:]

        u, gates = _in_projection(xp, scale[:bp, None], shift[:bp, None], g_pre[l, None], cos_p, sin_p,
                                  w_main[l], w_gate[l], tm_p, conv_ch, att_w)
        u3 = u.reshape(bp, tp, -1)
        ya, cbuf = _short_conv(u3, jnp.zeros((bp, CONV_WIDTH - 1, conv_ch), F32), conv_w[l], g_a,
                               min(tp, 512), BF16)
        kcmp = _compress_prompt(u3, c_kv, cmp_pe[l], w1_b[l], w2_b[l])
        yb = _attention_prompt(u3, gates.reshape(bp, tp, LANES), kcmp, g_b, expand_p, cols, att_w)
        xp = _merge(ya.reshape(bp * tp, conv_ch), yb.reshape(bp * tp, att_w), xp, gate[:bp, None],
                    w_out_b[l], g_post[l, None], min(tp, 512))
        kv_p.append(u3[:, :, c_kv:c_kv + 4 * KV_W].reshape(bp, tp, 4, N_KV, HEAD_DIM))
        win_p.append(u3[:, tp - min(WINDOW, tp):, c_kvw:].reshape(bp, -1, 2, N_KV, HEAD_DIM))
        conv_p.append(cbuf)

        rep = lambda a: jnp.repeat(a[bp:], ts, axis=0)[None]
        u, gates = _in_projection(xs, rep(scale), rep(shift), g_pre[l, None], cos_s, sin_s,
                                  w_main[l], w_gate[l], rows_s, conv_ch, att_w)
        u3 = u.reshape(bs, ts, -1)
        ya, cbuf = _short_conv(u3, state_conv[l], conv_w[l], g_a, ts, F32)
        q_r = to_heads(u[:, c_q:c_q + att_w])
        z_r = to_heads(u[:, c_zb:c_zb + att_w])
        gcol = gates[:, :3 * n_heads].reshape(rows_s, n_heads, 3)
        gcol_r = to_heads(jnp.pad(gcol, ((0, 0), (0, 0), (0, HEAD_DIM - 3))).reshape(rows_s, att_w))
        ggrp_r = jnp.broadcast_to(g_b.reshape(N_KV, gqa, 1, HEAD_DIM),
                                  (N_KV, gqa, T_PAD, HEAD_DIM)).reshape(N_KV, gqa * T_PAD, HEAD_DIM)
        yb_r, win = _attention_sample(l, page_table, cache4, cache_win4, u3, q_r, gcol_r, z_r, ggrp_r,
                                      cmp_pe[l], w1_b[l], w2_b[l], expand_s, cols, att_w)
        yb = yb_r.reshape(bs, N_KV, gqa, T_PAD, HEAD_DIM)[:, :, :, :ts].transpose(0, 3, 1, 2, 4)
        xs = _merge(ya.reshape(rows_s, conv_ch), yb.reshape(rows_s, att_w), xs, rep(gate),
                    w_out_b[l], g_post[l, None], rows_s)
        kv_s.append(u3[:, :, c_kv:c_kv + 4 * KV_W].reshape(bs, ts, 4, N_KV, HEAD_DIM))
        win_s.append(win.reshape(bs, wbuf, 2, N_KV, HEAD_DIM))
        conv_s.append(cbuf)

    return (xp.reshape(bp, tp, d), xs.reshape(bs, ts, d), jnp.stack(kv_p), jnp.stack(kv_s),
            jnp.stack(win_p), jnp.stack(win_s), jnp.stack(conv_p), jnp.stack(conv_s))
```

```python
import functools

import jax
import jax.numpy as jnp
from jax import lax
from jax.experimental import pallas as pl
from jax.experimental.pallas import tpu as pltpu

F32 = jnp.float32
BF16 = jnp.bfloat16

HEAD_DIM = 128
N_KV = 2
CONV_WIDTH = 3
CMP_LEN = 32
CMP_STRIDE = 16
SEL_BLOCK = 64
N_SELECT = 8
WINDOW = 512
ROPE_THETA = 10000.0
EPS = 1e-6
NEG = -1e30
FORCE_SCORE = 1e3
KV_W = N_KV * HEAD_DIM
LANES = 128
VMEM_LIMIT = 52 * 1024 * 1024

TQ = 128
KC_SEL = 512
T_PAD = 8
SUB_PITCH = 24


def _cparams(sem):
    return pltpu.CompilerParams(dimension_semantics=sem, vmem_limit_bytes=VMEM_LIMIT)


def _silu(x):
    return x / (1.0 + jnp.exp(-x))


def _sigmoid(x):
    return 1.0 / (1.0 + jnp.exp(-x))


def _dot(a, b):
    return jnp.dot(a, b, preferred_element_type=F32)


def _dot_nt(a, b):
    return lax.dot_general(a, b, (((1,), (1,)), ((), ())), preferred_element_type=F32)


def _mod_kernel(c_ref, w_ref, b_ref, o_ref):
    a = _silu(c_ref[...]).astype(BF16)
    o_ref[...] = _dot(a, w_ref[...].astype(BF16)) + b_ref[...]


def _modulation(c_all, w_ada, b_ada):
    depth, d, n = w_ada.shape
    rows = c_all.shape[0]
    tn = 768
    return pl.pallas_call(
        _mod_kernel,
        grid=(depth, n // tn),
        in_specs=[
            pl.BlockSpec((rows, d), lambda l, j: (0, 0)),
            pl.BlockSpec((None, d, tn), lambda l, j: (l, 0, j)),
            pl.BlockSpec((None, 1, tn), lambda l, j: (l, 0, j)),
        ],
        out_specs=pl.BlockSpec((None, rows, tn), lambda l, j: (l, 0, j)),
        out_shape=jax.ShapeDtypeStruct((depth, rows, n), F32),
        compiler_params=_cparams(("parallel", "parallel")),
        name="mod",
    )(c_all, w_ada, b_ada.reshape(depth, 1, n))


def _rope_cols(v, cos, sin):
    outs = []
    for h in range(v.shape[1] // HEAD_DIM):
        xh = v[:, h * HEAD_DIM:(h + 1) * HEAD_DIM]
        outs.append(xh * cos + pltpu.roll(xh, HEAD_DIM // 2, 1) * sin)
    return outs[0] if len(outs) == 1 else jnp.concatenate(outs, axis=1)


def _inproj_kernel(x_ref, scale_ref, shift_ref, g_ref, cos_ref, sin_ref, w_ref, wg_ref,
                   u_ref, gates_ref, h_ref, *, full_lo, full_hi, half_tiles):
    j = pl.program_id(1)

    @pl.when(j == 0)
    def _():
        x = x_ref[...]
        y = x * lax.rsqrt(jnp.mean(x * x, axis=-1, keepdims=True) + EPS)
        h = (y * g_ref[...]) * (1.0 + scale_ref[...]) + shift_ref[...]
        hb = h.astype(BF16)
        h_ref[...] = hb
        gates_ref[...] = _sigmoid(_dot(hb, wg_ref[...]))

    acc = _dot(h_ref[...], w_ref[...])
    is_full = (j >= full_lo) & (j < full_hi)
    is_half = functools.reduce(lambda a, b: a | b, [j == t for t in half_tiles])

    @pl.when(is_full)
    def _():
        u_ref[...] = _rope_cols(acc, cos_ref[...], sin_ref[...])

    @pl.when(is_half)
    def _():
        u_ref[...] = jnp.concatenate(
            [_rope_cols(acc[:, :KV_W], cos_ref[...], sin_ref[...]), acc[:, KV_W:]], axis=1)

    @pl.when(jnp.logical_not(is_full | is_half))
    def _():
        u_ref[...] = acc


def _in_projection(x2d, scale, shift, g_pre, cos, sin, w_main, w_gate, tm, conv_ch, att_w):
    r, d = x2d.shape
    nm = w_main.shape[1]
    tn = 2 * KV_W
    tiles_per_scale = (r // tm) // scale.shape[0]
    s_rows = scale.shape[1]
    pos_tiles = cos.shape[0] // tm
    q0 = 4 * conv_ch // tn
    q1 = q0 + att_w // tn
    half_tiles = (q1, q1 + 1, q1 + 2 + att_w // tn)
    kern = functools.partial(_inproj_kernel, full_lo=q0, full_hi=q1, half_tiles=half_tiles)
    return pl.pallas_call(
        kern,
        grid=(r // tm, nm // tn),
        in_specs=[
            pl.BlockSpec((tm, d), lambda i, j: (i, 0)),
            pl.BlockSpec((None, s_rows, d), lambda i, j: (i // tiles_per_scale, 0, 0)),
            pl.BlockSpec((None, s_rows, d), lambda i, j: (i // tiles_per_scale, 0, 0)),
            pl.BlockSpec((1, d), lambda i, j: (0, 0)),
            pl.BlockSpec((tm, HEAD_DIM), lambda i, j: (i % pos_tiles, 0)),
            pl.BlockSpec((tm, HEAD_DIM), lambda i, j: (i % pos_tiles, 0)),
            pl.BlockSpec((d, tn), lambda i, j: (0, j)),
            pl.BlockSpec((d, LANES), lambda i, j: (0, 0)),
        ],
        out_specs=[
            pl.BlockSpec((tm, tn), lambda i, j: (i, j)),
            pl.BlockSpec((tm, LANES), lambda i, j: (i, 0)),
        ],
        out_shape=[jax.ShapeDtypeStruct((r, nm), F32), jax.ShapeDtypeStruct((r, LANES), F32)],
        scratch_shapes=[pltpu.VMEM((tm, d), BF16)],
        compiler_params=_cparams(("parallel", "arbitrary")),
        name="inproj",
    )(x2d, scale, shift, g_pre, cos, sin, w_main, w_gate)


def _conv_kernel(b_ref, c_ref, x_ref, z_ref, init_ref, w_ref, g_ref, ya_ref, st_ref, up_ref, *, tr):
    i = pl.program_id(1)
    pad = 8

    @pl.when(i == 0)
    def _():
        up_ref[pl.ds(pad - 2, 2), :] = init_ref[...]

    up_ref[pl.ds(pad, tr), :] = c_ref[...] * x_ref[...]
    w = w_ref[...]
    y = (w[0:1, :] * up_ref[pl.ds(pad - 2, tr), :]
         + w[1:2, :] * up_ref[pl.ds(pad - 1, tr), :]
         + w[2:3, :] * up_ref[pl.ds(pad, tr), :])
    ya = _silu(z_ref[...]) * (b_ref[...] * y)
    yn = ya * lax.rsqrt(jnp.mean(ya * ya, axis=-1, keepdims=True) + EPS)
    ya_ref[...] = (yn * g_ref[...]).astype(ya_ref.dtype)
    last = up_ref[pl.ds(pad + tr - 2, 2), :]
    st_ref[...] = last
    up_ref[pl.ds(pad - 2, 2), :] = last


def _short_conv(u3, init_state, conv_w, g_a, tr, out_dtype):
    b, t, _ = u3.shape
    c = conv_w.shape[1]
    col = lambda k: pl.BlockSpec((None, tr, c), lambda bi, i, k=k: (bi, i, k))
    return pl.pallas_call(
        functools.partial(_conv_kernel, tr=tr),
        grid=(b, t // tr),
        in_specs=[
            col(0), col(1), col(2), col(3),
            pl.BlockSpec((None, CONV_WIDTH - 1, c), lambda bi, i: (bi, 0, 0)),
            pl.BlockSpec((CONV_WIDTH, c), lambda bi, i: (0, 0)),
            pl.BlockSpec((1, c), lambda bi, i: (0, 0)),
        ],
        out_specs=[
            pl.BlockSpec((None, tr, c), lambda bi, i: (bi, i, 0)),
            pl.BlockSpec((None, CONV_WIDTH - 1, c), lambda bi, i: (bi, 0, 0)),
        ],
        out_shape=[jax.ShapeDtypeStruct((b, t, c), out_dtype),
                   jax.ShapeDtypeStruct((b, CONV_WIDTH - 1, c), F32)],
        scratch_shapes=[pltpu.VMEM((tr + 8, c), F32)],
        compiler_params=_cparams(("parallel", "arbitrary")),
        name="conv",
    )(u3, u3, u3, u3, init_state, conv_w, g_a)


def _pe_term(pe, w1cat):
    flat = lambda lo: jnp.concatenate([pe[lo + r:lo + r + 1, :] for r in range(CMP_STRIDE)], axis=1)
    rows = jnp.concatenate([flat(0), flat(CMP_STRIDE), jnp.zeros((6, CMP_STRIDE * HEAD_DIM), F32)], axis=0)
    r = _dot(rows.astype(BF16), w1cat)
    return r[0:1, :HEAD_DIM] + r[1:2, HEAD_DIM:]


def _compress_block(load_sub, pe_term, w1cat, w2, nsp):
    a = jnp.concatenate([load_sub(r).astype(BF16) for r in range(CMP_STRIDE)], axis=1)
    pq = _dot(a, w1cat)
    pre = pq[:, :HEAD_DIM] + pltpu.roll(pq[:, HEAD_DIM:], nsp - 1, 0) + pe_term
    return _dot(_silu(pre).astype(BF16), w2)


def _compress_kernel(k_ref, pe_ref, w1_ref, w2_ref, o_ref, *, nsp):
    load_sub = lambda r: k_ref[pl.ds(r, nsp, stride=CMP_STRIDE), :]
    w1cat = w1_ref[...]
    o_ref[...] = _compress_block(load_sub, _pe_term(pe_ref[...], w1cat), w1cat, w2_ref[...], nsp)


def _compress_prompt(u3, kv_col0, cmp_pe, cmp_w1, cmp_w2):
    b, t, _ = u3.shape
    nsp = t // CMP_STRIDE
    blk0 = kv_col0 // HEAD_DIM
    return pl.pallas_call(
        functools.partial(_compress_kernel, nsp=nsp),
        grid=(b, 2 * N_KV),
        in_specs=[
            pl.BlockSpec((None, t, HEAD_DIM), lambda bi, s: (bi, 0, blk0 + s)),
            pl.BlockSpec((None, CMP_LEN, HEAD_DIM), lambda bi, s: (s // N_KV, 0, 0)),
            pl.BlockSpec((None, CMP_STRIDE * HEAD_DIM, 2 * HEAD_DIM), lambda bi, s: (s // N_KV, 0, 0)),
            pl.BlockSpec((None, HEAD_DIM, HEAD_DIM), lambda bi, s: (s // N_KV, 0, 0)),
        ],
        out_specs=pl.BlockSpec((None, None, nsp, HEAD_DIM), lambda bi, s: (bi, s, 0, 0)),
        out_shape=jax.ShapeDtypeStruct((b, 2 * N_KV, nsp, HEAD_DIM), F32),
        compiler_params=_cparams(("parallel", "parallel")),
        name="compress",
    )(u3, cmp_pe, cmp_w1, cmp_w2)


def _bias(mask):
    return jnp.where(mask, 0.0, NEG)


def _row_max(s):
    return jnp.maximum(jnp.max(s, axis=-1, keepdims=True), 0.1 * NEG)


def _softmax_parts(s):
    e = jnp.exp(s - _row_max(s))
    return e, jnp.maximum(jnp.sum(e, axis=-1, keepdims=True), 1e-30)


def _fold_lanes(x, op):
    out = x[:, :LANES]
    for c in range(1, x.shape[1] // LANES):
        out = op(out, x[:, c * LANES:(c + 1) * LANES])
    return out


def _compressed_branch(qg, kc, vc, qpos_rows, nc, n_rep):
    ncp = min(kc.shape[0], -(-nc // LANES) * LANES)
    assert ncp >= nc
    kc, vc = kc[:ncp], vc[:ncp]
    cidx = lax.broadcasted_iota(jnp.int32, (1, ncp), 1)
    mask = ((cidx * CMP_STRIDE + (CMP_LEN - 1)) <= qpos_rows) & (cidx < nc)
    e, l = _softmax_parts(_dot_nt(qg, kc.astype(BF16)) + _bias(mask))
    p = e / l
    o = _dot(p.astype(BF16), vc.astype(BF16))
    rt = qg.shape[0] // n_rep
    p_tok = p[0:rt]
    for r in range(1, n_rep):
        p_tok = p_tok + p[r * rt:(r + 1) * rt]
    return o, p_tok


def _pool_matrix(rows, cols, blocks_on_rows):
    per = SEL_BLOCK // CMP_STRIDE
    r = lax.broadcasted_iota(jnp.int32, (rows, cols), 0)
    c = lax.broadcasted_iota(jnp.int32, (rows, cols), 1)
    hit = (c // per == r) if blocks_on_rows else (r // per == c)
    return jnp.where(hit, 1.0, 0.0).astype(F32)


def _top_blocks(score, blk, nb, axis):
    rank = jnp.zeros(score.shape, jnp.int32)
    for j in range(nb):
        one = score[j:j + 1, :] if axis == 0 else score[:, j:j + 1]
        beats = (one > score) | ((one == score) & (blk > j))
        rank = rank + jnp.where(beats, 1, 0)
    return jnp.where((rank < N_SELECT) & (score > 0.5 * NEG), 1.0, 0.0)


def _block_scores(imp, blk, cur):
    forced = (blk == 0) | (blk == cur) | (blk == cur - 1)
    return jnp.where(blk <= cur, jnp.where(forced, FORCE_SCORE, imp), NEG)


def _select_blocks_rows(p_tok, tok_row, nb):
    rt, ncp = p_tok.shape
    nbr = -(-nb // 8) * 8
    imp_t = lax.dot_general(_pool_matrix(nbr, ncp, True), p_tok, (((1,), (1,)), ((), ())),
                            preferred_element_type=F32, precision=lax.Precision.HIGHEST)
    blk = lax.broadcasted_iota(jnp.int32, (nbr, rt), 0)
    sel_t = _top_blocks(_block_scores(imp_t, blk, tok_row // SEL_BLOCK), blk, nb, 0)
    if nbr < LANES:
        sel_t = jnp.concatenate([sel_t, jnp.zeros((LANES - nbr, rt), F32)], axis=0)
    return sel_t.T.astype(BF16)


def _select_blocks_lanes(p_tok, tok_col, nb):
    rt, ncp = p_tok.shape
    imp = jnp.dot(p_tok, _pool_matrix(ncp, LANES, False), preferred_element_type=F32,
                  precision=lax.Precision.HIGHEST)
    blk = lax.broadcasted_iota(jnp.int32, (rt, LANES), 1)
    return _top_blocks(_block_scores(imp, blk, tok_col // SEL_BLOCK), blk, nb, 1).astype(BF16)


def _attn_prompt_kernel(q_ref, kvs_ref, kvw_ref, cmp_ref, gates_ref, z_ref, ggrp_ref, exp_ref, o_ref,
                        s_ref, *, nc, nb, gqa, att_w, t, wk):
    i = pl.program_id(1)
    s0 = i * TQ
    scale = HEAD_DIM ** -0.5
    tok = lax.broadcasted_iota(jnp.int32, (TQ, 1), 0) + s0
    tok_row = lax.broadcasted_iota(jnp.int32, (1, TQ), 1) + s0
    qpos_rows = jnp.concatenate([tok] * gqa, axis=0)
    rows = TQ * gqa
    rep = lambda a: jnp.concatenate([a] * gqa, axis=0)
    kcol = lambda g: slice(g * HEAD_DIM, (g + 1) * HEAD_DIM)
    vcol = lambda g: slice(KV_W + g * HEAD_DIM, KV_W + (g + 1) * HEAD_DIM)
    groups = range(N_KV)

    qs = [(jnp.concatenate(
        [q_ref[:, (g * gqa + r) * HEAD_DIM:(g * gqa + r + 1) * HEAD_DIM] for r in range(gqa)],
        axis=0) * scale).astype(BF16) for g in groups]

    o_cmp, sel = [], []
    for g in groups:
        o, p_tok = _compressed_branch(qs[g], cmp_ref[g], cmp_ref[N_KV + g], qpos_rows, nc, gqa)
        o_cmp.append(o)
        sel.append(_select_blocks_rows(p_tok, tok_row, nb))

    w_lo = pl.multiple_of(jnp.clip(s0 - WINDOW, 0, t - wk), TQ)
    wpos = w_lo + lax.broadcasted_iota(jnp.int32, (1, wk), 1)
    wdist = tok - wpos
    wbias = rep(_bias((wdist >= 0) & (wdist < WINDOW)))
    o_win = []
    for g in groups:
        kw = kvw_ref[pl.ds(w_lo, wk), kcol(g)].astype(BF16)
        vw = kvw_ref[pl.ds(w_lo, wk), vcol(g)].astype(BF16)
        e, l = _softmax_parts(_dot_nt(qs[g], kw) + wbias)
        o_win.append(_dot(e.astype(BF16), vw) / l)

    n_sel = (s0 + TQ + KC_SEL - 1) // KC_SEL

    def score_body(ci, mruns):
        k0 = pl.multiple_of(ci * KC_SEL, KC_SEL)
        causal = (k0 + lax.broadcasted_iota(jnp.int32, (1, KC_SEL), 1)) <= tok
        out = []
        for g in groups:
            k = kvs_ref[pl.ds(k0, KC_SEL), kcol(g)].astype(BF16)
            in_sel = _dot(sel[g], exp_ref[:, pl.ds(k0, KC_SEL)])
            s = _dot_nt(qs[g], k) + rep(_bias((in_sel > 0.5) & causal))
            s_ref[g, :, pl.ds(k0, KC_SEL)] = s
            out.append(jnp.maximum(mruns[g], _fold_lanes(s, jnp.maximum)))
        return tuple(out)

    mruns = lax.fori_loop(0, n_sel, score_body, tuple(jnp.full((rows, LANES), NEG, F32) for _ in groups))
    ms = [_row_max(mruns[g]) for g in groups]

    def value_body(ci, carry):
        k0 = pl.multiple_of(ci * KC_SEL, KC_SEL)
        out = []
        for g in groups:
            lrun, acc = carry[g]
            v = kvs_ref[pl.ds(k0, KC_SEL), vcol(g)].astype(BF16)
            e = jnp.exp(s_ref[g, :, pl.ds(k0, KC_SEL)] - ms[g])
            out.append((lrun + _fold_lanes(e, jnp.add), acc + _dot(e.astype(BF16), v)))
        return tuple(out)

    init = tuple((jnp.zeros((rows, LANES), F32), jnp.zeros((rows, HEAD_DIM), F32)) for _ in groups)
    sums = lax.fori_loop(0, n_sel, value_body, init)

    heads = []
    for g in groups:
        lrun, acc = sums[g]
        o_slc = acc / jnp.maximum(jnp.sum(lrun, axis=-1, keepdims=True), 1e-30)
        for r in range(gqa):
            h = g * gqa + r
            sl = slice(r * TQ, (r + 1) * TQ)
            o = (gates_ref[:, 3 * h:3 * h + 1] * o_cmp[g][sl] + gates_ref[:, 3 * h + 1:3 * h + 2] * o_slc[sl]
                 + gates_ref[:, 3 * h + 2:3 * h + 3] * o_win[g][sl])
            heads.append(o * _silu(z_ref[:, h * HEAD_DIM:(h + 1) * HEAD_DIM]))

    ss = jnp.sum(heads[0] * heads[0], axis=-1, keepdims=True)
    for y in heads[1:]:
        ss = ss + jnp.sum(y * y, axis=-1, keepdims=True)
    inv = lax.rsqrt(ss / att_w + EPS)
    o_ref[...] = jnp.concatenate(
        [(y * inv) * ggrp_ref[:, h * HEAD_DIM:(h + 1) * HEAD_DIM] for h, y in enumerate(heads)],
        axis=1).astype(o_ref.dtype)


def _attention_prompt(u3, gates3, kcmp, g_b, expand, cols, att_w):
    b, t, _ = u3.shape
    gqa = att_w // HEAD_DIM // N_KV
    nsub = t // CMP_STRIDE
    nc = nsub - CMP_LEN // CMP_STRIDE + 1
    nb = max(-(-t // SEL_BLOCK), N_SELECT)
    wk = min(t, WINDOW + TQ)
    kern = functools.partial(_attn_prompt_kernel, nc=nc, nb=nb, gqa=gqa, att_w=att_w, t=t, wk=wk)
    two_kv = 2 * KV_W
    return pl.pallas_call(
        kern,
        grid=(b, t // TQ),
        in_specs=[
            pl.BlockSpec((None, TQ, att_w), lambda bi, i: (bi, i, cols["q"] // att_w)),
            pl.BlockSpec((None, t, two_kv), lambda bi, i: (bi, 0, cols["kvs"] // two_kv)),
            pl.BlockSpec((None, t, two_kv), lambda bi, i: (bi, 0, cols["kvw"] // two_kv)),
            pl.BlockSpec((None, 2 * N_KV, kcmp.shape[2], HEAD_DIM), lambda bi, i: (bi, 0, 0, 0)),
            pl.BlockSpec((None, TQ, LANES), lambda bi, i: (bi, i, 0)),
            pl.BlockSpec((None, TQ, att_w), lambda bi, i: (bi, i, cols["zb"] // att_w)),
            pl.BlockSpec((1, att_w), lambda bi, i: (0, 0)),
            pl.BlockSpec((LANES, t), lambda bi, i: (0, 0)),
        ],
        out_specs=pl.BlockSpec((None, TQ, att_w), lambda bi, i: (bi, i, 0)),
        out_shape=jax.ShapeDtypeStruct((b, t, att_w), BF16),
        scratch_shapes=[pltpu.VMEM((N_KV, TQ * gqa, t), F32)],
        compiler_params=_cparams(("parallel", "arbitrary")),
        name="attn_prompt",
    )(u3, u3, u3, kcmp, gates3, u3, g_b, expand)


def _attn_sample_kernel(pt_ref, q_ref, gcol_ref, z_ref, ggrp_ref, newkv_ref, neww_ref, cw_ref,
                        pe_ref, w1_ref, w2_ref, exp_ref, *rest,
                        n_pages, page, ts, wbuf, nsp, nc, nb, gqa, att_w):
    page_refs = rest[:n_pages]
    o_ref, win_ref, sub_ref, full_ref, kwin_ref, pe_term_ref = rest[n_pages:]
    del pt_ref
    n_kv_streams, n_win_streams, n_cmp_streams = 4 * N_KV, 2 * N_KV, 2 * N_KV
    past = n_pages * page
    ktot = nsp * CMP_STRIDE
    scale = HEAD_DIM ** -0.5
    rep = lambda a: jnp.concatenate([a] * gqa, axis=0)

    @pl.when(pl.program_id(0) == 0)
    def _():
        for br in range(2):
            pe_term_ref[br] = jnp.broadcast_to(_pe_term(pe_ref[br], w1_ref[br]), (8, HEAD_DIM))

    sub_per_page = page // CMP_STRIDE
    n_sub_past = past // CMP_STRIDE
    for s in range(n_cmp_streams):
        for p in range(n_pages):
            rows_p = page_refs[p][pl.ds(s, page, stride=n_kv_streams), :]
            for c in range(sub_per_page):
                sub_ref[s, pl.ds((p * sub_per_page + c) * SUB_PITCH, CMP_STRIDE), :] = (
                    rows_p[c * CMP_STRIDE:(c + 1) * CMP_STRIDE])
        sub_ref[s, pl.ds(n_sub_past * SUB_PITCH, (nsp - n_sub_past) * SUB_PITCH), :] = jnp.zeros(
            ((nsp - n_sub_past) * SUB_PITCH, HEAD_DIM), F32)
        sub_ref[s, pl.ds(n_sub_past * SUB_PITCH, ts), :] = newkv_ref[pl.ds(s, ts, stride=n_kv_streams), :]
    for s in range(n_kv_streams - n_cmp_streams):
        src = n_cmp_streams + s
        for p in range(n_pages):
            full_ref[s, pl.ds(p * page, page), :] = page_refs[p][pl.ds(src, page, stride=n_kv_streams), :]
        full_ref[s, pl.ds(past, ts), :] = newkv_ref[pl.ds(src, ts, stride=n_kv_streams), :]
        full_ref[s, pl.ds(past + ts, ktot - past - ts), :] = jnp.zeros((ktot - past - ts, HEAD_DIM), F32)
    for s in range(n_win_streams):
        kwin_ref[s, pl.ds(0, wbuf), :] = cw_ref[pl.ds(s, wbuf, stride=n_win_streams), :]
        kwin_ref[s, pl.ds(wbuf, ts), :] = neww_ref[pl.ds(s, ts, stride=n_win_streams), :]
        kwin_ref[s, pl.ds(wbuf + ts, LANES - ts), :] = jnp.zeros((LANES - ts, HEAD_DIM), F32)
    keep = (wbuf - ts) * n_win_streams
    win_ref[pl.ds(0, keep), :] = cw_ref[pl.ds(ts * n_win_streams, keep), :]
    win_ref[pl.ds(keep, ts * n_win_streams), :] = neww_ref[...]

    tok = past + lax.broadcasted_iota(jnp.int32, (T_PAD, 1), 0)
    qpos_rows = rep(tok)
    kpos = lax.broadcasted_iota(jnp.int32, (1, ktot), 1)
    widx = lax.broadcasted_iota(jnp.int32, (1, wbuf + LANES), 1)
    wpos = past - wbuf + widx
    wdist = tok - wpos
    wbias = rep(_bias((wdist >= 0) & (wdist < WINDOW) & (wpos >= 0) & (widx < wbuf + ts)))

    ys = []
    for g in range(N_KV):
        cmp_kv = []
        for br in range(2):
            s = br * N_KV + g
            load_sub = lambda r, s=s: sub_ref[s, pl.ds(r, nsp, stride=SUB_PITCH), :]
            cmp_kv.append(_compress_block(load_sub, pe_term_ref[br][0:1, :], w1_ref[br], w2_ref[br], nsp))
        qg = (q_ref[g] * scale).astype(BF16)
        o_cmp, p_tok = _compressed_branch(qg, cmp_kv[0], cmp_kv[1], qpos_rows, nc, gqa)
        sel = _select_blocks_lanes(p_tok, tok, nb)
        sbias = rep(_bias((_dot(sel, exp_ref[...]) > 0.5) & (kpos <= tok)))
        ks = full_ref[g].astype(BF16)
        vs = full_ref[N_KV + g].astype(BF16)
        e, l = _softmax_parts(_dot_nt(qg, ks) + sbias)
        o_slc = _dot(e.astype(BF16), vs) / l
        e, l = _softmax_parts(_dot_nt(qg, kwin_ref[g].astype(BF16)) + wbias)
        o_win = _dot(e.astype(BF16), kwin_ref[N_KV + g].astype(BF16)) / l
        gcol = gcol_ref[g]
        o = gcol[:, 0:1] * o_cmp + gcol[:, 1:2] * o_slc + gcol[:, 2:3] * o_win
        ys.append(o * _silu(z_ref[g]))

    ss = None
    for y in ys:
        rs = jnp.sum(y * y, axis=-1, keepdims=True)
        for r in range(gqa):
            part = rs[r * T_PAD:(r + 1) * T_PAD]
            ss = part if ss is None else ss + part
    inv = rep(lax.rsqrt(ss / att_w + EPS))
    for g in range(N_KV):
        o_ref[g] = (ys[g] * inv) * ggrp_ref[g]


def _attention_sample(layer, page_table, cache_rows, win_rows, newkv_rows, neww_rows, q_r, gcol_r, z_r,
                      ggrp_r, pe, w1, w2, expand, ts, att_w):
    db = q_r.shape[0]
    n_pages = page_table.shape[1]
    n_kv_streams, n_win_streams = 4 * N_KV, 2 * N_KV
    page = cache_rows.shape[2] // n_kv_streams
    past = n_pages * page
    wbuf = win_rows.shape[2] // n_win_streams
    gqa = att_w // HEAD_DIM // N_KV
    nsub = -(-(past + ts) // CMP_STRIDE)
    nc = nsub - CMP_LEN // CMP_STRIDE + 1
    nsp = -(-nsub // 8) * 8
    nb = max(-(-(past + ts) // SEL_BLOCK), N_SELECT)
    rows = gqa * T_PAD
    kern = functools.partial(_attn_sample_kernel, n_pages=n_pages, page=page, ts=ts, wbuf=wbuf,
                             nsp=nsp, nc=nc, nb=nb, gqa=gqa, att_w=att_w)
    per_b = lambda shape: pl.BlockSpec((None,) + shape, lambda b, pt: (b,) + (0,) * len(shape))
    const = lambda shape: pl.BlockSpec(shape, lambda b, pt: (0,) * len(shape))
    page_spec = lambda p: pl.BlockSpec((None, None, page * n_kv_streams, HEAD_DIM),
                                       lambda b, pt, p=p: (layer, pt[b * n_pages + p], 0, 0))
    in_specs = [
        per_b((N_KV, rows, HEAD_DIM)),
        per_b((N_KV, rows, HEAD_DIM)),
        per_b((N_KV, rows, HEAD_DIM)),
        const((N_KV, rows, HEAD_DIM)),
        per_b((ts * n_kv_streams, HEAD_DIM)),
        per_b((ts * n_win_streams, HEAD_DIM)),
        pl.BlockSpec((None, None, wbuf * n_win_streams, HEAD_DIM), lambda b, pt: (layer, b, 0, 0)),
        const((2, CMP_LEN, HEAD_DIM)),
        const((2, CMP_STRIDE * HEAD_DIM, 2 * HEAD_DIM)),
        const((2, HEAD_DIM, HEAD_DIM)),
        const((LANES, nsp * CMP_STRIDE)),
    ] + [page_spec(p) for p in range(n_pages)]
    grid_spec = pltpu.PrefetchScalarGridSpec(
        num_scalar_prefetch=1,
        grid=(db,),
        in_specs=in_specs,
        out_specs=[per_b((N_KV, rows, HEAD_DIM)), per_b((wbuf * n_win_streams, HEAD_DIM))],
        scratch_shapes=[pltpu.VMEM((2 * N_KV, nsp * SUB_PITCH, HEAD_DIM), F32),
                        pltpu.VMEM((2 * N_KV, nsp * CMP_STRIDE, HEAD_DIM), F32),
                        pltpu.VMEM((n_win_streams, wbuf + LANES, HEAD_DIM), F32),
                        pltpu.VMEM((2, 8, HEAD_DIM), F32)],
    )
    return pl.pallas_call(
        kern,
        grid_spec=grid_spec,
        out_shape=[jax.ShapeDtypeStruct((db, N_KV, rows, HEAD_DIM), F32),
                   jax.ShapeDtypeStruct((db, wbuf * n_win_streams, HEAD_DIM), F32)],
        compiler_params=_cparams(("arbitrary",)),
        name="attn_sample",
    )(page_table.reshape(-1), q_r, gcol_r, z_r, ggrp_r, newkv_rows, neww_rows, win_rows, pe, w1, w2, expand,
      *([cache_rows] * n_pages))


def _merge_kernel(ya_ref, yb_ref, x_ref, gate_ref, w_ref, g_ref, o_ref, *, conv_ch):
    y = (_dot(ya_ref[...].astype(BF16), w_ref[:conv_ch, :])
         + _dot(yb_ref[...].astype(BF16), w_ref[conv_ch:, :]))
    yn = y * lax.rsqrt(jnp.mean(y * y, axis=-1, keepdims=True) + EPS)
    o_ref[...] = x_ref[...] + gate_ref[...] * (yn * g_ref[...])


def _merge(ya, yb, x2d, gate, w_out, g_post, tm):
    r, d = x2d.shape
    conv_ch = ya.shape[1]
    tiles_per_gate = (r // tm) // gate.shape[0]
    s_rows = gate.shape[1]
    return pl.pallas_call(
        functools.partial(_merge_kernel, conv_ch=conv_ch),
        grid=(r // tm,),
        in_specs=[
            pl.BlockSpec((tm, conv_ch), lambda i: (i, 0)),
            pl.BlockSpec((tm, d - conv_ch), lambda i: (i, 0)),
            pl.BlockSpec((tm, d), lambda i: (i, 0)),
            pl.BlockSpec((None, s_rows, d), lambda i: (i // tiles_per_gate, 0, 0)),
            pl.BlockSpec((d, d), lambda i: (0, 0)),
            pl.BlockSpec((1, d), lambda i: (0, 0)),
        ],
        out_specs=pl.BlockSpec((tm, d), lambda i: (i, 0)),
        out_shape=jax.ShapeDtypeStruct((r, d), F32),
        compiler_params=_cparams(("parallel",)),
        name="merge",
    )(ya, yb, x2d, gate, w_out, g_post)


def _rope_tables(pos):
    half = HEAD_DIM // 2
    inv = jnp.power(ROPE_THETA, -jnp.arange(half, dtype=F32) / half)
    ang = pos.astype(F32)[:, None] * inv
    cos, sin = jnp.cos(ang), jnp.sin(ang)
    return jnp.concatenate([cos, cos], axis=-1), jnp.concatenate([-sin, sin], axis=-1)


def _expand_matrix(n_keys):
    blk = jnp.arange(n_keys, dtype=jnp.int32) // SEL_BLOCK
    return (blk[None, :] == jnp.arange(LANES, dtype=jnp.int32)[:, None]).astype(BF16)


def kernel(x_prompt, x_sample, cache_kv, cache_win, state_conv, page_table, c_prompt, c_sample,
           w_ada, b_ada, g_pre, w_in, conv_w, cmp_pe, cmp_w1, cmp_w2, g_grp, w_out, g_post):
    bp, tp, d = x_prompt.shape
    bs, ts, _ = x_sample.shape
    depth = w_in.shape[0]
    conv_ch = conv_w.shape[2]
    att_w = d - conv_ch
    n_heads = att_w // HEAD_DIM
    gqa = n_heads // N_KV
    n_pages, page = page_table.shape[1], cache_kv.shape[2]
    past = n_pages * page
    wbuf = cache_win.shape[2]
    assert tp % TQ == 0 and tp % KC_SEL == 0 and ts <= T_PAD
    assert att_w == 4 * KV_W and 3 * n_heads <= LANES

    c_q = 4 * conv_ch
    c_kv = c_q + att_w
    c_zb = c_kv + 4 * KV_W
    c_kvw = c_zb + att_w
    cols = {"q": c_q, "kv": c_kv, "kvs": c_kv + 2 * KV_W, "zb": c_zb, "kvw": c_kvw}
    o_kw = c_kv + 4 * KV_W
    o_g = o_kw + 2 * KV_W
    o_zb = o_g + 3 * n_heads
    w_main = jnp.concatenate([w_in[:, :, :o_kw], w_in[:, :, o_zb:], w_in[:, :, o_kw:o_g]], axis=-1).astype(BF16)
    w_gate = jnp.pad(w_in[:, :, o_g:o_zb], ((0, 0), (0, 0), (0, LANES - 3 * n_heads))).astype(BF16)
    w_out_b = w_out.astype(BF16)
    half = CMP_STRIDE * HEAD_DIM
    w1_b = jnp.concatenate([cmp_w1[:, :, :half], cmp_w1[:, :, half:]], axis=-1).astype(BF16)
    w2_b = cmp_w2.astype(BF16)

    mod = _modulation(jnp.concatenate([c_prompt, c_sample], axis=0), w_ada, b_ada)

    cos_p, sin_p = _rope_tables(jnp.arange(tp))
    pos_s = past + jnp.arange(ts)
    cos_s, sin_s = (jnp.tile(a, (bs, 1)) for a in _rope_tables(pos_s))
    expand_p = _expand_matrix(tp)
    nsub_s = -(-(past + ts) // CMP_STRIDE)
    nsp_s = -(-nsub_s // 8) * 8
    expand_s = _expand_matrix(nsp_s * CMP_STRIDE)

    tm_p = min(tp, 1024)
    rows_s = bs * ts
    cache_rows = cache_kv.reshape(depth, cache_kv.shape[1], page * 4 * N_KV, HEAD_DIM)
    win_rows = cache_win.reshape(depth, bs, wbuf * 2 * N_KV, HEAD_DIM)

    def to_heads(a):
        a = a.reshape(bs, ts, N_KV, gqa, HEAD_DIM).transpose(0, 2, 3, 1, 4)
        a = jnp.pad(a, ((0, 0), (0, 0), (0, 0), (0, T_PAD - ts), (0, 0)))
        return a.reshape(bs, N_KV, gqa * T_PAD, HEAD_DIM)

    xp = x_prompt.reshape(bp * tp, d)
    xs = x_sample.reshape(rows_s, d)
    kv_p, kv_s, win_p, win_s, conv_p, conv_s = [], [], [], [], [], []
    for l in range(depth):
        shift, scale, gate = mod[l, :, :d], mod[l, :, d:2 * d], mod[l, :, 2 * d:]
        g_a, g_b = g_grp[l, None, :conv_ch], g_grp[l, None, conv_ch:]

        u, gates = _in_projection(xp, scale[:bp, None], shift[:bp, None], g_pre[l, None], cos_p, sin_p,
                                  w_main[l], w_gate[l], tm_p, conv_ch, att_w)
        u3 = u.reshape(bp, tp, -1)
        ya, cbuf = _short_conv(u3, jnp.zeros((bp, CONV_WIDTH - 1, conv_ch), F32), conv_w[l], g_a,
                               min(tp, 512), BF16)
        kcmp = _compress_prompt(u3, c_kv, cmp_pe[l], w1_b[l], w2_b[l])
        yb = _attention_prompt(u3, gates.reshape(bp, tp, LANES), kcmp, g_b, expand_p, cols, att_w)
        xp = _merge(ya.reshape(bp * tp, conv_ch), yb.reshape(bp * tp, att_w), xp, gate[:bp, None],
                    w_out_b[l], g_post[l, None], min(tp, 512))
        kv_p.append(u3[:, :, c_kv:c_kv + 4 * KV_W].reshape(bp, tp, 4, N_KV, HEAD_DIM))
        win_p.append(u3[:, tp - min(WINDOW, tp):, c_kvw:].reshape(bp, -1, 2, N_KV, HEAD_DIM))
        conv_p.append(cbuf)

        rep = lambda a: jnp.repeat(a[bp:], ts, axis=0)[None]
        u, gates = _in_projection(xs, rep(scale), rep(shift), g_pre[l, None], cos_s, sin_s,
                                  w_main[l], w_gate[l], rows_s, conv_ch, att_w)
        u3 = u.reshape(bs, ts, -1)
        ya, cbuf = _short_conv(u3, state_conv[l], conv_w[l], g_a, ts, F32)
        q_r = to_heads(u[:, c_q:c_q + att_w])
        z_r = to_heads(u[:, c_zb:c_zb + att_w])
        gcol = gates[:, :3 * n_heads].reshape(rows_s, n_heads, 3)
        gcol_r = to_heads(jnp.pad(gcol, ((0, 0), (0, 0), (0, HEAD_DIM - 3))).reshape(rows_s, att_w))
        ggrp_r = jnp.broadcast_to(g_b.reshape(N_KV, gqa, 1, HEAD_DIM),
                                  (N_KV, gqa, T_PAD, HEAD_DIM)).reshape(N_KV, gqa * T_PAD, HEAD_DIM)
        newkv_rows = u[:, c_kv:c_kv + 4 * KV_W].reshape(bs, ts * 4 * N_KV, HEAD_DIM)
        neww_rows = u[:, c_kvw:].reshape(bs, ts * 2 * N_KV, HEAD_DIM)
        yb_r, win = _attention_sample(l, page_table, cache_rows, win_rows, newkv_rows, neww_rows, q_r, gcol_r,
                                      z_r, ggrp_r, cmp_pe[l], w1_b[l], w2_b[l], expand_s, ts, att_w)
        yb = yb_r.reshape(bs, N_KV, gqa, T_PAD, HEAD_DIM)[:, :, :, :ts].transpose(0, 3, 1, 2, 4)
        xs = _merge(ya.reshape(rows_s, conv_ch), yb.reshape(rows_s, att_w), xs, rep(gate),
                    w_out_b[l], g_post[l, None], rows_s)
        kv_s.append(u3[:, :, c_kv:c_kv + 4 * KV_W].reshape(bs, ts, 4, N_KV, HEAD_DIM))
        win_s.append(win.reshape(bs, wbuf, 2, N_KV, HEAD_DIM))
        conv_s.append(cbuf)

    return (xp.reshape(bp, tp, d), xs.reshape(bs, ts, d), jnp.stack(kv_p), jnp.stack(kv_s),
            jnp.stack(win_p), jnp.stack(win_s), jnp.stack(conv_p), jnp.stack(conv_s))
```

```python
import functools

import jax
import jax.numpy as jnp
from jax import lax
from jax.experimental import pallas as pl
from jax.experimental.pallas import tpu as pltpu

F32 = jnp.float32
BF16 = jnp.bfloat16

HEAD_DIM = 128
N_KV = 2
CONV_WIDTH = 3
CMP_LEN = 32
CMP_STRIDE = 16
SEL_BLOCK = 64
N_SELECT = 8
WINDOW = 512
ROPE_THETA = 10000.0
EPS = 1e-6
NEG = -1e30
FORCE_SCORE = 1e3
KV_W = N_KV * HEAD_DIM
LANES = 128
VMEM_LIMIT = 52 * 1024 * 1024

TQ = 128
KC_SEL = 512
T_PAD = 8
SUB_PITCH = 24


def _cparams(sem):
    return pltpu.CompilerParams(dimension_semantics=sem, vmem_limit_bytes=VMEM_LIMIT)


def _silu(x):
    return x / (1.0 + jnp.exp(-x))


def _sigmoid(x):
    return 1.0 / (1.0 + jnp.exp(-x))


def _dot(a, b):
    return jnp.dot(a, b, preferred_element_type=F32)


def _dot_nt(a, b):
    return lax.dot_general(a, b, (((1,), (1,)), ((), ())), preferred_element_type=F32)


def _mod_kernel(c_ref, w_ref, b_ref, o_ref):
    a = _silu(c_ref[...]).astype(BF16)
    o_ref[...] = _dot(a, w_ref[...].astype(BF16)) + b_ref[...]


def _modulation(c_all, w_ada, b_ada):
    depth, d, n = w_ada.shape
    rows = c_all.shape[0]
    tn = 768
    return pl.pallas_call(
        _mod_kernel,
        grid=(depth, n // tn),
        in_specs=[
            pl.BlockSpec((rows, d), lambda l, j: (0, 0)),
            pl.BlockSpec((None, d, tn), lambda l, j: (l, 0, j)),
            pl.BlockSpec((None, 1, tn), lambda l, j: (l, 0, j)),
        ],
        out_specs=pl.BlockSpec((None, rows, tn), lambda l, j: (l, 0, j)),
        out_shape=jax.ShapeDtypeStruct((depth, rows, n), F32),
        compiler_params=_cparams(("parallel", "parallel")),
        name="mod",
    )(c_all, w_ada, b_ada.reshape(depth, 1, n))


def _rope_cols(v, cos, sin):
    outs = []
    for h in range(v.shape[1] // HEAD_DIM):
        xh = v[:, h * HEAD_DIM:(h + 1) * HEAD_DIM]
        outs.append(xh * cos + pltpu.roll(xh, HEAD_DIM // 2, 1) * sin)
    return outs[0] if len(outs) == 1 else jnp.concatenate(outs, axis=1)


def _inproj_kernel(*refs, full_lo, full_hi, kv_lo, win_tile, tiles_per_seq, win_rows, has_prev):
    (x_ref, scale_ref, shift_ref, g_ref, cos_ref, sin_ref, w_ref, wg_ref) = refs[:8]
    u_ref, gates_ref, kv_ref, win_ref, h_ref = refs[8 + (2 if has_prev else 0):]
    i = pl.program_id(0)
    j = pl.program_id(1)
    tm = x_ref.shape[0]
    n_kv_streams, n_win_streams = 4 * N_KV, 2 * N_KV

    @pl.when(j == 0)
    def _():
        x = x_ref[...]
        y = x * lax.rsqrt(jnp.mean(x * x, axis=-1, keepdims=True) + EPS)
        h = (y * g_ref[...]) * (1.0 + scale_ref[...]) + shift_ref[...]
        hb = h.astype(BF16)
        h_ref[...] = hb
        gates_ref[...] = _sigmoid(_dot(hb, wg_ref[...]))

    u_ref[...] = _dot(h_ref[...], w_ref[...])

    @pl.when((j >= full_lo) & (j < full_hi))
    def _():
        u_ref[...] = _rope_cols(u_ref[...], cos_ref[...], sin_ref[...])

    def rope_first_half():
        u_ref[:, :KV_W] = _rope_cols(u_ref[:, :KV_W], cos_ref[...], sin_ref[...])

    for k in range(2):
        @pl.when(j == kv_lo + k)
        def _(k=k):
            rope_first_half()
            for q in range(n_kv_streams // 2):
                kv_ref[pl.ds(k * (n_kv_streams // 2) + q, tm, stride=n_kv_streams), :] = (
                    u_ref[:, q * HEAD_DIM:(q + 1) * HEAD_DIM])

    @pl.when(j == win_tile)
    def _():
        rope_first_half()

        @pl.when(i % tiles_per_seq == tiles_per_seq - 1)
        def _():
            for q in range(n_win_streams):
                win_ref[pl.ds(q, win_rows, stride=n_win_streams), :] = (
                    u_ref[pl.ds(tm - win_rows, win_rows), q * HEAD_DIM:(q + 1) * HEAD_DIM])


def _in_projection(x2d, scale, shift, g_pre, cos, sin, w_main, w_gate, tm, conv_ch, att_w, layer, depth,
                   n_seq, win_rows, prev):
    r, d = x2d.shape
    nm = w_main.shape[1]
    tn = 2 * KV_W
    n_tiles = r // tm
    tiles_per_scale = n_tiles // scale.shape[0]
    tiles_per_seq = n_tiles // n_seq
    s_rows = scale.shape[1]
    pos_tiles = cos.shape[0] // tm
    q0 = 4 * conv_ch // tn
    q1 = q0 + att_w // tn
    assert win_rows <= tm
    kern = functools.partial(_inproj_kernel, full_lo=q0, full_hi=q1, kv_lo=q1, win_tile=q1 + 2 + att_w // tn,
                             tiles_per_seq=tiles_per_seq, win_rows=win_rows, has_prev=prev is not None)
    n_kv_streams, n_win_streams = 4 * N_KV, 2 * N_KV
    in_specs = [
        pl.BlockSpec((tm, d), lambda i, j: (i, 0)),
        pl.BlockSpec((None, s_rows, d), lambda i, j: (i // tiles_per_scale, 0, 0)),
        pl.BlockSpec((None, s_rows, d), lambda i, j: (i // tiles_per_scale, 0, 0)),
        pl.BlockSpec((1, d), lambda i, j: (0, 0)),
        pl.BlockSpec((tm, HEAD_DIM), lambda i, j: (i % pos_tiles, 0)),
        pl.BlockSpec((tm, HEAD_DIM), lambda i, j: (i % pos_tiles, 0)),
        pl.BlockSpec((d, tn), lambda i, j: (0, j)),
        pl.BlockSpec((d, LANES), lambda i, j: (0, 0)),
    ]
    args = [x2d, scale, shift, g_pre, cos, sin, w_main, w_gate]
    aliases = {}
    if prev is not None:
        in_specs += [pl.BlockSpec(memory_space=pl.ANY), pl.BlockSpec(memory_space=pl.ANY)]
        args += list(prev)
        aliases = {8: 2, 9: 3}
    return pl.pallas_call(
        kern,
        grid=(n_tiles, nm // tn),
        in_specs=in_specs,
        out_specs=[
            pl.BlockSpec((tm, tn), lambda i, j: (i, j)),
            pl.BlockSpec((tm, LANES), lambda i, j: (i, 0)),
            pl.BlockSpec((None, tm * n_kv_streams, HEAD_DIM), lambda i, j: (layer, i, 0)),
            pl.BlockSpec((None, None, win_rows * n_win_streams, HEAD_DIM),
                         lambda i, j: (layer, i // tiles_per_seq, 0, 0)),
        ],
        out_shape=[jax.ShapeDtypeStruct((r, nm), F32), jax.ShapeDtypeStruct((r, LANES), F32),
                   jax.ShapeDtypeStruct((depth, r * n_kv_streams, HEAD_DIM), F32),
                   jax.ShapeDtypeStruct((depth, n_seq, win_rows * n_win_streams, HEAD_DIM), F32)],
        scratch_shapes=[pltpu.VMEM((tm, d), BF16)],
        input_output_aliases=aliases,
        compiler_params=_cparams(("arbitrary", "arbitrary")),
        name="inproj",
    )(*args)


def _conv_kernel(b_ref, c_ref, x_ref, z_ref, init_ref, w_ref, g_ref, ya_ref, st_ref, up_ref, *, tr):
    i = pl.program_id(1)
    pad = 8

    @pl.when(i == 0)
    def _():
        up_ref[pl.ds(pad - 2, 2), :] = init_ref[...]

    up_ref[pl.ds(pad, tr), :] = c_ref[...] * x_ref[...]
    w = w_ref[...]
    y = (w[0:1, :] * up_ref[pl.ds(pad - 2, tr), :]
         + w[1:2, :] * up_ref[pl.ds(pad - 1, tr), :]
         + w[2:3, :] * up_ref[pl.ds(pad, tr), :])
    ya = _silu(z_ref[...]) * (b_ref[...] * y)
    yn = ya * lax.rsqrt(jnp.mean(ya * ya, axis=-1, keepdims=True) + EPS)
    ya_ref[...] = (yn * g_ref[...]).astype(ya_ref.dtype)
    last = up_ref[pl.ds(pad + tr - 2, 2), :]
    st_ref[...] = last
    up_ref[pl.ds(pad - 2, 2), :] = last


def _short_conv(u3, init_state, conv_w, g_a, tr, out_dtype):
    b, t, _ = u3.shape
    c = conv_w.shape[1]
    col = lambda k: pl.BlockSpec((None, tr, c), lambda bi, i, k=k: (bi, i, k))
    return pl.pallas_call(
        functools.partial(_conv_kernel, tr=tr),
        grid=(b, t // tr),
        in_specs=[
            col(0), col(1), col(2), col(3),
            pl.BlockSpec((None, CONV_WIDTH - 1, c), lambda bi, i: (bi, 0, 0)),
            pl.BlockSpec((CONV_WIDTH, c), lambda bi, i: (0, 0)),
            pl.BlockSpec((1, c), lambda bi, i: (0, 0)),
        ],
        out_specs=[
            pl.BlockSpec((None, tr, c), lambda bi, i: (bi, i, 0)),
            pl.BlockSpec((None, CONV_WIDTH - 1, c), lambda bi, i: (bi, 0, 0)),
        ],
        out_shape=[jax.ShapeDtypeStruct((b, t, c), out_dtype),
                   jax.ShapeDtypeStruct((b, CONV_WIDTH - 1, c), F32)],
        scratch_shapes=[pltpu.VMEM((tr + 8, c), F32)],
        compiler_params=_cparams(("parallel", "arbitrary")),
        name="conv",
    )(u3, u3, u3, u3, init_state, conv_w, g_a)


def _conv_step_kernel(b_ref, c_ref, x_ref, z_ref, h1_ref, h2_ref, w_ref, g_ref, ya_ref, uc_ref, *, ts):
    uc = c_ref[...] * x_ref[...]
    t = lax.broadcasted_iota(jnp.int32, uc.shape, 0) % ts
    back1 = jnp.where(t < 1, h1_ref[...], pltpu.roll(uc, 1, 0))
    back2 = jnp.where(t < 2, h2_ref[...], pltpu.roll(uc, 2, 0))
    w = w_ref[...]
    y = w[0:1, :] * back2 + w[1:2, :] * back1 + w[2:3, :] * uc
    ya = _silu(z_ref[...]) * (b_ref[...] * y)
    yn = ya * lax.rsqrt(jnp.mean(ya * ya, axis=-1, keepdims=True) + EPS)
    ya_ref[...] = yn * g_ref[...]
    uc_ref[...] = uc


def _short_conv_step(u2d, init_state, conv_w, g_a, ts):
    r = u2d.shape[0]
    nseq, _, c = init_state.shape
    assert ts >= CONV_WIDTH - 1
    zero = jnp.zeros((nseq, ts - 1, c), F32)
    h1 = jnp.concatenate([init_state[:, 1:2], zero], axis=1).reshape(r, c)
    h2 = jnp.concatenate([init_state, zero[:, 1:]], axis=1).reshape(r, c)
    col = lambda k: pl.BlockSpec((r, c), lambda i, k=k: (0, k))
    full = lambda shape: pl.BlockSpec(shape, lambda i: (0,) * len(shape))
    ya, uc = pl.pallas_call(
        functools.partial(_conv_step_kernel, ts=ts),
        grid=(1,),
        in_specs=[col(0), col(1), col(2), col(3), full((r, c)), full((r, c)), full(conv_w.shape), full(g_a.shape)],
        out_specs=[full((r, c)), full((r, c))],
        out_shape=[jax.ShapeDtypeStruct((r, c), F32), jax.ShapeDtypeStruct((r, c), F32)],
        compiler_params=_cparams(("arbitrary",)),
        name="conv_step",
    )(u2d, u2d, u2d, u2d, h1, h2, conv_w, g_a)
    return ya, uc.reshape(nseq, ts, c)[:, ts - (CONV_WIDTH - 1):]


def _pe_term(pe, w1cat):
    flat = lambda lo: jnp.concatenate([pe[lo + r:lo + r + 1, :] for r in range(CMP_STRIDE)], axis=1)
    rows = jnp.concatenate([flat(0), flat(CMP_STRIDE), jnp.zeros((6, CMP_STRIDE * HEAD_DIM), F32)], axis=0)
    r = _dot(rows.astype(BF16), w1cat)
    return r[0:1, :HEAD_DIM] + r[1:2, HEAD_DIM:]


def _compress_block(load_sub, pe_term, w1cat, w2, nsp):
    a = jnp.concatenate([load_sub(r).astype(BF16) for r in range(CMP_STRIDE)], axis=1)
    pq = _dot(a, w1cat)
    pre = pq[:, :HEAD_DIM] + pltpu.roll(pq[:, HEAD_DIM:], nsp - 1, 0) + pe_term
    return _dot(_silu(pre).astype(BF16), w2)


def _compress_kernel(k_ref, pe_ref, w1_ref, w2_ref, o_ref, *, nsp):
    load_sub = lambda r: k_ref[pl.ds(r, nsp, stride=CMP_STRIDE), :]
    w1cat = w1_ref[...]
    o_ref[...] = _compress_block(load_sub, _pe_term(pe_ref[...], w1cat), w1cat, w2_ref[...], nsp)


def _compress_prompt(u3, kv_col0, cmp_pe, cmp_w1, cmp_w2):
    b, t, _ = u3.shape
    nsp = t // CMP_STRIDE
    blk0 = kv_col0 // HEAD_DIM
    return pl.pallas_call(
        functools.partial(_compress_kernel, nsp=nsp),
        grid=(b, 2 * N_KV),
        in_specs=[
            pl.BlockSpec((None, t, HEAD_DIM), lambda bi, s: (bi, 0, blk0 + s)),
            pl.BlockSpec((None, CMP_LEN, HEAD_DIM), lambda bi, s: (s // N_KV, 0, 0)),
            pl.BlockSpec((None, CMP_STRIDE * HEAD_DIM, 2 * HEAD_DIM), lambda bi, s: (s // N_KV, 0, 0)),
            pl.BlockSpec((None, HEAD_DIM, HEAD_DIM), lambda bi, s: (s // N_KV, 0, 0)),
        ],
        out_specs=pl.BlockSpec((None, None, nsp, HEAD_DIM), lambda bi, s: (bi, s, 0, 0)),
        out_shape=jax.ShapeDtypeStruct((b, 2 * N_KV, nsp, HEAD_DIM), F32),
        compiler_params=_cparams(("parallel", "parallel")),
        name="compress",
    )(u3, cmp_pe, cmp_w1, cmp_w2)


def _bias(mask):
    return jnp.where(mask, 0.0, NEG)


def _row_max(s):
    return jnp.maximum(jnp.max(s, axis=-1, keepdims=True), 0.1 * NEG)


def _softmax_parts(s):
    e = jnp.exp(s - _row_max(s))
    return e, jnp.maximum(jnp.sum(e, axis=-1, keepdims=True), 1e-30)


def _fold_lanes(x, op):
    out = x[:, :LANES]
    for c in range(1, x.shape[1] // LANES):
        out = op(out, x[:, c * LANES:(c + 1) * LANES])
    return out


def _compressed_branch(qg, kc, vc, qpos_rows, nc, n_rep):
    ncp = min(kc.shape[0], -(-nc // LANES) * LANES)
    assert ncp >= nc
    kc, vc = kc[:ncp], vc[:ncp]
    cidx = lax.broadcasted_iota(jnp.int32, (1, ncp), 1)
    mask = ((cidx * CMP_STRIDE + (CMP_LEN - 1)) <= qpos_rows) & (cidx < nc)
    e, l = _softmax_parts(_dot_nt(qg, kc.astype(BF16)) + _bias(mask))
    p = e / l
    o = _dot(p.astype(BF16), vc.astype(BF16))
    rt = qg.shape[0] // n_rep
    p_tok = p[0:rt]
    for r in range(1, n_rep):
        p_tok = p_tok + p[r * rt:(r + 1) * rt]
    return o, p_tok


def _pool_matrix(rows, cols, blocks_on_rows):
    per = SEL_BLOCK // CMP_STRIDE
    r = lax.broadcasted_iota(jnp.int32, (rows, cols), 0)
    c = lax.broadcasted_iota(jnp.int32, (rows, cols), 1)
    hit = (c // per == r) if blocks_on_rows else (r // per == c)
    return jnp.where(hit, 1.0, 0.0).astype(F32)


def _top_blocks(score, blk, nb, axis):
    rank = jnp.zeros(score.shape, jnp.int32)
    for j in range(nb):
        one = score[j:j + 1, :] if axis == 0 else score[:, j:j + 1]
        beats = (one > score) | ((one == score) & (blk > j))
        rank = rank + jnp.where(beats, 1, 0)
    return jnp.where((rank < N_SELECT) & (score > 0.5 * NEG), 1.0, 0.0)


def _block_scores(imp, blk, cur):
    forced = (blk == 0) | (blk == cur) | (blk == cur - 1)
    return jnp.where(blk <= cur, jnp.where(forced, FORCE_SCORE, imp), NEG)


def _select_blocks_rows(p_tok, tok_row, nb):
    rt, ncp = p_tok.shape
    nbr = -(-nb // 8) * 8
    imp_t = lax.dot_general(_pool_matrix(nbr, ncp, True), p_tok, (((1,), (1,)), ((), ())),
                            preferred_element_type=F32, precision=lax.Precision.HIGHEST)
    blk = lax.broadcasted_iota(jnp.int32, (nbr, rt), 0)
    sel_t = _top_blocks(_block_scores(imp_t, blk, tok_row // SEL_BLOCK), blk, nb, 0)
    if nbr < LANES:
        sel_t = jnp.concatenate([sel_t, jnp.zeros((LANES - nbr, rt), F32)], axis=0)
    return sel_t.T.astype(BF16)


def _select_blocks_lanes(p_tok, tok_col, nb):
    rt, ncp = p_tok.shape
    imp = jnp.dot(p_tok, _pool_matrix(ncp, LANES, False), preferred_element_type=F32,
                  precision=lax.Precision.HIGHEST)
    blk = lax.broadcasted_iota(jnp.int32, (rt, LANES), 1)
    return _top_blocks(_block_scores(imp, blk, tok_col // SEL_BLOCK), blk, nb, 1).astype(BF16)


def _attn_prompt_kernel(q_ref, kvs_ref, kvw_ref, cmp_ref, gates_ref, z_ref, ggrp_ref, exp_ref, o_ref,
                        s_ref, *, nc, nb, gqa, att_w, t, wk):
    i = pl.program_id(1)
    s0 = i * TQ
    scale = HEAD_DIM ** -0.5
    tok = lax.broadcasted_iota(jnp.int32, (TQ, 1), 0) + s0
    tok_row = lax.broadcasted_iota(jnp.int32, (1, TQ), 1) + s0
    qpos_rows = jnp.concatenate([tok] * gqa, axis=0)
    rows = TQ * gqa
    rep = lambda a: jnp.concatenate([a] * gqa, axis=0)
    kcol = lambda g: slice(g * HEAD_DIM, (g + 1) * HEAD_DIM)
    vcol = lambda g: slice(KV_W + g * HEAD_DIM, KV_W + (g + 1) * HEAD_DIM)
    groups = range(N_KV)

    qs = [(jnp.concatenate(
        [q_ref[:, (g * gqa + r) * HEAD_DIM:(g * gqa + r + 1) * HEAD_DIM] for r in range(gqa)],
        axis=0) * scale).astype(BF16) for g in groups]

    o_cmp, sel = [], []
    for g in groups:
        o, p_tok = _compressed_branch(qs[g], cmp_ref[g], cmp_ref[N_KV + g], qpos_rows, nc, gqa)
        o_cmp.append(o)
        sel.append(_select_blocks_rows(p_tok, tok_row, nb))

    w_lo = pl.multiple_of(jnp.clip(s0 - WINDOW, 0, t - wk), TQ)
    wpos = w_lo + lax.broadcasted_iota(jnp.int32, (1, wk), 1)
    wdist = tok - wpos
    wbias = rep(_bias((wdist >= 0) & (wdist < WINDOW)))
    o_win = []
    for g in groups:
        kw = kvw_ref[pl.ds(w_lo, wk), kcol(g)].astype(BF16)
        vw = kvw_ref[pl.ds(w_lo, wk), vcol(g)].astype(BF16)
        e, l = _softmax_parts(_dot_nt(qs[g], kw) + wbias)
        o_win.append(_dot(e.astype(BF16), vw) / l)

    n_sel = (s0 + TQ + KC_SEL - 1) // KC_SEL

    def score_body(ci, mruns):
        k0 = pl.multiple_of(ci * KC_SEL, KC_SEL)
        causal = (k0 + lax.broadcasted_iota(jnp.int32, (1, KC_SEL), 1)) <= tok
        out = []
        for g in groups:
            k = kvs_ref[pl.ds(k0, KC_SEL), kcol(g)].astype(BF16)
            in_sel = _dot(sel[g], exp_ref[:, pl.ds(k0, KC_SEL)])
            s = _dot_nt(qs[g], k) + rep(_bias((in_sel > 0.5) & causal))
            s_ref[g, :, pl.ds(k0, KC_SEL)] = s
            out.append(jnp.maximum(mruns[g], _fold_lanes(s, jnp.maximum)))
        return tuple(out)

    mruns = lax.fori_loop(0, n_sel, score_body, tuple(jnp.full((rows, LANES), NEG, F32) for _ in groups))
    ms = [_row_max(mruns[g]) for g in groups]

    def value_body(ci, carry):
        k0 = pl.multiple_of(ci * KC_SEL, KC_SEL)
        out = []
        for g in groups:
            lrun, acc = carry[g]
            v = kvs_ref[pl.ds(k0, KC_SEL), vcol(g)].astype(BF16)
            e = jnp.exp(s_ref[g, :, pl.ds(k0, KC_SEL)] - ms[g])
            out.append((lrun + _fold_lanes(e, jnp.add), acc + _dot(e.astype(BF16), v)))
        return tuple(out)

    init = tuple((jnp.zeros((rows, LANES), F32), jnp.zeros((rows, HEAD_DIM), F32)) for _ in groups)
    sums = lax.fori_loop(0, n_sel, value_body, init)

    heads = []
    for g in groups:
        lrun, acc = sums[g]
        o_slc = acc / jnp.maximum(jnp.sum(lrun, axis=-1, keepdims=True), 1e-30)
        for r in range(gqa):
            h = g * gqa + r
            sl = slice(r * TQ, (r + 1) * TQ)
            o = (gates_ref[:, 3 * h:3 * h + 1] * o_cmp[g][sl] + gates_ref[:, 3 * h + 1:3 * h + 2] * o_slc[sl]
                 + gates_ref[:, 3 * h + 2:3 * h + 3] * o_win[g][sl])
            heads.append(o * _silu(z_ref[:, h * HEAD_DIM:(h + 1) * HEAD_DIM]))

    ss = jnp.sum(heads[0] * heads[0], axis=-1, keepdims=True)
    for y in heads[1:]:
        ss = ss + jnp.sum(y * y, axis=-1, keepdims=True)
    inv = lax.rsqrt(ss / att_w + EPS)
    o_ref[...] = jnp.concatenate(
        [(y * inv) * ggrp_ref[:, h * HEAD_DIM:(h + 1) * HEAD_DIM] for h, y in enumerate(heads)],
        axis=1).astype(o_ref.dtype)


def _attention_prompt(u3, gates3, kcmp, g_b, expand, cols, att_w):
    b, t, _ = u3.shape
    gqa = att_w // HEAD_DIM // N_KV
    nsub = t // CMP_STRIDE
    nc = nsub - CMP_LEN // CMP_STRIDE + 1
    nb = max(-(-t // SEL_BLOCK), N_SELECT)
    wk = min(t, WINDOW + TQ)
    kern = functools.partial(_attn_prompt_kernel, nc=nc, nb=nb, gqa=gqa, att_w=att_w, t=t, wk=wk)
    two_kv = 2 * KV_W
    return pl.pallas_call(
        kern,
        grid=(b, t // TQ),
        in_specs=[
            pl.BlockSpec((None, TQ, att_w), lambda bi, i: (bi, i, cols["q"] // att_w)),
            pl.BlockSpec((None, t, two_kv), lambda bi, i: (bi, 0, cols["kvs"] // two_kv)),
            pl.BlockSpec((None, t, two_kv), lambda bi, i: (bi, 0, cols["kvw"] // two_kv)),
            pl.BlockSpec((None, 2 * N_KV, kcmp.shape[2], HEAD_DIM), lambda bi, i: (bi, 0, 0, 0)),
            pl.BlockSpec((None, TQ, LANES), lambda bi, i: (bi, i, 0)),
            pl.BlockSpec((None, TQ, att_w), lambda bi, i: (bi, i, cols["zb"] // att_w)),
            pl.BlockSpec((1, att_w), lambda bi, i: (0, 0)),
            pl.BlockSpec((LANES, t), lambda bi, i: (0, 0)),
        ],
        out_specs=pl.BlockSpec((None, TQ, att_w), lambda bi, i: (bi, i, 0)),
        out_shape=jax.ShapeDtypeStruct((b, t, att_w), BF16),
        scratch_shapes=[pltpu.VMEM((N_KV, TQ * gqa, t), F32)],
        compiler_params=_cparams(("parallel", "arbitrary")),
        name="attn_prompt",
    )(u3, u3, u3, kcmp, gates3, u3, g_b, expand)


def _attn_sample_kernel(pt_ref, q_ref, gcol_ref, z_ref, ggrp_ref, newkv_ref, neww_ref, cw_ref,
                        pe_ref, w1_ref, w2_ref, exp_ref, *rest,
                        n_pages, page, ts, wbuf, nsp, nc, nb, gqa, att_w, has_prev):
    page_refs = rest[:n_pages]
    o_ref, win_ref, sub_ref, full_ref, kwin_ref, pe_term_ref = rest[n_pages + (1 if has_prev else 0):]
    del pt_ref
    n_kv_streams, n_win_streams, n_cmp_streams = 4 * N_KV, 2 * N_KV, 2 * N_KV
    past = n_pages * page
    ktot = nsp * CMP_STRIDE
    scale = HEAD_DIM ** -0.5
    rep = lambda a: jnp.concatenate([a] * gqa, axis=0)

    @pl.when(pl.program_id(0) == 0)
    def _():
        for br in range(2):
            pe_term_ref[br] = jnp.broadcast_to(_pe_term(pe_ref[br], w1_ref[br]), (8, HEAD_DIM))

    sub_per_page = page // CMP_STRIDE
    n_sub_past = past // CMP_STRIDE
    for s in range(n_cmp_streams):
        for p in range(n_pages):
            rows_p = page_refs[p][pl.ds(s, page, stride=n_kv_streams), :]
            for c in range(sub_per_page):
                sub_ref[s, pl.ds((p * sub_per_page + c) * SUB_PITCH, CMP_STRIDE), :] = (
                    rows_p[c * CMP_STRIDE:(c + 1) * CMP_STRIDE])
        sub_ref[s, pl.ds(n_sub_past * SUB_PITCH, (nsp - n_sub_past) * SUB_PITCH), :] = jnp.zeros(
            ((nsp - n_sub_past) * SUB_PITCH, HEAD_DIM), F32)
        sub_ref[s, pl.ds(n_sub_past * SUB_PITCH, ts), :] = newkv_ref[pl.ds(s, ts, stride=n_kv_streams), :]
    for s in range(n_kv_streams - n_cmp_streams):
        src = n_cmp_streams + s
        for p in range(n_pages):
            full_ref[s, pl.ds(p * page, page), :] = page_refs[p][pl.ds(src, page, stride=n_kv_streams), :]
        full_ref[s, pl.ds(past, ts), :] = newkv_ref[pl.ds(src, ts, stride=n_kv_streams), :]
        full_ref[s, pl.ds(past + ts, ktot - past - ts), :] = jnp.zeros((ktot - past - ts, HEAD_DIM), F32)
    for s in range(n_win_streams):
        kwin_ref[s, pl.ds(0, wbuf), :] = cw_ref[pl.ds(s, wbuf, stride=n_win_streams), :]
        kwin_ref[s, pl.ds(wbuf, ts), :] = neww_ref[pl.ds(s, ts, stride=n_win_streams), :]
        kwin_ref[s, pl.ds(wbuf + ts, LANES - ts), :] = jnp.zeros((LANES - ts, HEAD_DIM), F32)
    keep = (wbuf - ts) * n_win_streams
    win_ref[pl.ds(0, keep), :] = cw_ref[pl.ds(ts * n_win_streams, keep), :]
    win_ref[pl.ds(keep, ts * n_win_streams), :] = neww_ref[...]

    tok = past + lax.broadcasted_iota(jnp.int32, (T_PAD, 1), 0)
    qpos_rows = rep(tok)
    kpos = lax.broadcasted_iota(jnp.int32, (1, ktot), 1)
    widx = lax.broadcasted_iota(jnp.int32, (1, wbuf + LANES), 1)
    wpos = past - wbuf + widx
    wdist = tok - wpos
    wbias = rep(_bias((wdist >= 0) & (wdist < WINDOW) & (wpos >= 0) & (widx < wbuf + ts)))

    ys = []
    for g in range(N_KV):
        cmp_kv = []
        for br in range(2):
            s = br * N_KV + g
            load_sub = lambda r, s=s: sub_ref[s, pl.ds(r, nsp, stride=SUB_PITCH), :]
            cmp_kv.append(_compress_block(load_sub, pe_term_ref[br][0:1, :], w1_ref[br], w2_ref[br], nsp))
        qg = (q_ref[g] * scale).astype(BF16)
        o_cmp, p_tok = _compressed_branch(qg, cmp_kv[0], cmp_kv[1], qpos_rows, nc, gqa)
        sel = _select_blocks_lanes(p_tok, tok, nb)
        sbias = rep(_bias((_dot(sel, exp_ref[...]) > 0.5) & (kpos <= tok)))
        ks = full_ref[g].astype(BF16)
        vs = full_ref[N_KV + g].astype(BF16)
        e, l = _softmax_parts(_dot_nt(qg, ks) + sbias)
        o_slc = _dot(e.astype(BF16), vs) / l
        e, l = _softmax_parts(_dot_nt(qg, kwin_ref[g].astype(BF16)) + wbias)
        o_win = _dot(e.astype(BF16), kwin_ref[N_KV + g].astype(BF16)) / l
        gcol = gcol_ref[g]
        o = gcol[:, 0:1] * o_cmp + gcol[:, 1:2] * o_slc + gcol[:, 2:3] * o_win
        ys.append(o * _silu(z_ref[g]))

    ss = None
    for y in ys:
        rs = jnp.sum(y * y, axis=-1, keepdims=True)
        for r in range(gqa):
            part = rs[r * T_PAD:(r + 1) * T_PAD]
            ss = part if ss is None else ss + part
    inv = rep(lax.rsqrt(ss / att_w + EPS))
    for g in range(N_KV):
        o_ref[g] = (ys[g] * inv) * ggrp_ref[g]


def _attention_sample(layer, page_table, cache_rows, win_rows, newkv_rows, neww_rows, q_r, gcol_r, z_r,
                      ggrp_r, pe, w1, w2, expand, ts, att_w, win_prev):
    db = q_r.shape[0]
    n_pages = page_table.shape[1]
    n_kv_streams, n_win_streams = 4 * N_KV, 2 * N_KV
    page = cache_rows.shape[2] // n_kv_streams
    past = n_pages * page
    wbuf = win_rows.shape[2] // n_win_streams
    gqa = att_w // HEAD_DIM // N_KV
    nsub = -(-(past + ts) // CMP_STRIDE)
    nc = nsub - CMP_LEN // CMP_STRIDE + 1
    nsp = -(-nsub // 8) * 8
    nb = max(-(-(past + ts) // SEL_BLOCK), N_SELECT)
    rows = gqa * T_PAD
    kern = functools.partial(_attn_sample_kernel, n_pages=n_pages, page=page, ts=ts, wbuf=wbuf,
                             nsp=nsp, nc=nc, nb=nb, gqa=gqa, att_w=att_w, has_prev=win_prev is not None)
    per_b = lambda shape: pl.BlockSpec((None,) + shape, lambda b, pt: (b,) + (0,) * len(shape))
    const = lambda shape: pl.BlockSpec(shape, lambda b, pt: (0,) * len(shape))
    page_spec = lambda p: pl.BlockSpec((None, None, page * n_kv_streams, HEAD_DIM),
                                       lambda b, pt, p=p: (layer, pt[b * n_pages + p], 0, 0))
    in_specs = [
        per_b((N_KV, rows, HEAD_DIM)),
        per_b((N_KV, rows, HEAD_DIM)),
        per_b((N_KV, rows, HEAD_DIM)),
        const((N_KV, rows, HEAD_DIM)),
        per_b((ts * n_kv_streams, HEAD_DIM)),
        per_b((ts * n_win_streams, HEAD_DIM)),
        pl.BlockSpec((None, None, wbuf * n_win_streams, HEAD_DIM), lambda b, pt: (layer, b, 0, 0)),
        const((2, CMP_LEN, HEAD_DIM)),
        const((2, CMP_STRIDE * HEAD_DIM, 2 * HEAD_DIM)),
        const((2, HEAD_DIM, HEAD_DIM)),
        const((LANES, nsp * CMP_STRIDE)),
    ] + [page_spec(p) for p in range(n_pages)]
    args = [page_table.reshape(-1), q_r, gcol_r, z_r, ggrp_r, newkv_rows, neww_rows, win_rows, pe, w1, w2,
            expand] + [cache_rows] * n_pages
    aliases = {}
    if win_prev is not None:
        in_specs.append(pl.BlockSpec(memory_space=pl.ANY))
        aliases = {len(args): 1}
        args.append(win_prev)
    depth = win_rows.shape[0]
    grid_spec = pltpu.PrefetchScalarGridSpec(
        num_scalar_prefetch=1,
        grid=(db,),
        in_specs=in_specs,
        out_specs=[per_b((N_KV, rows, HEAD_DIM)),
                   pl.BlockSpec((None, None, wbuf * n_win_streams, HEAD_DIM), lambda b, pt: (layer, b, 0, 0))],
        scratch_shapes=[pltpu.VMEM((2 * N_KV, nsp * SUB_PITCH, HEAD_DIM), F32),
                        pltpu.VMEM((2 * N_KV, nsp * CMP_STRIDE, HEAD_DIM), F32),
                        pltpu.VMEM((n_win_streams, wbuf + LANES, HEAD_DIM), F32),
                        pltpu.VMEM((2, 8, HEAD_DIM), F32)],
    )
    return pl.pallas_call(
        kern,
        grid_spec=grid_spec,
        out_shape=[jax.ShapeDtypeStruct((db, N_KV, rows, HEAD_DIM), F32),
                   jax.ShapeDtypeStruct((depth, db, wbuf * n_win_streams, HEAD_DIM), F32)],
        input_output_aliases=aliases,
        compiler_params=_cparams(("arbitrary",)),
        name="attn_sample",
    )(*args)


def _merge_kernel(ya_ref, yb_ref, x_ref, gate_ref, w_ref, g_ref, o_ref, *, conv_ch):
    y = (_dot(ya_ref[...].astype(BF16), w_ref[:conv_ch, :])
         + _dot(yb_ref[...].astype(BF16), w_ref[conv_ch:, :]))
    yn = y * lax.rsqrt(jnp.mean(y * y, axis=-1, keepdims=True) + EPS)
    o_ref[...] = x_ref[...] + gate_ref[...] * (yn * g_ref[...])


def _merge(ya, yb, x2d, gate, w_out, g_post, tm):
    r, d = x2d.shape
    conv_ch = ya.shape[1]
    tiles_per_gate = (r // tm) // gate.shape[0]
    s_rows = gate.shape[1]
    return pl.pallas_call(
        functools.partial(_merge_kernel, conv_ch=conv_ch),
        grid=(r // tm,),
        in_specs=[
            pl.BlockSpec((tm, conv_ch), lambda i: (i, 0)),
            pl.BlockSpec((tm, d - conv_ch), lambda i: (i, 0)),
            pl.BlockSpec((tm, d), lambda i: (i, 0)),
            pl.BlockSpec((None, s_rows, d), lambda i: (i // tiles_per_gate, 0, 0)),
            pl.BlockSpec((d, d), lambda i: (0, 0)),
            pl.BlockSpec((1, d), lambda i: (0, 0)),
        ],
        out_specs=pl.BlockSpec((tm, d), lambda i: (i, 0)),
        out_shape=jax.ShapeDtypeStruct((r, d), F32),
        compiler_params=_cparams(("parallel",)),
        name="merge",
    )(ya, yb, x2d, gate, w_out, g_post)


def _rope_tables(pos):
    half = HEAD_DIM // 2
    inv = jnp.power(ROPE_THETA, -jnp.arange(half, dtype=F32) / half)
    ang = pos.astype(F32)[:, None] * inv
    cos, sin = jnp.cos(ang), jnp.sin(ang)
    return jnp.concatenate([cos, cos], axis=-1), jnp.concatenate([-sin, sin], axis=-1)


def _expand_matrix(n_keys):
    blk = jnp.arange(n_keys, dtype=jnp.int32) // SEL_BLOCK
    return (blk[None, :] == jnp.arange(LANES, dtype=jnp.int32)[:, None]).astype(BF16)


def kernel(x_prompt, x_sample, cache_kv, cache_win, state_conv, page_table, c_prompt, c_sample,
           w_ada, b_ada, g_pre, w_in, conv_w, cmp_pe, cmp_w1, cmp_w2, g_grp, w_out, g_post):
    bp, tp, d = x_prompt.shape
    bs, ts, _ = x_sample.shape
    depth = w_in.shape[0]
    conv_ch = conv_w.shape[2]
    att_w = d - conv_ch
    n_heads = att_w // HEAD_DIM
    gqa = n_heads // N_KV
    n_pages, page = page_table.shape[1], cache_kv.shape[2]
    past = n_pages * page
    wbuf = cache_win.shape[2]
    assert tp % TQ == 0 and tp % KC_SEL == 0 and ts <= T_PAD
    assert att_w == 4 * KV_W and 3 * n_heads <= LANES

    c_q = 4 * conv_ch
    c_kv = c_q + att_w
    c_zb = c_kv + 4 * KV_W
    c_kvw = c_zb + att_w
    cols = {"q": c_q, "kv": c_kv, "kvs": c_kv + 2 * KV_W, "zb": c_zb, "kvw": c_kvw}
    o_kw = c_kv + 4 * KV_W
    o_g = o_kw + 2 * KV_W
    o_zb = o_g + 3 * n_heads
    w_main = jnp.concatenate([w_in[:, :, :o_kw], w_in[:, :, o_zb:], w_in[:, :, o_kw:o_g]], axis=-1).astype(BF16)
    w_gate = jnp.pad(w_in[:, :, o_g:o_zb], ((0, 0), (0, 0), (0, LANES - 3 * n_heads))).astype(BF16)
    w_out_b = w_out.astype(BF16)
    half = CMP_STRIDE * HEAD_DIM
    w1_b = jnp.concatenate([cmp_w1[:, :, :half], cmp_w1[:, :, half:]], axis=-1).astype(BF16)
    w2_b = cmp_w2.astype(BF16)

    mod = _modulation(jnp.concatenate([c_prompt, c_sample], axis=0), w_ada, b_ada)

    cos_p, sin_p = _rope_tables(jnp.arange(tp))
    pos_s = past + jnp.arange(ts)
    cos_s, sin_s = (jnp.tile(a, (bs, 1)) for a in _rope_tables(pos_s))
    expand_p = _expand_matrix(tp)
    nsub_s = -(-(past + ts) // CMP_STRIDE)
    nsp_s = -(-nsub_s // 8) * 8
    expand_s = _expand_matrix(nsp_s * CMP_STRIDE)

    tm_p = min(tp, 1024)
    rows_s = bs * ts
    cache_rows = cache_kv.reshape(depth, cache_kv.shape[1], page * 4 * N_KV, HEAD_DIM)
    win_rows = cache_win.reshape(depth, bs, wbuf * 2 * N_KV, HEAD_DIM)

    def to_heads(a):
        a = a.reshape(bs, ts, N_KV, gqa, HEAD_DIM).transpose(0, 2, 3, 1, 4)
        a = jnp.pad(a, ((0, 0), (0, 0), (0, 0), (0, T_PAD - ts), (0, 0)))
        return a.reshape(bs, N_KV, gqa * T_PAD, HEAD_DIM)

    xp = x_prompt.reshape(bp * tp, d)
    xs = x_sample.reshape(rows_s, d)
    win_keep = min(WINDOW, tp)
    rows_p = rows_sm = win_s = None
    conv_p, conv_s = [], []
    for l in range(depth):
        shift, scale, gate = mod[l, :, :d], mod[l, :, d:2 * d], mod[l, :, 2 * d:]
        g_a, g_b = g_grp[l, None, :conv_ch], g_grp[l, None, conv_ch:]

        u, gates, *rows_p = _in_projection(xp, scale[:bp, None], shift[:bp, None], g_pre[l, None], cos_p, sin_p,
                                           w_main[l], w_gate[l], tm_p, conv_ch, att_w, l, depth, bp, win_keep,
                                           rows_p)
        u3 = u.reshape(bp, tp, -1)
        ya, cbuf = _short_conv(u3, jnp.zeros((bp, CONV_WIDTH - 1, conv_ch), F32), conv_w[l], g_a,
                               min(tp, 512), BF16)
        kcmp = _compress_prompt(u3, c_kv, cmp_pe[l], w1_b[l], w2_b[l])
        yb = _attention_prompt(u3, gates.reshape(bp, tp, LANES), kcmp, g_b, expand_p, cols, att_w)
        xp = _merge(ya.reshape(bp * tp, conv_ch), yb.reshape(bp * tp, att_w), xp, gate[:bp, None],
                    w_out_b[l], g_post[l, None], min(tp, 512))
        conv_p.append(cbuf)

        rep = lambda a: jnp.repeat(a[bp:], ts, axis=0)[None]
        u, gates, *rows_sm = _in_projection(xs, rep(scale), rep(shift), g_pre[l, None], cos_s, sin_s,
                                            w_main[l], w_gate[l], rows_s, conv_ch, att_w, l, depth, 1, rows_s,
                                            rows_sm)
        ya, cbuf = _short_conv_step(u, state_conv[l], conv_w[l], g_a, ts)
        q_r = to_heads(u[:, c_q:c_q + att_w])
        z_r = to_heads(u[:, c_zb:c_zb + att_w])
        gcol = gates[:, :3 * n_heads].reshape(rows_s, n_heads, 3)
        gcol_r = to_heads(jnp.pad(gcol, ((0, 0), (0, 0), (0, HEAD_DIM - 3))).reshape(rows_s, att_w))
        ggrp_r = jnp.broadcast_to(g_b.reshape(N_KV, gqa, 1, HEAD_DIM),
                                  (N_KV, gqa, T_PAD, HEAD_DIM)).reshape(N_KV, gqa * T_PAD, HEAD_DIM)
        newkv_rows = rows_sm[0][l].reshape(bs, ts * 4 * N_KV, HEAD_DIM)
        neww_rows = rows_sm[1][l].reshape(bs, ts * 2 * N_KV, HEAD_DIM)
        yb_r, win_s = _attention_sample(l, page_table, cache_rows, win_rows, newkv_rows, neww_rows, q_r, gcol_r,
                                        z_r, ggrp_r, cmp_pe[l], w1_b[l], w2_b[l], expand_s, ts, att_w, win_s)
        yb = yb_r.reshape(bs, N_KV, gqa, T_PAD, HEAD_DIM)[:, :, :, :ts].transpose(0, 3, 1, 2, 4)
        xs = _merge(ya, yb.reshape(rows_s, att_w), xs, rep(gate), w_out_b[l], g_post[l, None], rows_s)
        conv_s.append(cbuf)

    kv_shape = lambda b, t: (depth, b, t, 4, N_KV, HEAD_DIM)
    win_shape = lambda b, t: (depth, b, t, 2, N_KV, HEAD_DIM)
    return (xp.reshape(bp, tp, d), xs.reshape(bs, ts, d),
            rows_p[0].reshape(kv_shape(bp, tp)), rows_sm[0].reshape(kv_shape(bs, ts)),
            rows_p[1].reshape(win_shape(bp, win_keep)), win_s.reshape(win_shape(bs, wbuf)),
            jnp.stack(conv_p), jnp.stack(conv_s))
```

```python
import functools

import jax
import jax.numpy as jnp
from jax import lax
from jax.experimental import pallas as pl
from jax.experimental.pallas import tpu as pltpu

F32 = jnp.float32
BF16 = jnp.bfloat16

HEAD_DIM = 128
N_KV = 2
CONV_WIDTH = 3
CMP_LEN = 32
CMP_STRIDE = 16
SEL_BLOCK = 64
N_SELECT = 8
WINDOW = 512
ROPE_THETA = 10000.0
EPS = 1e-6
NEG = -1e30
FORCE_SCORE = 1e3
KV_W = N_KV * HEAD_DIM
LANES = 128
VMEM_LIMIT = 52 * 1024 * 1024

TQ = 128
KC_SEL = 512
T_PAD = 8


def _cparams(sem):
    return pltpu.CompilerParams(dimension_semantics=sem, vmem_limit_bytes=VMEM_LIMIT)


def _silu(x):
    return x / (1.0 + jnp.exp(-x))


def _sigmoid(x):
    return 1.0 / (1.0 + jnp.exp(-x))


def _dot(a, b):
    return jnp.dot(a, b, preferred_element_type=F32)


def _dot_nt(a, b):
    return lax.dot_general(a, b, (((1,), (1,)), ((), ())), preferred_element_type=F32)


def _mod_kernel(c_ref, w_ref, b_ref, o_ref):
    a = _silu(c_ref[...]).astype(BF16)
    o_ref[...] = _dot(a, w_ref[...].astype(BF16)) + b_ref[...]


def _modulation(c_all, w_ada, b_ada):
    depth, d, n = w_ada.shape
    rows = c_all.shape[0]
    tn = 768
    return pl.pallas_call(
        _mod_kernel,
        grid=(depth, n // tn),
        in_specs=[
            pl.BlockSpec((rows, d), lambda l, j: (0, 0)),
            pl.BlockSpec((None, d, tn), lambda l, j: (l, 0, j)),
            pl.BlockSpec((None, 1, tn), lambda l, j: (l, 0, j)),
        ],
        out_specs=pl.BlockSpec((None, rows, tn), lambda l, j: (l, 0, j)),
        out_shape=jax.ShapeDtypeStruct((depth, rows, n), F32),
        compiler_params=_cparams(("parallel", "parallel")),
        name="mod",
    )(c_all, w_ada, b_ada.reshape(depth, 1, n))


def _rope_cols(v, cos, sin):
    outs = []
    for h in range(v.shape[1] // HEAD_DIM):
        xh = v[:, h * HEAD_DIM:(h + 1) * HEAD_DIM]
        outs.append(xh * cos + pltpu.roll(xh, HEAD_DIM // 2, 1) * sin)
    return outs[0] if len(outs) == 1 else jnp.concatenate(outs, axis=1)


def _inproj_kernel(*refs, full_lo, full_hi, kv_lo, win_tile, tiles_per_seq, win_rows, has_prev):
    (x_ref, scale_ref, shift_ref, g_ref, cos_ref, sin_ref, w_ref, wg_ref) = refs[:8]
    u_ref, gates_ref, kv_ref, win_ref, h_ref = refs[8 + (2 if has_prev else 0):]
    i = pl.program_id(0)
    j = pl.program_id(1)
    tm = x_ref.shape[0]
    n_kv_streams, n_win_streams = 4 * N_KV, 2 * N_KV

    @pl.when(j == 0)
    def _():
        x = x_ref[...]
        y = x * lax.rsqrt(jnp.mean(x * x, axis=-1, keepdims=True) + EPS)
        h = (y * g_ref[...]) * (1.0 + scale_ref[...]) + shift_ref[...]
        hb = h.astype(BF16)
        h_ref[...] = hb
        gates_ref[...] = _sigmoid(_dot(hb, wg_ref[...]))

    u_ref[...] = _dot(h_ref[...], w_ref[...])

    @pl.when((j >= full_lo) & (j < full_hi))
    def _():
        u_ref[...] = _rope_cols(u_ref[...], cos_ref[...], sin_ref[...])

    def rope_first_half():
        u_ref[:, :KV_W] = _rope_cols(u_ref[:, :KV_W], cos_ref[...], sin_ref[...])

    for k in range(2):
        @pl.when(j == kv_lo + k)
        def _(k=k):
            rope_first_half()
            for q in range(n_kv_streams // 2):
                kv_ref[pl.ds(k * (n_kv_streams // 2) + q, tm, stride=n_kv_streams), :] = (
                    u_ref[:, q * HEAD_DIM:(q + 1) * HEAD_DIM])

    @pl.when(j == win_tile)
    def _():
        rope_first_half()

        @pl.when(i % tiles_per_seq == tiles_per_seq - 1)
        def _():
            for q in range(n_win_streams):
                win_ref[pl.ds(q, win_rows, stride=n_win_streams), :] = (
                    u_ref[pl.ds(tm - win_rows, win_rows), q * HEAD_DIM:(q + 1) * HEAD_DIM])


def _in_projection(x2d, scale, shift, g_pre, cos, sin, w_main, w_gate, tm, conv_ch, att_w, layer, depth,
                   n_seq, win_rows, prev):
    r, d = x2d.shape
    nm = w_main.shape[1]
    tn = 2 * KV_W
    n_tiles = r // tm
    tiles_per_scale = n_tiles // scale.shape[0]
    tiles_per_seq = n_tiles // n_seq
    s_rows = scale.shape[1]
    pos_tiles = cos.shape[0] // tm
    q0 = 4 * conv_ch // tn
    q1 = q0 + att_w // tn
    assert win_rows <= tm
    kern = functools.partial(_inproj_kernel, full_lo=q0, full_hi=q1, kv_lo=q1, win_tile=q1 + 2 + att_w // tn,
                             tiles_per_seq=tiles_per_seq, win_rows=win_rows, has_prev=prev is not None)
    n_kv_streams, n_win_streams = 4 * N_KV, 2 * N_KV
    in_specs = [
        pl.BlockSpec((tm, d), lambda i, j: (i, 0)),
        pl.BlockSpec((None, s_rows, d), lambda i, j: (i // tiles_per_scale, 0, 0)),
        pl.BlockSpec((None, s_rows, d), lambda i, j: (i // tiles_per_scale, 0, 0)),
        pl.BlockSpec((1, d), lambda i, j: (0, 0)),
        pl.BlockSpec((tm, HEAD_DIM), lambda i, j: (i % pos_tiles, 0)),
        pl.BlockSpec((tm, HEAD_DIM), lambda i, j: (i % pos_tiles, 0)),
        pl.BlockSpec((d, tn), lambda i, j: (0, j)),
        pl.BlockSpec((d, LANES), lambda i, j: (0, 0)),
    ]
    args = [x2d, scale, shift, g_pre, cos, sin, w_main, w_gate]
    aliases = {}
    if prev is not None:
        in_specs += [pl.BlockSpec(memory_space=pl.ANY), pl.BlockSpec(memory_space=pl.ANY)]
        args += list(prev)
        aliases = {8: 2, 9: 3}
    return pl.pallas_call(
        kern,
        grid=(n_tiles, nm // tn),
        in_specs=in_specs,
        out_specs=[
            pl.BlockSpec((tm, tn), lambda i, j: (i, j)),
            pl.BlockSpec((tm, LANES), lambda i, j: (i, 0)),
            pl.BlockSpec((None, tm * n_kv_streams, HEAD_DIM), lambda i, j: (layer, i, 0)),
            pl.BlockSpec((None, None, win_rows * n_win_streams, HEAD_DIM),
                         lambda i, j: (layer, i // tiles_per_seq, 0, 0)),
        ],
        out_shape=[jax.ShapeDtypeStruct((r, nm), F32), jax.ShapeDtypeStruct((r, LANES), F32),
                   jax.ShapeDtypeStruct((depth, r * n_kv_streams, HEAD_DIM), F32),
                   jax.ShapeDtypeStruct((depth, n_seq, win_rows * n_win_streams, HEAD_DIM), F32)],
        scratch_shapes=[pltpu.VMEM((tm, d), BF16)],
        input_output_aliases=aliases,
        compiler_params=_cparams(("arbitrary", "arbitrary")),
        name="inproj",
    )(*args)


def _conv_kernel(b_ref, c_ref, x_ref, z_ref, init_ref, w_ref, g_ref, ya_ref, st_ref, up_ref, *, tr):
    i = pl.program_id(1)
    pad = 8

    @pl.when(i == 0)
    def _():
        up_ref[pl.ds(pad - 2, 2), :] = init_ref[...]

    up_ref[pl.ds(pad, tr), :] = c_ref[...] * x_ref[...]
    w = w_ref[...]
    y = (w[0:1, :] * up_ref[pl.ds(pad - 2, tr), :]
         + w[1:2, :] * up_ref[pl.ds(pad - 1, tr), :]
         + w[2:3, :] * up_ref[pl.ds(pad, tr), :])
    ya = _silu(z_ref[...]) * (b_ref[...] * y)
    yn = ya * lax.rsqrt(jnp.mean(ya * ya, axis=-1, keepdims=True) + EPS)
    ya_ref[...] = (yn * g_ref[...]).astype(ya_ref.dtype)
    last = up_ref[pl.ds(pad + tr - 2, 2), :]
    st_ref[...] = last
    up_ref[pl.ds(pad - 2, 2), :] = last


def _short_conv(u3, init_state, conv_w, g_a, tr, out_dtype):
    b, t, _ = u3.shape
    c = conv_w.shape[1]
    col = lambda k: pl.BlockSpec((None, tr, c), lambda bi, i, k=k: (bi, i, k))
    return pl.pallas_call(
        functools.partial(_conv_kernel, tr=tr),
        grid=(b, t // tr),
        in_specs=[
            col(0), col(1), col(2), col(3),
            pl.BlockSpec((None, CONV_WIDTH - 1, c), lambda bi, i: (bi, 0, 0)),
            pl.BlockSpec((CONV_WIDTH, c), lambda bi, i: (0, 0)),
            pl.BlockSpec((1, c), lambda bi, i: (0, 0)),
        ],
        out_specs=[
            pl.BlockSpec((None, tr, c), lambda bi, i: (bi, i, 0)),
            pl.BlockSpec((None, CONV_WIDTH - 1, c), lambda bi, i: (bi, 0, 0)),
        ],
        out_shape=[jax.ShapeDtypeStruct((b, t, c), out_dtype),
                   jax.ShapeDtypeStruct((b, CONV_WIDTH - 1, c), F32)],
        scratch_shapes=[pltpu.VMEM((tr + 8, c), F32)],
        compiler_params=_cparams(("parallel", "arbitrary")),
        name="conv",
    )(u3, u3, u3, u3, init_state, conv_w, g_a)


def _conv_step_kernel(b_ref, c_ref, x_ref, z_ref, h1_ref, h2_ref, w_ref, g_ref, ya_ref, uc_ref, *, ts):
    uc = c_ref[...] * x_ref[...]
    t = lax.broadcasted_iota(jnp.int32, uc.shape, 0) % ts
    back1 = jnp.where(t < 1, h1_ref[...], pltpu.roll(uc, 1, 0))
    back2 = jnp.where(t < 2, h2_ref[...], pltpu.roll(uc, 2, 0))
    w = w_ref[...]
    y = w[0:1, :] * back2 + w[1:2, :] * back1 + w[2:3, :] * uc
    ya = _silu(z_ref[...]) * (b_ref[...] * y)
    yn = ya * lax.rsqrt(jnp.mean(ya * ya, axis=-1, keepdims=True) + EPS)
    ya_ref[...] = yn * g_ref[...]
    uc_ref[...] = uc


def _short_conv_step(u2d, init_state, conv_w, g_a, ts):
    r = u2d.shape[0]
    nseq, _, c = init_state.shape
    assert ts >= CONV_WIDTH - 1
    zero = jnp.zeros((nseq, ts - 1, c), F32)
    h1 = jnp.concatenate([init_state[:, 1:2], zero], axis=1).reshape(r, c)
    h2 = jnp.concatenate([init_state, zero[:, 1:]], axis=1).reshape(r, c)
    col = lambda k: pl.BlockSpec((r, c), lambda i, k=k: (0, k))
    full = lambda shape: pl.BlockSpec(shape, lambda i: (0,) * len(shape))
    ya, uc = pl.pallas_call(
        functools.partial(_conv_step_kernel, ts=ts),
        grid=(1,),
        in_specs=[col(0), col(1), col(2), col(3), full((r, c)), full((r, c)), full(conv_w.shape), full(g_a.shape)],
        out_specs=[full((r, c)), full((r, c))],
        out_shape=[jax.ShapeDtypeStruct((r, c), F32), jax.ShapeDtypeStruct((r, c), F32)],
        compiler_params=_cparams(("arbitrary",)),
        name="conv_step",
    )(u2d, u2d, u2d, u2d, h1, h2, conv_w, g_a)
    return ya, uc.reshape(nseq, ts, c)[:, ts - (CONV_WIDTH - 1):]


def _pe_term(pe, w1cat):
    flat = lambda lo: jnp.concatenate([pe[lo + r:lo + r + 1, :] for r in range(CMP_STRIDE)], axis=1)
    rows = jnp.concatenate([flat(0), flat(CMP_STRIDE), jnp.zeros((6, CMP_STRIDE * HEAD_DIM), F32)], axis=0)
    r = _dot(rows.astype(BF16), w1cat)
    return r[0:1, :HEAD_DIM] + r[1:2, HEAD_DIM:]


def _compress_block(load_sub, pe_term, w1cat, w2, nsp):
    a = jnp.concatenate([load_sub(r).astype(BF16) for r in range(CMP_STRIDE)], axis=1)
    pq = _dot(a, w1cat)
    pre = pq[:, :HEAD_DIM] + pltpu.roll(pq[:, HEAD_DIM:], nsp - 1, 0) + pe_term
    return _dot(_silu(pre).astype(BF16), w2)


def _compress_kernel(k_ref, pe_ref, w1_ref, w2_ref, o_ref, *, nsp):
    load_sub = lambda r: k_ref[pl.ds(r, nsp, stride=CMP_STRIDE), :]
    w1cat = w1_ref[...]
    o_ref[...] = _compress_block(load_sub, _pe_term(pe_ref[...], w1cat), w1cat, w2_ref[...], nsp)


def _compress_prompt(u3, kv_col0, cmp_pe, cmp_w1, cmp_w2):
    b, t, _ = u3.shape
    nsp = t // CMP_STRIDE
    blk0 = kv_col0 // HEAD_DIM
    return pl.pallas_call(
        functools.partial(_compress_kernel, nsp=nsp),
        grid=(b, 2 * N_KV),
        in_specs=[
            pl.BlockSpec((None, t, HEAD_DIM), lambda bi, s: (bi, 0, blk0 + s)),
            pl.BlockSpec((None, CMP_LEN, HEAD_DIM), lambda bi, s: (s // N_KV, 0, 0)),
            pl.BlockSpec((None, CMP_STRIDE * HEAD_DIM, 2 * HEAD_DIM), lambda bi, s: (s // N_KV, 0, 0)),
            pl.BlockSpec((None, HEAD_DIM, HEAD_DIM), lambda bi, s: (s // N_KV, 0, 0)),
        ],
        out_specs=pl.BlockSpec((None, None, nsp, HEAD_DIM), lambda bi, s: (bi, s, 0, 0)),
        out_shape=jax.ShapeDtypeStruct((b, 2 * N_KV, nsp, HEAD_DIM), F32),
        compiler_params=_cparams(("parallel", "parallel")),
        name="compress",
    )(u3, cmp_pe, cmp_w1, cmp_w2)


def _bias(mask):
    return jnp.where(mask, 0.0, NEG)


def _row_max(s):
    return jnp.maximum(jnp.max(s, axis=-1, keepdims=True), 0.1 * NEG)


def _softmax_parts(s):
    e = jnp.exp(s - _row_max(s))
    return e, jnp.maximum(jnp.sum(e, axis=-1, keepdims=True), 1e-30)


def _fold_lanes(x, op):
    out = x[:, :LANES]
    for c in range(1, x.shape[1] // LANES):
        out = op(out, x[:, c * LANES:(c + 1) * LANES])
    return out


def _compressed_branch(qg, kc, vc, qpos_rows, nc, n_rep):
    ncp = min(kc.shape[0], -(-nc // LANES) * LANES)
    assert ncp >= nc
    kc, vc = kc[:ncp], vc[:ncp]
    cidx = lax.broadcasted_iota(jnp.int32, (1, ncp), 1)
    mask = ((cidx * CMP_STRIDE + (CMP_LEN - 1)) <= qpos_rows) & (cidx < nc)
    e, l = _softmax_parts(_dot_nt(qg, kc.astype(BF16)) + _bias(mask))
    p = e / l
    o = _dot(p.astype(BF16), vc.astype(BF16))
    rt = qg.shape[0] // n_rep
    p_tok = p[0:rt]
    for r in range(1, n_rep):
        p_tok = p_tok + p[r * rt:(r + 1) * rt]
    return o, p_tok


def _pool_matrix(rows, cols, blocks_on_rows):
    per = SEL_BLOCK // CMP_STRIDE
    r = lax.broadcasted_iota(jnp.int32, (rows, cols), 0)
    c = lax.broadcasted_iota(jnp.int32, (rows, cols), 1)
    hit = (c // per == r) if blocks_on_rows else (r // per == c)
    return jnp.where(hit, 1.0, 0.0).astype(F32)


def _top_blocks(score, blk, nb, axis):
    rank = jnp.zeros(score.shape, jnp.int32)
    for j in range(nb):
        one = score[j:j + 1, :] if axis == 0 else score[:, j:j + 1]
        beats = (one > score) | ((one == score) & (blk > j))
        rank = rank + jnp.where(beats, 1, 0)
    return jnp.where((rank < N_SELECT) & (score > 0.5 * NEG), 1.0, 0.0)


def _block_scores(imp, blk, cur):
    forced = (blk == 0) | (blk == cur) | (blk == cur - 1)
    return jnp.where(blk <= cur, jnp.where(forced, FORCE_SCORE, imp), NEG)


def _select_blocks_rows(p_tok, tok_row, nb):
    rt, ncp = p_tok.shape
    nbr = -(-nb // 8) * 8
    imp_t = lax.dot_general(_pool_matrix(nbr, ncp, True), p_tok, (((1,), (1,)), ((), ())),
                            preferred_element_type=F32, precision=lax.Precision.HIGHEST)
    blk = lax.broadcasted_iota(jnp.int32, (nbr, rt), 0)
    sel_t = _top_blocks(_block_scores(imp_t, blk, tok_row // SEL_BLOCK), blk, nb, 0)
    if nbr < LANES:
        sel_t = jnp.concatenate([sel_t, jnp.zeros((LANES - nbr, rt), F32)], axis=0)
    return sel_t.T.astype(BF16)


def _select_blocks_lanes(p_tok, tok_col, nb):
    rt, ncp = p_tok.shape
    imp = jnp.dot(p_tok, _pool_matrix(ncp, LANES, False), preferred_element_type=F32,
                  precision=lax.Precision.HIGHEST)
    blk = lax.broadcasted_iota(jnp.int32, (rt, LANES), 1)
    return _top_blocks(_block_scores(imp, blk, tok_col // SEL_BLOCK), blk, nb, 1).astype(BF16)


def _attn_prompt_kernel(q_ref, kvs_ref, kvw_ref, cmp_ref, gates_ref, z_ref, ggrp_ref, exp_ref, o_ref,
                        s_ref, *, nc, nb, gqa, att_w, t, wk):
    i = pl.program_id(1)
    s0 = i * TQ
    scale = HEAD_DIM ** -0.5
    tok = lax.broadcasted_iota(jnp.int32, (TQ, 1), 0) + s0
    tok_row = lax.broadcasted_iota(jnp.int32, (1, TQ), 1) + s0
    qpos_rows = jnp.concatenate([tok] * gqa, axis=0)
    rows = TQ * gqa
    rep = lambda a: jnp.concatenate([a] * gqa, axis=0)
    kcol = lambda g: slice(g * HEAD_DIM, (g + 1) * HEAD_DIM)
    vcol = lambda g: slice(KV_W + g * HEAD_DIM, KV_W + (g + 1) * HEAD_DIM)
    groups = range(N_KV)

    qs = [(jnp.concatenate(
        [q_ref[:, (g * gqa + r) * HEAD_DIM:(g * gqa + r + 1) * HEAD_DIM] for r in range(gqa)],
        axis=0) * scale).astype(BF16) for g in groups]

    o_cmp, sel = [], []
    for g in groups:
        o, p_tok = _compressed_branch(qs[g], cmp_ref[g], cmp_ref[N_KV + g], qpos_rows, nc, gqa)
        o_cmp.append(o)
        sel.append(_select_blocks_rows(p_tok, tok_row, nb))

    w_lo = pl.multiple_of(jnp.clip(s0 - WINDOW, 0, t - wk), TQ)
    wpos = w_lo + lax.broadcasted_iota(jnp.int32, (1, wk), 1)
    wdist = tok - wpos
    wbias = rep(_bias((wdist >= 0) & (wdist < WINDOW)))
    o_win = []
    for g in groups:
        kw = kvw_ref[pl.ds(w_lo, wk), kcol(g)].astype(BF16)
        vw = kvw_ref[pl.ds(w_lo, wk), vcol(g)].astype(BF16)
        e, l = _softmax_parts(_dot_nt(qs[g], kw) + wbias)
        o_win.append(_dot(e.astype(BF16), vw) / l)

    n_sel = (s0 + TQ + KC_SEL - 1) // KC_SEL

    def score_body(ci, mruns):
        k0 = pl.multiple_of(ci * KC_SEL, KC_SEL)
        causal = (k0 + lax.broadcasted_iota(jnp.int32, (1, KC_SEL), 1)) <= tok
        out = []
        for g in groups:
            k = kvs_ref[pl.ds(k0, KC_SEL), kcol(g)].astype(BF16)
            in_sel = _dot(sel[g], exp_ref[:, pl.ds(k0, KC_SEL)])
            s = _dot_nt(qs[g], k) + rep(_bias((in_sel > 0.5) & causal))
            s_ref[g, :, pl.ds(k0, KC_SEL)] = s
            out.append(jnp.maximum(mruns[g], _fold_lanes(s, jnp.maximum)))
        return tuple(out)

    mruns = lax.fori_loop(0, n_sel, score_body, tuple(jnp.full((rows, LANES), NEG, F32) for _ in groups))
    ms = [_row_max(mruns[g]) for g in groups]

    def value_body(ci, carry):
        k0 = pl.multiple_of(ci * KC_SEL, KC_SEL)
        out = []
        for g in groups:
            lrun, acc = carry[g]
            v = kvs_ref[pl.ds(k0, KC_SEL), vcol(g)].astype(BF16)
            e = jnp.exp(s_ref[g, :, pl.ds(k0, KC_SEL)] - ms[g])
            out.append((lrun + _fold_lanes(e, jnp.add), acc + _dot(e.astype(BF16), v)))
        return tuple(out)

    init = tuple((jnp.zeros((rows, LANES), F32), jnp.zeros((rows, HEAD_DIM), F32)) for _ in groups)
    sums = lax.fori_loop(0, n_sel, value_body, init)

    heads = []
    for g in groups:
        lrun, acc = sums[g]
        o_slc = acc / jnp.maximum(jnp.sum(lrun, axis=-1, keepdims=True), 1e-30)
        for r in range(gqa):
            h = g * gqa + r
            sl = slice(r * TQ, (r + 1) * TQ)
            o = (gates_ref[:, 3 * h:3 * h + 1] * o_cmp[g][sl] + gates_ref[:, 3 * h + 1:3 * h + 2] * o_slc[sl]
                 + gates_ref[:, 3 * h + 2:3 * h + 3] * o_win[g][sl])
            heads.append(o * _silu(z_ref[:, h * HEAD_DIM:(h + 1) * HEAD_DIM]))

    ss = jnp.sum(heads[0] * heads[0], axis=-1, keepdims=True)
    for y in heads[1:]:
        ss = ss + jnp.sum(y * y, axis=-1, keepdims=True)
    inv = lax.rsqrt(ss / att_w + EPS)
    o_ref[...] = jnp.concatenate(
        [(y * inv) * ggrp_ref[:, h * HEAD_DIM:(h + 1) * HEAD_DIM] for h, y in enumerate(heads)],
        axis=1).astype(o_ref.dtype)


def _attention_prompt(u3, gates3, kcmp, g_b, expand, cols, att_w):
    b, t, _ = u3.shape
    gqa = att_w // HEAD_DIM // N_KV
    nsub = t // CMP_STRIDE
    nc = nsub - CMP_LEN // CMP_STRIDE + 1
    nb = max(-(-t // SEL_BLOCK), N_SELECT)
    wk = min(t, WINDOW + TQ)
    kern = functools.partial(_attn_prompt_kernel, nc=nc, nb=nb, gqa=gqa, att_w=att_w, t=t, wk=wk)
    two_kv = 2 * KV_W
    return pl.pallas_call(
        kern,
        grid=(b, t // TQ),
        in_specs=[
            pl.BlockSpec((None, TQ, att_w), lambda bi, i: (bi, i, cols["q"] // att_w)),
            pl.BlockSpec((None, t, two_kv), lambda bi, i: (bi, 0, cols["kvs"] // two_kv)),
            pl.BlockSpec((None, t, two_kv), lambda bi, i: (bi, 0, cols["kvw"] // two_kv)),
            pl.BlockSpec((None, 2 * N_KV, kcmp.shape[2], HEAD_DIM), lambda bi, i: (bi, 0, 0, 0)),
            pl.BlockSpec((None, TQ, LANES), lambda bi, i: (bi, i, 0)),
            pl.BlockSpec((None, TQ, att_w), lambda bi, i: (bi, i, cols["zb"] // att_w)),
            pl.BlockSpec((1, att_w), lambda bi, i: (0, 0)),
            pl.BlockSpec((LANES, t), lambda bi, i: (0, 0)),
        ],
        out_specs=pl.BlockSpec((None, TQ, att_w), lambda bi, i: (bi, i, 0)),
        out_shape=jax.ShapeDtypeStruct((b, t, att_w), BF16),
        scratch_shapes=[pltpu.VMEM((N_KV, TQ * gqa, t), F32)],
        compiler_params=_cparams(("parallel", "arbitrary")),
        name="attn_prompt",
    )(u3, u3, u3, kcmp, gates3, u3, g_b, expand)


def _page_copies(pt_ref, cache_ref, full_ref, sem_ref, layer, b, slot, n_pages, page):
    copies = []
    for p in range(n_pages):
        pg = pt_ref[b * n_pages + p]
        for s in range(4 * N_KV):
            copies.append(pltpu.make_async_copy(
                cache_ref.at[layer, pg, :, s, :], full_ref.at[slot, s, pl.ds(p * page, page), :],
                sem_ref.at[slot]))
    return copies


def _attn_sample_kernel(pt_ref, q_ref, gcol_ref, z_ref, ggrp_ref, newkv_ref, neww_ref, cw_ref,
                        pe_ref, w1_ref, w2_ref, exp_ref, cache_ref, *rest,
                        layer, n_pages, page, ts, wbuf, nsp, nc, nb, gqa, att_w, has_prev):
    o_ref, win_ref, full_ref, kwin_ref, pe_term_ref, sem_ref = rest[(1 if has_prev else 0):]
    n_kv_streams, n_win_streams, n_cmp_streams = 4 * N_KV, 2 * N_KV, 2 * N_KV
    past = n_pages * page
    ktot = nsp * CMP_STRIDE
    scale = HEAD_DIM ** -0.5
    rep = lambda a: jnp.concatenate([a] * gqa, axis=0)
    b = pl.program_id(0)
    n_b = pl.num_programs(0)
    slot = b % 2
    copies = lambda bb, sl: _page_copies(pt_ref, cache_ref, full_ref, sem_ref, layer, bb, sl, n_pages, page)

    @pl.when(b == 0)
    def _():
        for br in range(2):
            pe_term_ref[br] = jnp.broadcast_to(_pe_term(pe_ref[br], w1_ref[br]), (8, HEAD_DIM))
        for cp in copies(0, 0):
            cp.start()

    for parity in range(2):
        @pl.when((slot == parity) & (b + 1 < n_b))
        def _(parity=parity):
            for cp in copies(b + 1, 1 - parity):
                cp.start()

    for parity in range(2):
        @pl.when(slot == parity)
        def _(parity=parity):
            for cp in copies(b, parity):
                cp.wait()

    for s in range(n_kv_streams):
        full_ref[slot, s, pl.ds(past, ts), :] = newkv_ref[pl.ds(s, ts, stride=n_kv_streams), :]
        full_ref[slot, s, pl.ds(past + ts, ktot - past - ts), :] = jnp.zeros((ktot - past - ts, HEAD_DIM), F32)
    for s in range(n_win_streams):
        kwin_ref[s, pl.ds(0, wbuf), :] = cw_ref[pl.ds(s, wbuf, stride=n_win_streams), :]
        kwin_ref[s, pl.ds(wbuf, ts), :] = neww_ref[pl.ds(s, ts, stride=n_win_streams), :]
        kwin_ref[s, pl.ds(wbuf + ts, LANES - ts), :] = jnp.zeros((LANES - ts, HEAD_DIM), F32)
    keep = (wbuf - ts) * n_win_streams
    win_ref[pl.ds(0, keep), :] = cw_ref[pl.ds(ts * n_win_streams, keep), :]
    win_ref[pl.ds(keep, ts * n_win_streams), :] = neww_ref[...]

    tok = past + lax.broadcasted_iota(jnp.int32, (T_PAD, 1), 0)
    qpos_rows = rep(tok)
    kpos = lax.broadcasted_iota(jnp.int32, (1, ktot), 1)
    widx = lax.broadcasted_iota(jnp.int32, (1, wbuf + LANES), 1)
    wpos = past - wbuf + widx
    wdist = tok - wpos
    wbias = rep(_bias((wdist >= 0) & (wdist < WINDOW) & (wpos >= 0) & (widx < wbuf + ts)))

    groups = range(N_KV)
    cmp_kv = [[_compress_block(
        (lambda r, s=br * N_KV + g: full_ref[slot, s, pl.ds(r, nsp, stride=CMP_STRIDE), :]),
        pe_term_ref[br][0:1, :], w1_ref[br], w2_ref[br], nsp) for br in range(2)] for g in groups]
    qs = [(q_ref[g] * scale).astype(BF16) for g in groups]
    cmp_out = [_compressed_branch(qs[g], cmp_kv[g][0], cmp_kv[g][1], qpos_rows, nc, gqa) for g in groups]
    sels = [_select_blocks_lanes(cmp_out[g][1], tok, nb) for g in groups]
    sbias = [rep(_bias((_dot(sels[g], exp_ref[...]) > 0.5) & (kpos <= tok))) for g in groups]
    slc = [_softmax_parts(_dot_nt(qs[g], full_ref[slot, n_cmp_streams + g].astype(BF16)) + sbias[g]) for g in groups]
    o_slc = [_dot(slc[g][0].astype(BF16), full_ref[slot, n_cmp_streams + N_KV + g].astype(BF16)) / slc[g][1] for g in groups]
    win = [_softmax_parts(_dot_nt(qs[g], kwin_ref[g].astype(BF16)) + wbias) for g in groups]
    o_win = [_dot(win[g][0].astype(BF16), kwin_ref[N_KV + g].astype(BF16)) / win[g][1] for g in groups]
    ys = []
    for g in groups:
        gcol = gcol_ref[g]
        o = gcol[:, 0:1] * cmp_out[g][0] + gcol[:, 1:2] * o_slc[g] + gcol[:, 2:3] * o_win[g]
        ys.append(o * _silu(z_ref[g]))

    ss = None
    for y in ys:
        rs = jnp.sum(y * y, axis=-1, keepdims=True)
        for r in range(gqa):
            part = rs[r * T_PAD:(r + 1) * T_PAD]
            ss = part if ss is None else ss + part
    inv = rep(lax.rsqrt(ss / att_w + EPS))
    for g in range(N_KV):
        o_ref[g] = (ys[g] * inv) * ggrp_ref[g]


def _attention_sample(layer, page_table, cache_row, win_rows, newkv_rows, neww_rows, q_r, gcol_r, z_r,
                      ggrp_r, pe, w1, w2, expand, ts, att_w, win_prev):
    db = q_r.shape[0]
    n_pages = page_table.shape[1]
    n_kv_streams, n_win_streams = 4 * N_KV, 2 * N_KV
    page = cache_row.shape[2]
    past = n_pages * page
    wbuf = win_rows.shape[2] // n_win_streams
    gqa = att_w // HEAD_DIM // N_KV
    nsub = -(-(past + ts) // CMP_STRIDE)
    nc = nsub - CMP_LEN // CMP_STRIDE + 1
    nsp = -(-nsub // 8) * 8
    nb = max(-(-(past + ts) // SEL_BLOCK), N_SELECT)
    rows = gqa * T_PAD
    kern = functools.partial(_attn_sample_kernel, layer=layer, n_pages=n_pages, page=page, ts=ts, wbuf=wbuf,
                             nsp=nsp, nc=nc, nb=nb, gqa=gqa, att_w=att_w, has_prev=win_prev is not None)
    per_b = lambda shape: pl.BlockSpec((None,) + shape, lambda b, pt: (b,) + (0,) * len(shape))
    const = lambda shape: pl.BlockSpec(shape, lambda b, pt: (0,) * len(shape))
    in_specs = [
        per_b((N_KV, rows, HEAD_DIM)),
        per_b((N_KV, rows, HEAD_DIM)),
        per_b((N_KV, rows, HEAD_DIM)),
        const((N_KV, rows, HEAD_DIM)),
        per_b((ts * n_kv_streams, HEAD_DIM)),
        per_b((ts * n_win_streams, HEAD_DIM)),
        pl.BlockSpec((None, None, wbuf * n_win_streams, HEAD_DIM), lambda b, pt: (layer, b, 0, 0)),
        const((2, CMP_LEN, HEAD_DIM)),
        const((2, CMP_STRIDE * HEAD_DIM, 2 * HEAD_DIM)),
        const((2, HEAD_DIM, HEAD_DIM)),
        const((LANES, nsp * CMP_STRIDE)),
        pl.BlockSpec(memory_space=pl.ANY),
    ]
    args = [page_table.reshape(-1), q_r, gcol_r, z_r, ggrp_r, newkv_rows, neww_rows, win_rows, pe, w1, w2,
            expand, cache_row]
    aliases = {}
    if win_prev is not None:
        in_specs.append(pl.BlockSpec(memory_space=pl.ANY))
        aliases = {len(args): 1}
        args.append(win_prev)
    depth = win_rows.shape[0]
    grid_spec = pltpu.PrefetchScalarGridSpec(
        num_scalar_prefetch=1,
        grid=(db,),
        in_specs=in_specs,
        out_specs=[per_b((N_KV, rows, HEAD_DIM)),
                   pl.BlockSpec((None, None, wbuf * n_win_streams, HEAD_DIM), lambda b, pt: (layer, b, 0, 0))],
        scratch_shapes=[pltpu.VMEM((2, n_kv_streams, nsp * CMP_STRIDE, HEAD_DIM), F32),
                        pltpu.VMEM((n_win_streams, wbuf + LANES, HEAD_DIM), F32),
                        pltpu.VMEM((2, 8, HEAD_DIM), F32),
                        pltpu.SemaphoreType.DMA((2,))],
    )
    return pl.pallas_call(
        kern,
        grid_spec=grid_spec,
        out_shape=[jax.ShapeDtypeStruct((db, N_KV, rows, HEAD_DIM), F32),
                   jax.ShapeDtypeStruct((depth, db, wbuf * n_win_streams, HEAD_DIM), F32)],
        input_output_aliases=aliases,
        compiler_params=_cparams(("arbitrary",)),
        name="attn_sample",
    )(*args)


def _merge_kernel(ya_ref, yb_ref, x_ref, gate_ref, w_ref, g_ref, o_ref, *, conv_ch):
    y = (_dot(ya_ref[...].astype(BF16), w_ref[:conv_ch, :])
         + _dot(yb_ref[...].astype(BF16), w_ref[conv_ch:, :]))
    yn = y * lax.rsqrt(jnp.mean(y * y, axis=-1, keepdims=True) + EPS)
    o_ref[...] = x_ref[...] + gate_ref[...] * (yn * g_ref[...])


def _merge(ya, yb, x2d, gate, w_out, g_post, tm):
    r, d = x2d.shape
    conv_ch = ya.shape[1]
    tiles_per_gate = (r // tm) // gate.shape[0]
    s_rows = gate.shape[1]
    return pl.pallas_call(
        functools.partial(_merge_kernel, conv_ch=conv_ch),
        grid=(r // tm,),
        in_specs=[
            pl.BlockSpec((tm, conv_ch), lambda i: (i, 0)),
            pl.BlockSpec((tm, d - conv_ch), lambda i: (i, 0)),
            pl.BlockSpec((tm, d), lambda i: (i, 0)),
            pl.BlockSpec((None, s_rows, d), lambda i: (i // tiles_per_gate, 0, 0)),
            pl.BlockSpec((d, d), lambda i: (0, 0)),
            pl.BlockSpec((1, d), lambda i: (0, 0)),
        ],
        out_specs=pl.BlockSpec((tm, d), lambda i: (i, 0)),
        out_shape=jax.ShapeDtypeStruct((r, d), F32),
        compiler_params=_cparams(("parallel",)),
        name="merge",
    )(ya, yb, x2d, gate, w_out, g_post)


def _rope_tables(pos):
    half = HEAD_DIM // 2
    inv = jnp.power(ROPE_THETA, -jnp.arange(half, dtype=F32) / half)
    ang = pos.astype(F32)[:, None] * inv
    cos, sin = jnp.cos(ang), jnp.sin(ang)
    return jnp.concatenate([cos, cos], axis=-1), jnp.concatenate([-sin, sin], axis=-1)


def _expand_matrix(n_keys):
    blk = jnp.arange(n_keys, dtype=jnp.int32) // SEL_BLOCK
    return (blk[None, :] == jnp.arange(LANES, dtype=jnp.int32)[:, None]).astype(BF16)


def kernel(x_prompt, x_sample, cache_kv, cache_win, state_conv, page_table, c_prompt, c_sample,
           w_ada, b_ada, g_pre, w_in, conv_w, cmp_pe, cmp_w1, cmp_w2, g_grp, w_out, g_post):
    bp, tp, d = x_prompt.shape
    bs, ts, _ = x_sample.shape
    depth = w_in.shape[0]
    conv_ch = conv_w.shape[2]
    att_w = d - conv_ch
    n_heads = att_w // HEAD_DIM
    gqa = n_heads // N_KV
    n_pages, page = page_table.shape[1], cache_kv.shape[2]
    past = n_pages * page
    wbuf = cache_win.shape[2]
    assert tp % TQ == 0 and tp % KC_SEL == 0 and ts <= T_PAD
    assert att_w == 4 * KV_W and 3 * n_heads <= LANES

    c_q = 4 * conv_ch
    c_kv = c_q + att_w
    c_zb = c_kv + 4 * KV_W
    c_kvw = c_zb + att_w
    cols = {"q": c_q, "kv": c_kv, "kvs": c_kv + 2 * KV_W, "zb": c_zb, "kvw": c_kvw}
    o_kw = c_kv + 4 * KV_W
    o_g = o_kw + 2 * KV_W
    o_zb = o_g + 3 * n_heads
    w_main = jnp.concatenate([w_in[:, :, :o_kw], w_in[:, :, o_zb:], w_in[:, :, o_kw:o_g]], axis=-1).astype(BF16)
    w_gate = jnp.pad(w_in[:, :, o_g:o_zb], ((0, 0), (0, 0), (0, LANES - 3 * n_heads))).astype(BF16)
    w_out_b = w_out.astype(BF16)
    half = CMP_STRIDE * HEAD_DIM
    w1_b = jnp.concatenate([cmp_w1[:, :, :half], cmp_w1[:, :, half:]], axis=-1).astype(BF16)
    w2_b = cmp_w2.astype(BF16)

    mod = _modulation(jnp.concatenate([c_prompt, c_sample], axis=0), w_ada, b_ada)

    cos_p, sin_p = _rope_tables(jnp.arange(tp))
    pos_s = past + jnp.arange(ts)
    cos_s, sin_s = (jnp.tile(a, (bs, 1)) for a in _rope_tables(pos_s))
    expand_p = _expand_matrix(tp)
    nsub_s = -(-(past + ts) // CMP_STRIDE)
    nsp_s = -(-nsub_s // 8) * 8
    expand_s = _expand_matrix(nsp_s * CMP_STRIDE)

    tm_p = min(tp, 1024)
    rows_s = bs * ts
    cache_row = cache_kv.reshape(depth, cache_kv.shape[1], page, 4 * N_KV, HEAD_DIM)
    win_rows = cache_win.reshape(depth, bs, wbuf * 2 * N_KV, HEAD_DIM)

    def to_heads(a):
        a = a.reshape(bs, ts, N_KV, gqa, HEAD_DIM).transpose(0, 2, 3, 1, 4)
        a = jnp.pad(a, ((0, 0), (0, 0), (0, 0), (0, T_PAD - ts), (0, 0)))
        return a.reshape(bs, N_KV, gqa * T_PAD, HEAD_DIM)

    xp = x_prompt.reshape(bp * tp, d)
    xs = x_sample.reshape(rows_s, d)
    win_keep = min(WINDOW, tp)
    rows_p = rows_sm = win_s = None
    conv_p, conv_s = [], []
    for l in range(depth):
        shift, scale, gate = mod[l, :, :d], mod[l, :, d:2 * d], mod[l, :, 2 * d:]
        g_a, g_b = g_grp[l, None, :conv_ch], g_grp[l, None, conv_ch:]

        u, gates, *rows_p = _in_projection(xp, scale[:bp, None], shift[:bp, None], g_pre[l, None], cos_p, sin_p,
                                           w_main[l], w_gate[l], tm_p, conv_ch, att_w, l, depth, bp, win_keep,
                                           rows_p)
        u3 = u.reshape(bp, tp, -1)
        ya, cbuf = _short_conv(u3, jnp.zeros((bp, CONV_WIDTH - 1, conv_ch), F32), conv_w[l], g_a,
                               min(tp, 512), BF16)
        kcmp = _compress_prompt(u3, c_kv, cmp_pe[l], w1_b[l], w2_b[l])
        yb = _attention_prompt(u3, gates.reshape(bp, tp, LANES), kcmp, g_b, expand_p, cols, att_w)
        xp = _merge(ya.reshape(bp * tp, conv_ch), yb.reshape(bp * tp, att_w), xp, gate[:bp, None],
                    w_out_b[l], g_post[l, None], min(tp, 512))
        conv_p.append(cbuf)

        rep = lambda a: jnp.repeat(a[bp:], ts, axis=0)[None]
        u, gates, *rows_sm = _in_projection(xs, rep(scale), rep(shift), g_pre[l, None], cos_s, sin_s,
                                            w_main[l], w_gate[l], rows_s, conv_ch, att_w, l, depth, 1, rows_s,
                                            rows_sm)
        ya, cbuf = _short_conv_step(u, state_conv[l], conv_w[l], g_a, ts)
        q_r = to_heads(u[:, c_q:c_q + att_w])
        z_r = to_heads(u[:, c_zb:c_zb + att_w])
        gcol = gates[:, :3 * n_heads].reshape(rows_s, n_heads, 3)
        gcol_r = to_heads(jnp.pad(gcol, ((0, 0), (0, 0), (0, HEAD_DIM - 3))).reshape(rows_s, att_w))
        ggrp_r = jnp.broadcast_to(g_b.reshape(N_KV, gqa, 1, HEAD_DIM),
                                  (N_KV, gqa, T_PAD, HEAD_DIM)).reshape(N_KV, gqa * T_PAD, HEAD_DIM)
        newkv_rows = rows_sm[0][l].reshape(bs, ts * 4 * N_KV, HEAD_DIM)
        neww_rows = rows_sm[1][l].reshape(bs, ts * 2 * N_KV, HEAD_DIM)
        yb_r, win_s = _attention_sample(l, page_table, cache_row, win_rows, newkv_rows, neww_rows, q_r, gcol_r,
                                        z_r, ggrp_r, cmp_pe[l], w1_b[l], w2_b[l], expand_s, ts, att_w, win_s)
        yb = yb_r.reshape(bs, N_KV, gqa, T_PAD, HEAD_DIM)[:, :, :, :ts].transpose(0, 3, 1, 2, 4)
        xs = _merge(ya, yb.reshape(rows_s, att_w), xs, rep(gate), w_out_b[l], g_post[l, None], rows_s)
        conv_s.append(cbuf)

    kv_shape = lambda b, t: (depth, b, t, 4, N_KV, HEAD_DIM)
    win_shape = lambda b, t: (depth, b, t, 2, N_KV, HEAD_DIM)
    return (xp.reshape(bp, tp, d), xs.reshape(bs, ts, d),
            rows_p[0].reshape(kv_shape(bp, tp)), rows_sm[0].reshape(kv_shape(bs, ts)),
            rows_p[1].reshape(win_shape(bp, win_keep)), win_s.reshape(win_shape(bs, wbuf)),
            jnp.stack(conv_p), jnp.stack(conv_s))
```

```python
import functools

import jax
import jax.numpy as jnp
from jax import lax
from jax.experimental import pallas as pl
from jax.experimental.pallas import tpu as pltpu

F32 = jnp.float32
BF16 = jnp.bfloat16

HEAD_DIM = 128
N_KV = 2
CONV_WIDTH = 3
CMP_LEN = 32
CMP_STRIDE = 16
SEL_BLOCK = 64
N_SELECT = 8
WINDOW = 512
ROPE_THETA = 10000.0
EPS = 1e-6
NEG = -1e30
FORCE_SCORE = 1e3
KV_W = N_KV * HEAD_DIM
LANES = 128
VMEM_LIMIT = 52 * 1024 * 1024

TM_PROJ = 1024
TN_MOD = 768
TR_CONV = 512
TM_MERGE = 512
TQ = 128
KC_SEL = 512
T_PAD = 8


def _cparams(sem):
    return pltpu.CompilerParams(dimension_semantics=sem, vmem_limit_bytes=VMEM_LIMIT)


def _silu(x):
    return x / (1.0 + jnp.exp(-x))


def _sigmoid(x):
    return 1.0 / (1.0 + jnp.exp(-x))


def _dot(a, b):
    return jnp.dot(a, b, preferred_element_type=F32)


def _dot_nt(a, b):
    return lax.dot_general(a, b, (((1,), (1,)), ((), ())), preferred_element_type=F32)


def _mod_kernel(c_ref, w_ref, b_ref, o_ref):
    a = _silu(c_ref[...]).astype(BF16)
    o_ref[...] = _dot(a, w_ref[...].astype(BF16)) + b_ref[...]


def _modulation(c_all, w_ada, b_ada):
    depth, d, n = w_ada.shape
    rows = c_all.shape[0]
    tn = TN_MOD
    return pl.pallas_call(
        _mod_kernel,
        grid=(depth, n // tn),
        in_specs=[
            pl.BlockSpec((rows, d), lambda l, j: (0, 0)),
            pl.BlockSpec((None, d, tn), lambda l, j: (l, 0, j)),
            pl.BlockSpec((None, 1, tn), lambda l, j: (l, 0, j)),
        ],
        out_specs=pl.BlockSpec((None, rows, tn), lambda l, j: (l, 0, j)),
        out_shape=jax.ShapeDtypeStruct((depth, rows, n), F32),
        compiler_params=_cparams(("parallel", "parallel")),
        name="mod",
    )(c_all, w_ada, b_ada.reshape(depth, 1, n))


def _rope_cols(v, cos, sin):
    outs = []
    for h in range(v.shape[1] // HEAD_DIM):
        xh = v[:, h * HEAD_DIM:(h + 1) * HEAD_DIM]
        outs.append(xh * cos + pltpu.roll(xh, HEAD_DIM // 2, 1) * sin)
    return outs[0] if len(outs) == 1 else jnp.concatenate(outs, axis=1)


def _inproj_kernel(*refs, full_lo, full_hi, kv_lo, win_tile, tiles_per_seq, win_rows, has_prev):
    (x_ref, scale_ref, shift_ref, g_ref, cos_ref, sin_ref, w_ref, wt_ref, wg_ref) = refs[:9]
    u_ref, gates_ref, kv_ref, win_ref, h_ref = refs[9 + (2 if has_prev else 0):]
    i = pl.program_id(0)
    j = pl.program_id(1)
    tm = x_ref.shape[0]
    n_kv_streams, n_win_streams = 4 * N_KV, 2 * N_KV

    @pl.when(j == 0)
    def _():
        x = x_ref[...]
        y = x * lax.rsqrt(jnp.mean(x * x, axis=-1, keepdims=True) + EPS)
        h = (y * g_ref[...]) * (1.0 + scale_ref[...]) + shift_ref[...]
        hb = h.astype(BF16)
        h_ref[...] = hb
        gates_ref[...] = _sigmoid(_dot(hb, wg_ref[...]))

    @pl.when(j <= win_tile)
    def _():
        u_ref[...] = _dot(h_ref[...], w_ref[...])

    @pl.when(j > win_tile)
    def _():
        u_ref[...] = _dot(h_ref[...], wt_ref[...])

    @pl.when((j >= full_lo) & (j < full_hi))
    def _():
        u_ref[...] = _rope_cols(u_ref[...], cos_ref[...], sin_ref[...])

    def rope_first_half():
        u_ref[:, :KV_W] = _rope_cols(u_ref[:, :KV_W], cos_ref[...], sin_ref[...])

    for k in range(2):
        @pl.when(j == kv_lo + k)
        def _(k=k):
            rope_first_half()
            for q in range(n_kv_streams // 2):
                kv_ref[pl.ds(k * (n_kv_streams // 2) + q, tm, stride=n_kv_streams), :] = (
                    u_ref[:, q * HEAD_DIM:(q + 1) * HEAD_DIM])

    @pl.when(j == win_tile)
    def _():
        rope_first_half()

        @pl.when(i % tiles_per_seq == tiles_per_seq - 1)
        def _():
            for q in range(n_win_streams):
                win_ref[pl.ds(q, win_rows, stride=n_win_streams), :] = (
                    u_ref[pl.ds(tm - win_rows, win_rows), q * HEAD_DIM:(q + 1) * HEAD_DIM])


def _in_projection(x2d, scale, shift, g_pre, cos, sin, w_head, w_tail, tm, conv_ch, att_w, layer, depth,
                   n_seq, win_rows, prev):
    r, d = x2d.shape
    tn = 2 * KV_W
    n_head = w_head.shape[1] // tn
    n_tail = att_w // tn
    nm = (n_head + n_tail) * tn
    win_tile = n_head - 1
    n_tiles = r // tm
    tiles_per_scale = n_tiles // scale.shape[0]
    tiles_per_seq = n_tiles // n_seq
    s_rows = scale.shape[1]
    pos_tiles = cos.shape[0] // tm
    q0 = 4 * conv_ch // tn
    q1 = q0 + att_w // tn
    assert win_rows <= tm
    assert win_tile == q1 + 2
    kern = functools.partial(_inproj_kernel, full_lo=q0, full_hi=q1, kv_lo=q1, win_tile=win_tile,
                             tiles_per_seq=tiles_per_seq, win_rows=win_rows, has_prev=prev is not None)
    u_tile = lambda j: jnp.where(j == win_tile, n_head + n_tail - 1, jnp.where(j > win_tile, j - 1, j))
    n_kv_streams, n_win_streams = 4 * N_KV, 2 * N_KV
    in_specs = [
        pl.BlockSpec((tm, d), lambda i, j: (i, 0)),
        pl.BlockSpec((None, s_rows, d), lambda i, j: (i // tiles_per_scale, 0, 0)),
        pl.BlockSpec((None, s_rows, d), lambda i, j: (i // tiles_per_scale, 0, 0)),
        pl.BlockSpec((1, d), lambda i, j: (0, 0)),
        pl.BlockSpec((tm, HEAD_DIM), lambda i, j: (i % pos_tiles, 0)),
        pl.BlockSpec((tm, HEAD_DIM), lambda i, j: (i % pos_tiles, 0)),
        pl.BlockSpec((d, tn), lambda i, j: (0, jnp.minimum(j, win_tile))),
        pl.BlockSpec((d, tn), lambda i, j: (0, jnp.maximum(j - n_head, 0))),
        pl.BlockSpec((d, LANES), lambda i, j: (0, att_w // LANES)),
    ]
    args = [x2d, scale, shift, g_pre, cos, sin, w_head, w_tail, w_tail]
    aliases = {}
    if prev is not None:
        in_specs += [pl.BlockSpec(memory_space=pl.ANY), pl.BlockSpec(memory_space=pl.ANY)]
        aliases = {len(args): 2, len(args) + 1: 3}
        args += list(prev)
    return pl.pallas_call(
        kern,
        grid=(n_tiles, nm // tn),
        in_specs=in_specs,
        out_specs=[
            pl.BlockSpec((tm, tn), lambda i, j: (i, u_tile(j))),
            pl.BlockSpec((tm, LANES), lambda i, j: (i, 0)),
            pl.BlockSpec((None, tm * n_kv_streams, HEAD_DIM), lambda i, j: (layer, i, 0)),
            pl.BlockSpec((None, None, win_rows * n_win_streams, HEAD_DIM),
                         lambda i, j: (layer, i // tiles_per_seq, 0, 0)),
        ],
        out_shape=[jax.ShapeDtypeStruct((r, nm), F32), jax.ShapeDtypeStruct((r, LANES), F32),
                   jax.ShapeDtypeStruct((depth, r * n_kv_streams, HEAD_DIM), F32),
                   jax.ShapeDtypeStruct((depth, n_seq, win_rows * n_win_streams, HEAD_DIM), F32)],
        scratch_shapes=[pltpu.VMEM((tm, d), BF16)],
        input_output_aliases=aliases,
        compiler_params=_cparams(("arbitrary", "arbitrary")),
        name="inproj",
    )(*args)


def _conv_kernel(b_ref, c_ref, x_ref, z_ref, init_ref, w_ref, g_ref, ya_ref, st_ref, up_ref, *, tr):
    i = pl.program_id(1)
    pad = 8

    @pl.when(i == 0)
    def _():
        up_ref[pl.ds(pad - 2, 2), :] = init_ref[...]

    up_ref[pl.ds(pad, tr), :] = c_ref[...] * x_ref[...]
    w = w_ref[...]
    y = (w[0:1, :] * up_ref[pl.ds(pad - 2, tr), :]
         + w[1:2, :] * up_ref[pl.ds(pad - 1, tr), :]
         + w[2:3, :] * up_ref[pl.ds(pad, tr), :])
    ya = _silu(z_ref[...]) * (b_ref[...] * y)
    yn = ya * lax.rsqrt(jnp.mean(ya * ya, axis=-1, keepdims=True) + EPS)
    ya_ref[...] = (yn * g_ref[...]).astype(ya_ref.dtype)
    last = up_ref[pl.ds(pad + tr - 2, 2), :]
    st_ref[...] = last
    up_ref[pl.ds(pad - 2, 2), :] = last


def _short_conv(u3, init_state, conv_w, g_a, tr, out_dtype):
    b, t, _ = u3.shape
    c = conv_w.shape[1]
    col = lambda k: pl.BlockSpec((None, tr, c), lambda bi, i, k=k: (bi, i, k))
    return pl.pallas_call(
        functools.partial(_conv_kernel, tr=tr),
        grid=(b, t // tr),
        in_specs=[
            col(0), col(1), col(2), col(3),
            pl.BlockSpec((None, CONV_WIDTH - 1, c), lambda bi, i: (bi, 0, 0)),
            pl.BlockSpec((CONV_WIDTH, c), lambda bi, i: (0, 0)),
            pl.BlockSpec((1, c), lambda bi, i: (0, 0)),
        ],
        out_specs=[
            pl.BlockSpec((None, tr, c), lambda bi, i: (bi, i, 0)),
            pl.BlockSpec((None, CONV_WIDTH - 1, c), lambda bi, i: (bi, 0, 0)),
        ],
        out_shape=[jax.ShapeDtypeStruct((b, t, c), out_dtype),
                   jax.ShapeDtypeStruct((b, CONV_WIDTH - 1, c), F32)],
        scratch_shapes=[pltpu.VMEM((tr + 8, c), F32)],
        compiler_params=_cparams(("parallel", "arbitrary")),
        name="conv",
    )(u3, u3, u3, u3, init_state, conv_w, g_a)


def _conv_step_kernel(b_ref, c_ref, x_ref, z_ref, h1_ref, h2_ref, w_ref, g_ref, ya_ref, uc_ref, *, ts):
    uc = c_ref[...] * x_ref[...]
    t = lax.broadcasted_iota(jnp.int32, uc.shape, 0) % ts
    back1 = jnp.where(t < 1, h1_ref[...], pltpu.roll(uc, 1, 0))
    back2 = jnp.where(t < 2, h2_ref[...], pltpu.roll(uc, 2, 0))
    w = w_ref[...]
    y = w[0:1, :] * back2 + w[1:2, :] * back1 + w[2:3, :] * uc
    ya = _silu(z_ref[...]) * (b_ref[...] * y)
    yn = ya * lax.rsqrt(jnp.mean(ya * ya, axis=-1, keepdims=True) + EPS)
    ya_ref[...] = yn * g_ref[...]
    uc_ref[...] = uc


def _short_conv_step(u2d, init_state, conv_w, g_a, ts):
    r = u2d.shape[0]
    nseq, _, c = init_state.shape
    assert ts >= CONV_WIDTH - 1
    zero = jnp.zeros((nseq, ts - 1, c), F32)
    h1 = jnp.concatenate([init_state[:, 1:2], zero], axis=1).reshape(r, c)
    h2 = jnp.concatenate([init_state, zero[:, 1:]], axis=1).reshape(r, c)
    col = lambda k: pl.BlockSpec((r, c), lambda i, k=k: (0, k))
    full = lambda shape: pl.BlockSpec(shape, lambda i: (0,) * len(shape))
    ya, uc = pl.pallas_call(
        functools.partial(_conv_step_kernel, ts=ts),
        grid=(1,),
        in_specs=[col(0), col(1), col(2), col(3), full((r, c)), full((r, c)), full(conv_w.shape), full(g_a.shape)],
        out_specs=[full((r, c)), full((r, c))],
        out_shape=[jax.ShapeDtypeStruct((r, c), F32), jax.ShapeDtypeStruct((r, c), F32)],
        compiler_params=_cparams(("arbitrary",)),
        name="conv_step",
    )(u2d, u2d, u2d, u2d, h1, h2, conv_w, g_a)
    return ya, uc.reshape(nseq, ts, c)[:, ts - (CONV_WIDTH - 1):]


def _pe_term(pe, w1cat):
    flat = lambda lo: jnp.concatenate([pe[lo + r:lo + r + 1, :] for r in range(CMP_STRIDE)], axis=1)
    rows = jnp.concatenate([flat(0), flat(CMP_STRIDE), jnp.zeros((6, CMP_STRIDE * HEAD_DIM), F32)], axis=0)
    r = _dot(rows.astype(BF16), w1cat)
    return r[0:1, :HEAD_DIM] + r[1:2, HEAD_DIM:]


def _compress_block(load_sub, pe_term, w1cat, w2, nsp):
    a = jnp.concatenate([load_sub(r).astype(BF16) for r in range(CMP_STRIDE)], axis=1)
    pq = _dot(a, w1cat)
    pre = pq[:, :HEAD_DIM] + pltpu.roll(pq[:, HEAD_DIM:], nsp - 1, 0) + pe_term
    return _dot(_silu(pre).astype(BF16), w2)


def _compress_kernel(k_ref, pe_ref, w1_ref, w2_ref, o_ref, *, nsp):
    load_sub = lambda r: k_ref[pl.ds(r, nsp, stride=CMP_STRIDE), :]
    w1cat = w1_ref[...]
    o_ref[...] = _compress_block(load_sub, _pe_term(pe_ref[...], w1cat), w1cat, w2_ref[...], nsp)


def _compress_prompt(u3, kv_col0, cmp_pe, cmp_w1, cmp_w2):
    b, t, _ = u3.shape
    nsp = t // CMP_STRIDE
    blk0 = kv_col0 // HEAD_DIM
    return pl.pallas_call(
        functools.partial(_compress_kernel, nsp=nsp),
        grid=(b, 2 * N_KV),
        in_specs=[
            pl.BlockSpec((None, t, HEAD_DIM), lambda bi, s: (bi, 0, blk0 + s)),
            pl.BlockSpec((None, CMP_LEN, HEAD_DIM), lambda bi, s: (s // N_KV, 0, 0)),
            pl.BlockSpec((None, CMP_STRIDE * HEAD_DIM, 2 * HEAD_DIM), lambda bi, s: (s // N_KV, 0, 0)),
            pl.BlockSpec((None, HEAD_DIM, HEAD_DIM), lambda bi, s: (s // N_KV, 0, 0)),
        ],
        out_specs=pl.BlockSpec((None, None, nsp, HEAD_DIM), lambda bi, s: (bi, s, 0, 0)),
        out_shape=jax.ShapeDtypeStruct((b, 2 * N_KV, nsp, HEAD_DIM), F32),
        compiler_params=_cparams(("parallel", "parallel")),
        name="compress",
    )(u3, cmp_pe, cmp_w1, cmp_w2)


def _bias(mask):
    return jnp.where(mask, 0.0, NEG)


def _row_max(s):
    return jnp.maximum(jnp.max(s, axis=-1, keepdims=True), 0.1 * NEG)


def _softmax_parts(s):
    e = jnp.exp(s - _row_max(s))
    return e, jnp.maximum(jnp.sum(e, axis=-1, keepdims=True), 1e-30)


def _fold_lanes(x, op):
    out = x[:, :LANES]
    for c in range(1, x.shape[1] // LANES):
        out = op(out, x[:, c * LANES:(c + 1) * LANES])
    return out


def _compressed_branch(qg, kc, vc, qpos_rows, nc, n_rep):
    ncp = min(kc.shape[0], -(-nc // LANES) * LANES)
    assert ncp >= nc
    kc, vc = kc[:ncp], vc[:ncp]
    cidx = lax.broadcasted_iota(jnp.int32, (1, ncp), 1)
    mask = ((cidx * CMP_STRIDE + (CMP_LEN - 1)) <= qpos_rows) & (cidx < nc)
    e, l = _softmax_parts(_dot_nt(qg, kc.astype(BF16)) + _bias(mask))
    p = e / l
    o = _dot(p.astype(BF16), vc.astype(BF16))
    rt = qg.shape[0] // n_rep
    p_tok = p[0:rt]
    for r in range(1, n_rep):
        p_tok = p_tok + p[r * rt:(r + 1) * rt]
    return o, p_tok


def _pool_matrix(rows, cols, blocks_on_rows):
    per = SEL_BLOCK // CMP_STRIDE
    r = lax.broadcasted_iota(jnp.int32, (rows, cols), 0)
    c = lax.broadcasted_iota(jnp.int32, (rows, cols), 1)
    hit = (c // per == r) if blocks_on_rows else (r // per == c)
    return jnp.where(hit, 1.0, 0.0).astype(F32)


def _top_blocks(score, blk, nb, axis):
    rank = jnp.zeros(score.shape, jnp.int32)
    for j in range(nb):
        one = score[j:j + 1, :] if axis == 0 else score[:, j:j + 1]
        beats = (one > score) | ((one == score) & (blk > j))
        rank = rank + jnp.where(beats, 1, 0)
    return jnp.where((rank < N_SELECT) & (score > 0.5 * NEG), 1.0, 0.0)


def _block_scores(imp, blk, cur):
    forced = (blk == 0) | (blk == cur) | (blk == cur - 1)
    return jnp.where(blk <= cur, jnp.where(forced, FORCE_SCORE, imp), NEG)


def _select_blocks_rows(p_tok, tok_row, nb):
    rt, ncp = p_tok.shape
    nbr = -(-nb // 8) * 8
    imp_t = lax.dot_general(_pool_matrix(nbr, ncp, True), p_tok, (((1,), (1,)), ((), ())),
                            preferred_element_type=F32, precision=lax.Precision.HIGHEST)
    blk = lax.broadcasted_iota(jnp.int32, (nbr, rt), 0)
    sel_t = _top_blocks(_block_scores(imp_t, blk, tok_row // SEL_BLOCK), blk, nb, 0)
    if nbr < LANES:
        sel_t = jnp.concatenate([sel_t, jnp.zeros((LANES - nbr, rt), F32)], axis=0)
    return sel_t.T.astype(BF16)


def _select_blocks_lanes(p_tok, tok_col, nb):
    rt, ncp = p_tok.shape
    imp = jnp.dot(p_tok, _pool_matrix(ncp, LANES, False), preferred_element_type=F32,
                  precision=lax.Precision.HIGHEST)
    blk = lax.broadcasted_iota(jnp.int32, (rt, LANES), 1)
    return _top_blocks(_block_scores(imp, blk, tok_col // SEL_BLOCK), blk, nb, 1).astype(BF16)


def _attn_prompt_kernel(q_ref, kvs_ref, kvw_ref, cmp_ref, gates_ref, z_ref, ggrp_ref, exp_ref, o_ref,
                        s_ref, *, nc, nb, gqa, att_w, t, wk):
    i = pl.program_id(1)
    s0 = i * TQ
    scale = HEAD_DIM ** -0.5
    tok = lax.broadcasted_iota(jnp.int32, (TQ, 1), 0) + s0
    tok_row = lax.broadcasted_iota(jnp.int32, (1, TQ), 1) + s0
    qpos_rows = jnp.concatenate([tok] * gqa, axis=0)
    rows = TQ * gqa
    rep = lambda a: jnp.concatenate([a] * gqa, axis=0)
    kcol = lambda g: slice(g * HEAD_DIM, (g + 1) * HEAD_DIM)
    vcol = lambda g: slice(KV_W + g * HEAD_DIM, KV_W + (g + 1) * HEAD_DIM)
    groups = range(N_KV)

    qs = [(jnp.concatenate(
        [q_ref[:, (g * gqa + r) * HEAD_DIM:(g * gqa + r + 1) * HEAD_DIM] for r in range(gqa)],
        axis=0) * scale).astype(BF16) for g in groups]

    o_cmp, sel = [], []
    for g in groups:
        o, p_tok = _compressed_branch(qs[g], cmp_ref[g], cmp_ref[N_KV + g], qpos_rows, nc, gqa)
        o_cmp.append(o)
        sel.append(_select_blocks_rows(p_tok, tok_row, nb))

    w_lo = pl.multiple_of(jnp.clip(s0 - WINDOW, 0, t - wk), TQ)
    wpos = w_lo + lax.broadcasted_iota(jnp.int32, (1, wk), 1)
    wdist = tok - wpos
    wbias = rep(_bias((wdist >= 0) & (wdist < WINDOW)))
    o_win = []
    for g in groups:
        kw = kvw_ref[pl.ds(w_lo, wk), kcol(g)].astype(BF16)
        vw = kvw_ref[pl.ds(w_lo, wk), vcol(g)].astype(BF16)
        e, l = _softmax_parts(_dot_nt(qs[g], kw) + wbias)
        o_win.append(_dot(e.astype(BF16), vw) / l)

    head_rows = lambda a, r: a[r * TQ:(r + 1) * TQ]
    gate = lambda h, k: gates_ref[:, 3 * h + k:3 * h + k + 1]
    partial, zact = [], []
    for g in groups:
        for r in range(gqa):
            h = g * gqa + r
            partial.append(gate(h, 0) * head_rows(o_cmp[g], r) + gate(h, 2) * head_rows(o_win[g], r))
            zact.append(_silu(z_ref[:, h * HEAD_DIM:(h + 1) * HEAD_DIM]))

    n_sel = (s0 + TQ + KC_SEL - 1) // KC_SEL

    def score_body(ci, mruns):
        k0 = pl.multiple_of(ci * KC_SEL, KC_SEL)
        causal = (k0 + lax.broadcasted_iota(jnp.int32, (1, KC_SEL), 1)) <= tok
        out = []
        for g in groups:
            k = kvs_ref[pl.ds(k0, KC_SEL), kcol(g)].astype(BF16)
            in_sel = _dot(sel[g], exp_ref[:, pl.ds(k0, KC_SEL)])
            s = _dot_nt(qs[g], k) + rep(_bias((in_sel > 0.5) & causal))
            s_ref[g, :, pl.ds(k0, KC_SEL)] = s
            out.append(jnp.maximum(mruns[g], _fold_lanes(s, jnp.maximum)))
        return tuple(out)

    mruns = lax.fori_loop(0, n_sel, score_body, tuple(jnp.full((rows, LANES), NEG, F32) for _ in groups))
    ms = [_row_max(mruns[g]) for g in groups]

    def value_body(ci, carry):
        k0 = pl.multiple_of(ci * KC_SEL, KC_SEL)
        out = []
        for g in groups:
            lrun, acc = carry[g]
            v = kvs_ref[pl.ds(k0, KC_SEL), vcol(g)].astype(BF16)
            e = jnp.exp(s_ref[g, :, pl.ds(k0, KC_SEL)] - ms[g])
            out.append((lrun + _fold_lanes(e, jnp.add), acc + _dot(e.astype(BF16), v)))
        return tuple(out)

    init = tuple((jnp.zeros((rows, LANES), F32), jnp.zeros((rows, HEAD_DIM), F32)) for _ in groups)
    sums = lax.fori_loop(0, n_sel, value_body, init)

    heads = []
    for g in groups:
        lrun, acc = sums[g]
        o_slc = acc / jnp.maximum(jnp.sum(lrun, axis=-1, keepdims=True), 1e-30)
        for r in range(gqa):
            h = g * gqa + r
            heads.append((partial[h] + gate(h, 1) * head_rows(o_slc, r)) * zact[h])

    ss = jnp.sum(heads[0] * heads[0], axis=-1, keepdims=True)
    for y in heads[1:]:
        ss = ss + jnp.sum(y * y, axis=-1, keepdims=True)
    inv = lax.rsqrt(ss / att_w + EPS)
    o_ref[...] = jnp.concatenate(
        [(y * inv) * ggrp_ref[:, h * HEAD_DIM:(h + 1) * HEAD_DIM] for h, y in enumerate(heads)],
        axis=1).astype(o_ref.dtype)


def _attention_prompt(u3, gates3, kcmp, g_b, expand, cols, att_w):
    b, t, _ = u3.shape
    gqa = att_w // HEAD_DIM // N_KV
    nsub = t // CMP_STRIDE
    nc = nsub - CMP_LEN // CMP_STRIDE + 1
    nb = max(-(-t // SEL_BLOCK), N_SELECT)
    wk = min(t, WINDOW + TQ)
    kern = functools.partial(_attn_prompt_kernel, nc=nc, nb=nb, gqa=gqa, att_w=att_w, t=t, wk=wk)
    two_kv = 2 * KV_W
    return pl.pallas_call(
        kern,
        grid=(b, t // TQ),
        in_specs=[
            pl.BlockSpec((None, TQ, att_w), lambda bi, i: (bi, i, cols["q"] // att_w)),
            pl.BlockSpec((None, t, two_kv), lambda bi, i: (bi, 0, cols["kvs"] // two_kv)),
            pl.BlockSpec((None, t, two_kv), lambda bi, i: (bi, 0, cols["kvw"] // two_kv)),
            pl.BlockSpec((None, 2 * N_KV, kcmp.shape[2], HEAD_DIM), lambda bi, i: (bi, 0, 0, 0)),
            pl.BlockSpec((None, TQ, LANES), lambda bi, i: (bi, i, 0)),
            pl.BlockSpec((None, TQ, att_w), lambda bi, i: (bi, i, cols["zb"] // att_w)),
            pl.BlockSpec((1, att_w), lambda bi, i: (0, 0)),
            pl.BlockSpec((LANES, t), lambda bi, i: (0, 0)),
        ],
        out_specs=pl.BlockSpec((None, TQ, att_w), lambda bi, i: (bi, i, 0)),
        out_shape=jax.ShapeDtypeStruct((b, t, att_w), BF16),
        scratch_shapes=[pltpu.VMEM((N_KV, TQ * gqa, t), F32)],
        compiler_params=_cparams(("parallel", "arbitrary")),
        name="attn_prompt",
    )(u3, u3, u3, kcmp, gates3, u3, g_b, expand)


def _page_copies(pt_ref, cache_ref, full_ref, sem_ref, layer, b, slot, n_pages, page):
    copies = []
    for p in range(n_pages):
        pg = pt_ref[b * n_pages + p]
        for s in range(4 * N_KV):
            copies.append(pltpu.make_async_copy(
                cache_ref.at[layer, pg, :, s, :], full_ref.at[slot, s, pl.ds(p * page, page), :],
                sem_ref.at[slot]))
    return copies


def _attn_sample_kernel(pt_ref, q_ref, gcol_ref, z_ref, ggrp_ref, newkv_ref, neww_ref, cw_ref,
                        pe_ref, w1_ref, w2_ref, exp_ref, cache_ref, *rest,
                        layer, n_pages, page, ts, wbuf, nsp, nc, nb, gqa, att_w, has_prev):
    o_ref, win_ref, full_ref, kwin_ref, pe_term_ref, sem_ref = rest[(1 if has_prev else 0):]
    n_kv_streams, n_win_streams, n_cmp_streams = 4 * N_KV, 2 * N_KV, 2 * N_KV
    past = n_pages * page
    ktot = nsp * CMP_STRIDE
    scale = HEAD_DIM ** -0.5
    rep = lambda a: jnp.concatenate([a] * gqa, axis=0)
    b = pl.program_id(0)
    n_b = pl.num_programs(0)
    slot = b % 2
    copies = lambda bb, sl: _page_copies(pt_ref, cache_ref, full_ref, sem_ref, layer, bb, sl, n_pages, page)

    @pl.when(b == 0)
    def _():
        for br in range(2):
            pe_term_ref[br] = jnp.broadcast_to(_pe_term(pe_ref[br], w1_ref[br]), (8, HEAD_DIM))
        for cp in copies(0, 0):
            cp.start()

    for parity in range(2):
        @pl.when((slot == parity) & (b + 1 < n_b))
        def _(parity=parity):
            for cp in copies(b + 1, 1 - parity):
                cp.start()

    for parity in range(2):
        @pl.when(slot == parity)
        def _(parity=parity):
            for cp in copies(b, parity):
                cp.wait()

    for s in range(n_kv_streams):
        full_ref[slot, s, pl.ds(past, ts), :] = newkv_ref[pl.ds(s, ts, stride=n_kv_streams), :]
        full_ref[slot, s, pl.ds(past + ts, ktot - past - ts), :] = jnp.zeros((ktot - past - ts, HEAD_DIM), F32)
    for s in range(n_win_streams):
        kwin_ref[s, pl.ds(0, wbuf), :] = cw_ref[pl.ds(s, wbuf, stride=n_win_streams), :]
        kwin_ref[s, pl.ds(wbuf, ts), :] = neww_ref[pl.ds(s, ts, stride=n_win_streams), :]
        kwin_ref[s, pl.ds(wbuf + ts, LANES - ts), :] = jnp.zeros((LANES - ts, HEAD_DIM), F32)
    keep = (wbuf - ts) * n_win_streams
    win_ref[pl.ds(0, keep), :] = cw_ref[pl.ds(ts * n_win_streams, keep), :]
    win_ref[pl.ds(keep, ts * n_win_streams), :] = neww_ref[...]

    tok = past + lax.broadcasted_iota(jnp.int32, (T_PAD, 1), 0)
    qpos_rows = rep(tok)
    kpos = lax.broadcasted_iota(jnp.int32, (1, ktot), 1)
    widx = lax.broadcasted_iota(jnp.int32, (1, wbuf + LANES), 1)
    wpos = past - wbuf + widx
    wdist = tok - wpos
    wbias = rep(_bias((wdist >= 0) & (wdist < WINDOW) & (wpos >= 0) & (widx < wbuf + ts)))

    groups = range(N_KV)
    cmp_kv = [[_compress_block(
        (lambda r, s=br * N_KV + g: full_ref[slot, s, pl.ds(r, nsp, stride=CMP_STRIDE), :]),
        pe_term_ref[br][0:1, :], w1_ref[br], w2_ref[br], nsp) for br in range(2)] for g in groups]
    qs = [(q_ref[g] * scale).astype(BF16) for g in groups]
    cmp_out = [_compressed_branch(qs[g], cmp_kv[g][0], cmp_kv[g][1], qpos_rows, nc, gqa) for g in groups]
    sels = [_select_blocks_lanes(cmp_out[g][1], tok, nb) for g in groups]
    sbias = [rep(_bias((_dot(sels[g], exp_ref[...]) > 0.5) & (kpos <= tok))) for g in groups]
    k_slc = lambda g: full_ref[slot, n_cmp_streams + g].astype(BF16)
    v_slc = lambda g: full_ref[slot, n_cmp_streams + N_KV + g].astype(BF16)
    slc = [_softmax_parts(_dot_nt(qs[g], k_slc(g)) + sbias[g]) for g in groups]
    o_slc = [_dot(slc[g][0].astype(BF16), v_slc(g)) / slc[g][1] for g in groups]
    win = [_softmax_parts(_dot_nt(qs[g], kwin_ref[g].astype(BF16)) + wbias) for g in groups]
    o_win = [_dot(win[g][0].astype(BF16), kwin_ref[N_KV + g].astype(BF16)) / win[g][1] for g in groups]
    ys = []
    for g in groups:
        gcol = gcol_ref[g]
        o = gcol[:, 0:1] * cmp_out[g][0] + gcol[:, 1:2] * o_slc[g] + gcol[:, 2:3] * o_win[g]
        ys.append(o * _silu(z_ref[g]))

    ss = None
    for y in ys:
        rs = jnp.sum(y * y, axis=-1, keepdims=True)
        for r in range(gqa):
            part = rs[r * T_PAD:(r + 1) * T_PAD]
            ss = part if ss is None else ss + part
    inv = rep(lax.rsqrt(ss / att_w + EPS))
    for g in range(N_KV):
        o_ref[g] = (ys[g] * inv) * ggrp_ref[g]


def _attention_sample(layer, page_table, cache_row, win_rows, newkv_rows, neww_rows, q_r, gcol_r, z_r,
                      ggrp_r, pe, w1, w2, expand, ts, att_w, win_prev):
    db = q_r.shape[0]
    n_pages = page_table.shape[1]
    n_kv_streams, n_win_streams = 4 * N_KV, 2 * N_KV
    page = cache_row.shape[2]
    past = n_pages * page
    wbuf = win_rows.shape[2] // n_win_streams
    gqa = att_w // HEAD_DIM // N_KV
    nsub = -(-(past + ts) // CMP_STRIDE)
    nc = nsub - CMP_LEN // CMP_STRIDE + 1
    nsp = -(-nsub // 8) * 8
    nb = max(-(-(past + ts) // SEL_BLOCK), N_SELECT)
    rows = gqa * T_PAD
    kern = functools.partial(_attn_sample_kernel, layer=layer, n_pages=n_pages, page=page, ts=ts, wbuf=wbuf,
                             nsp=nsp, nc=nc, nb=nb, gqa=gqa, att_w=att_w, has_prev=win_prev is not None)
    per_b = lambda shape: pl.BlockSpec((None,) + shape, lambda b, pt: (b,) + (0,) * len(shape))
    const = lambda shape: pl.BlockSpec(shape, lambda b, pt: (0,) * len(shape))
    in_specs = [
        per_b((N_KV, rows, HEAD_DIM)),
        per_b((N_KV, rows, HEAD_DIM)),
        per_b((N_KV, rows, HEAD_DIM)),
        const((N_KV, rows, HEAD_DIM)),
        per_b((ts * n_kv_streams, HEAD_DIM)),
        per_b((ts * n_win_streams, HEAD_DIM)),
        pl.BlockSpec((None, None, wbuf * n_win_streams, HEAD_DIM), lambda b, pt: (layer, b, 0, 0)),
        const((2, CMP_LEN, HEAD_DIM)),
        const((2, CMP_STRIDE * HEAD_DIM, 2 * HEAD_DIM)),
        const((2, HEAD_DIM, HEAD_DIM)),
        const((LANES, nsp * CMP_STRIDE)),
        pl.BlockSpec(memory_space=pl.ANY),
    ]
    args = [page_table.reshape(-1), q_r, gcol_r, z_r, ggrp_r, newkv_rows, neww_rows, win_rows, pe, w1, w2,
            expand, cache_row]
    aliases = {}
    if win_prev is not None:
        in_specs.append(pl.BlockSpec(memory_space=pl.ANY))
        aliases = {len(args): 1}
        args.append(win_prev)
    depth = win_rows.shape[0]
    grid_spec = pltpu.PrefetchScalarGridSpec(
        num_scalar_prefetch=1,
        grid=(db,),
        in_specs=in_specs,
        out_specs=[per_b((N_KV, rows, HEAD_DIM)),
                   pl.BlockSpec((None, None, wbuf * n_win_streams, HEAD_DIM), lambda b, pt: (layer, b, 0, 0))],
        scratch_shapes=[pltpu.VMEM((2, n_kv_streams, nsp * CMP_STRIDE, HEAD_DIM), F32),
                        pltpu.VMEM((n_win_streams, wbuf + LANES, HEAD_DIM), F32),
                        pltpu.VMEM((2, 8, HEAD_DIM), F32),
                        pltpu.SemaphoreType.DMA((2,))],
    )
    return pl.pallas_call(
        kern,
        grid_spec=grid_spec,
        out_shape=[jax.ShapeDtypeStruct((db, N_KV, rows, HEAD_DIM), F32),
                   jax.ShapeDtypeStruct((depth, db, wbuf * n_win_streams, HEAD_DIM), F32)],
        input_output_aliases=aliases,
        compiler_params=_cparams(("arbitrary",)),
        name="attn_sample",
    )(*args)


def _merge_kernel(ya_ref, yb_ref, x_ref, gate_ref, w_ref, g_ref, o_ref, *, conv_ch):
    y = (_dot(ya_ref[...].astype(BF16), w_ref[:conv_ch, :])
         + _dot(yb_ref[...].astype(BF16), w_ref[conv_ch:, :]))
    yn = y * lax.rsqrt(jnp.mean(y * y, axis=-1, keepdims=True) + EPS)
    o_ref[...] = x_ref[...] + gate_ref[...] * (yn * g_ref[...])


def _merge(ya, yb, x2d, gate, w_out, g_post, tm):
    r, d = x2d.shape
    conv_ch = ya.shape[1]
    tiles_per_gate = (r // tm) // gate.shape[0]
    s_rows = gate.shape[1]
    return pl.pallas_call(
        functools.partial(_merge_kernel, conv_ch=conv_ch),
        grid=(r // tm,),
        in_specs=[
            pl.BlockSpec((tm, conv_ch), lambda i: (i, 0)),
            pl.BlockSpec((tm, d - conv_ch), lambda i: (i, 0)),
            pl.BlockSpec((tm, d), lambda i: (i, 0)),
            pl.BlockSpec((None, s_rows, d), lambda i: (i // tiles_per_gate, 0, 0)),
            pl.BlockSpec((d, d), lambda i: (0, 0)),
            pl.BlockSpec((1, d), lambda i: (0, 0)),
        ],
        out_specs=pl.BlockSpec((tm, d), lambda i: (i, 0)),
        out_shape=jax.ShapeDtypeStruct((r, d), F32),
        compiler_params=_cparams(("parallel",)),
        name="merge",
    )(ya, yb, x2d, gate, w_out, g_post)


def _rope_tables(pos):
    half = HEAD_DIM // 2
    inv = jnp.power(ROPE_THETA, -jnp.arange(half, dtype=F32) / half)
    ang = pos.astype(F32)[:, None] * inv
    cos, sin = jnp.cos(ang), jnp.sin(ang)
    return jnp.concatenate([cos, cos], axis=-1), jnp.concatenate([-sin, sin], axis=-1)


def _expand_matrix(n_keys):
    blk = jnp.arange(n_keys, dtype=jnp.int32) // SEL_BLOCK
    return (blk[None, :] == jnp.arange(LANES, dtype=jnp.int32)[:, None]).astype(BF16)


def kernel(x_prompt, x_sample, cache_kv, cache_win, state_conv, page_table, c_prompt, c_sample,
           w_ada, b_ada, g_pre, w_in, conv_w, cmp_pe, cmp_w1, cmp_w2, g_grp, w_out, g_post):
    bp, tp, d = x_prompt.shape
    bs, ts, _ = x_sample.shape
    depth = w_in.shape[0]
    conv_ch = conv_w.shape[2]
    att_w = d - conv_ch
    n_heads = att_w // HEAD_DIM
    gqa = n_heads // N_KV
    n_pages, page = page_table.shape[1], cache_kv.shape[2]
    past = n_pages * page
    wbuf = cache_win.shape[2]
    assert tp % TQ == 0 and tp % KC_SEL == 0 and ts <= T_PAD
    assert att_w == 4 * KV_W and 3 * n_heads <= LANES

    c_q = 4 * conv_ch
    c_kv = c_q + att_w
    c_zb = c_kv + 4 * KV_W
    c_kvw = c_zb + att_w
    cols = {"q": c_q, "kv": c_kv, "kvs": c_kv + 2 * KV_W, "zb": c_zb, "kvw": c_kvw}
    o_kw = c_kv + 4 * KV_W
    o_g = o_kw + 2 * KV_W
    o_zb = o_g + 3 * n_heads
    w_head = w_in[:, :, :o_g].astype(BF16)
    w_tail = jnp.concatenate([w_in[:, :, o_zb:], w_in[:, :, o_g:o_zb],
                              jnp.zeros((depth, d, LANES - 3 * n_heads), F32)], axis=-1).astype(BF16)
    w_out_b = w_out.astype(BF16)
    half = CMP_STRIDE * HEAD_DIM
    w1_b = jnp.concatenate([cmp_w1[:, :, :half], cmp_w1[:, :, half:]], axis=-1).astype(BF16)
    w2_b = cmp_w2.astype(BF16)

    mod = _modulation(jnp.concatenate([c_prompt, c_sample], axis=0), w_ada, b_ada)

    cos_p, sin_p = _rope_tables(jnp.arange(tp))
    pos_s = past + jnp.arange(ts)
    cos_s, sin_s = (jnp.tile(a, (bs, 1)) for a in _rope_tables(pos_s))
    expand_p = _expand_matrix(tp)
    nsub_s = -(-(past + ts) // CMP_STRIDE)
    nsp_s = -(-nsub_s // 8) * 8
    expand_s = _expand_matrix(nsp_s * CMP_STRIDE)

    tm_p = min(tp, TM_PROJ)
    rows_s = bs * ts
    cache_row = cache_kv.reshape(depth, cache_kv.shape[1], page, 4 * N_KV, HEAD_DIM)
    win_rows = cache_win.reshape(depth, bs, wbuf * 2 * N_KV, HEAD_DIM)

    def to_heads(a):
        a = a.reshape(bs, ts, N_KV, gqa, HEAD_DIM).transpose(0, 2, 3, 1, 4)
        a = jnp.pad(a, ((0, 0), (0, 0), (0, 0), (0, T_PAD - ts), (0, 0)))
        return a.reshape(bs, N_KV, gqa * T_PAD, HEAD_DIM)

    xp = x_prompt.reshape(bp * tp, d)
    xs = x_sample.reshape(rows_s, d)
    win_keep = min(WINDOW, tp)
    rows_p = rows_sm = win_s = None
    conv_p, conv_s = [], []
    for l in range(depth):
        shift, scale, gate = mod[l, :, :d], mod[l, :, d:2 * d], mod[l, :, 2 * d:]
        g_a, g_b = g_grp[l, None, :conv_ch], g_grp[l, None, conv_ch:]

        u, gates, *rows_p = _in_projection(xp, scale[:bp, None], shift[:bp, None], g_pre[l, None], cos_p, sin_p,
                                           w_head[l], w_tail[l], tm_p, conv_ch, att_w, l, depth, bp, win_keep,
                                           rows_p)
        u3 = u.reshape(bp, tp, -1)
        ya, cbuf = _short_conv(u3, jnp.zeros((bp, CONV_WIDTH - 1, conv_ch), F32), conv_w[l], g_a,
                               min(tp, TR_CONV), BF16)
        kcmp = _compress_prompt(u3, c_kv, cmp_pe[l], w1_b[l], w2_b[l])
        yb = _attention_prompt(u3, gates.reshape(bp, tp, LANES), kcmp, g_b, expand_p, cols, att_w)
        xp = _merge(ya.reshape(bp * tp, conv_ch), yb.reshape(bp * tp, att_w), xp, gate[:bp, None],
                    w_out_b[l], g_post[l, None], min(tp, TM_MERGE))
        conv_p.append(cbuf)

        rep = lambda a: jnp.repeat(a[bp:], ts, axis=0)[None]
        u, gates, *rows_sm = _in_projection(xs, rep(scale), rep(shift), g_pre[l, None], cos_s, sin_s,
                                            w_head[l], w_tail[l], rows_s, conv_ch, att_w, l, depth, 1, rows_s,
                                            rows_sm)
        ya, cbuf = _short_conv_step(u, state_conv[l], conv_w[l], g_a, ts)
        q_r = to_heads(u[:, c_q:c_q + att_w])
        z_r = to_heads(u[:, c_zb:c_zb + att_w])
        gcol = gates[:, :3 * n_heads].reshape(rows_s, n_heads, 3)
        gcol_r = to_heads(jnp.pad(gcol, ((0, 0), (0, 0), (0, HEAD_DIM - 3))).reshape(rows_s, att_w))
        ggrp_r = jnp.broadcast_to(g_b.reshape(N_KV, gqa, 1, HEAD_DIM),
                                  (N_KV, gqa, T_PAD, HEAD_DIM)).reshape(N_KV, gqa * T_PAD, HEAD_DIM)
        newkv_rows = rows_sm[0][l].reshape(bs, ts * 4 * N_KV, HEAD_DIM)
        neww_rows = rows_sm[1][l].reshape(bs, ts * 2 * N_KV, HEAD_DIM)
        yb_r, win_s = _attention_sample(l, page_table, cache_row, win_rows, newkv_rows, neww_rows, q_r, gcol_r,
                                        z_r, ggrp_r, cmp_pe[l], w1_b[l], w2_b[l], expand_s, ts, att_w, win_s)
        yb = yb_r.reshape(bs, N_KV, gqa, T_PAD, HEAD_DIM)[:, :, :, :ts].transpose(0, 3, 1, 2, 4)
        xs = _merge(ya, yb.reshape(rows_s, att_w), xs, rep(gate), w_out_b[l], g_post[l, None], rows_s)
        conv_s.append(cbuf)

    kv_shape = lambda b, t: (depth, b, t, 4, N_KV, HEAD_DIM)
    win_shape = lambda b, t: (depth, b, t, 2, N_KV, HEAD_DIM)
    return (xp.reshape(bp, tp, d), xs.reshape(bs, ts, d),
            rows_p[0].reshape(kv_shape(bp, tp)), rows_sm[0].reshape(kv_shape(bs, ts)),
            rows_p[1].reshape(win_shape(bp, win_keep)), win_s.reshape(win_shape(bs, wbuf)),
            jnp.stack(conv_p), jnp.stack(conv_s))
```

```python
import functools

import jax
import jax.numpy as jnp
from jax import lax
from jax.experimental import pallas as pl
from jax.experimental.pallas import tpu as pltpu

F32 = jnp.float32
BF16 = jnp.bfloat16

HEAD_DIM = 128
N_KV = 2
CONV_WIDTH = 3
CMP_LEN = 32
CMP_STRIDE = 16
SEL_BLOCK = 64
N_SELECT = 8
WINDOW = 512
ROPE_THETA = 10000.0
EPS = 1e-6
NEG = -1e30
FORCE_SCORE = 1e3
KV_W = N_KV * HEAD_DIM
LANES = 128
VMEM_LIMIT = 52 * 1024 * 1024

TM_PROJ = 1024
TN_MOD = 768
TR_CONV = 512
TM_MERGE = 512
TQ = 128
KC_SEL = 512
T_PAD = 8


def _cparams(sem):
    return pltpu.CompilerParams(dimension_semantics=sem, vmem_limit_bytes=VMEM_LIMIT)


def _silu(x):
    return x / (1.0 + jnp.exp(-x))


def _sigmoid(x):
    return 1.0 / (1.0 + jnp.exp(-x))


def _dot(a, b):
    return jnp.dot(a, b, preferred_element_type=F32)


def _dot_nt(a, b):
    return lax.dot_general(a, b, (((1,), (1,)), ((), ())), preferred_element_type=F32)


def _mod_kernel(c_ref, w_ref, b_ref, o_ref):
    a = _silu(c_ref[...]).astype(BF16)
    o_ref[...] = _dot(a, w_ref[...].astype(BF16)) + b_ref[...]


def _modulation(c_all, w_ada, b_ada):
    depth, d, n = w_ada.shape
    rows = c_all.shape[0]
    tn = TN_MOD
    return pl.pallas_call(
        _mod_kernel,
        grid=(depth, n // tn),
        in_specs=[
            pl.BlockSpec((rows, d), lambda l, j: (0, 0)),
            pl.BlockSpec((None, d, tn), lambda l, j: (l, 0, j)),
            pl.BlockSpec((None, 1, tn), lambda l, j: (l, 0, j)),
        ],
        out_specs=pl.BlockSpec((None, rows, tn), lambda l, j: (l, 0, j)),
        out_shape=jax.ShapeDtypeStruct((depth, rows, n), F32),
        compiler_params=_cparams(("parallel", "parallel")),
        name="mod",
    )(c_all, w_ada, b_ada.reshape(depth, 1, n))


def _rope_cols(v, cos, sin):
    outs = []
    for h in range(v.shape[1] // HEAD_DIM):
        xh = v[:, h * HEAD_DIM:(h + 1) * HEAD_DIM]
        outs.append(xh * cos + pltpu.roll(xh, HEAD_DIM // 2, 1) * sin)
    return outs[0] if len(outs) == 1 else jnp.concatenate(outs, axis=1)


def _inproj_kernel(*refs, full_lo, full_hi, kv_lo, win_tile, tiles_per_seq, win_rows, has_prev):
    (x_ref, scale_ref, shift_ref, g_ref, cos_ref, sin_ref, w_ref, wg_ref) = refs[:8]
    u_ref, gates_ref, kv_ref, win_ref, h_ref = refs[8 + (2 if has_prev else 0):]
    i = pl.program_id(0)
    j = pl.program_id(1)
    tm = x_ref.shape[0]
    n_kv_streams, n_win_streams = 4 * N_KV, 2 * N_KV

    @pl.when(j == 0)
    def _():
        x = x_ref[...]
        y = x * lax.rsqrt(jnp.mean(x * x, axis=-1, keepdims=True) + EPS)
        h = (y * g_ref[...]) * (1.0 + scale_ref[...]) + shift_ref[...]
        hb = h.astype(BF16)
        h_ref[...] = hb
        gates_ref[...] = _sigmoid(_dot(hb, wg_ref[...]))

    u_ref[...] = _dot(h_ref[...], w_ref[...])

    @pl.when((j >= full_lo) & (j < full_hi))
    def _():
        u_ref[...] = _rope_cols(u_ref[...], cos_ref[...], sin_ref[...])

    def rope_first_half():
        u_ref[:, :KV_W] = _rope_cols(u_ref[:, :KV_W], cos_ref[...], sin_ref[...])

    for k in range(2):
        @pl.when(j == kv_lo + k)
        def _(k=k):
            rope_first_half()
            for q in range(n_kv_streams // 2):
                kv_ref[pl.ds(k * (n_kv_streams // 2) + q, tm, stride=n_kv_streams), :] = (
                    u_ref[:, q * HEAD_DIM:(q + 1) * HEAD_DIM])

    @pl.when(j == win_tile)
    def _():
        rope_first_half()

        @pl.when(i % tiles_per_seq == tiles_per_seq - 1)
        def _():
            for q in range(n_win_streams):
                win_ref[pl.ds(q, win_rows, stride=n_win_streams), :] = (
                    u_ref[pl.ds(tm - win_rows, win_rows), q * HEAD_DIM:(q + 1) * HEAD_DIM])


def _in_projection(x2d, scale, shift, g_pre, cos, sin, w_main, w_gate, tm, conv_ch, att_w, layer, depth,
                   n_seq, win_rows, prev):
    r, d = x2d.shape
    nm = w_main.shape[1]
    tn = 2 * KV_W
    n_tiles = r // tm
    tiles_per_scale = n_tiles // scale.shape[0]
    tiles_per_seq = n_tiles // n_seq
    s_rows = scale.shape[1]
    pos_tiles = cos.shape[0] // tm
    q0 = 4 * conv_ch // tn
    q1 = q0 + att_w // tn
    assert win_rows <= tm
    kern = functools.partial(_inproj_kernel, full_lo=q0, full_hi=q1, kv_lo=q1, win_tile=q1 + 2 + att_w // tn,
                             tiles_per_seq=tiles_per_seq, win_rows=win_rows, has_prev=prev is not None)
    n_kv_streams, n_win_streams = 4 * N_KV, 2 * N_KV
    in_specs = [
        pl.BlockSpec((tm, d), lambda i, j: (i, 0)),
        pl.BlockSpec((None, s_rows, d), lambda i, j: (i // tiles_per_scale, 0, 0)),
        pl.BlockSpec((None, s_rows, d), lambda i, j: (i // tiles_per_scale, 0, 0)),
        pl.BlockSpec((1, d), lambda i, j: (0, 0)),
        pl.BlockSpec((tm, HEAD_DIM), lambda i, j: (i % pos_tiles, 0)),
        pl.BlockSpec((tm, HEAD_DIM), lambda i, j: (i % pos_tiles, 0)),
        pl.BlockSpec((d, tn), lambda i, j: (0, j)),
        pl.BlockSpec((d, LANES), lambda i, j: (0, 0)),
    ]
    args = [x2d, scale, shift, g_pre, cos, sin, w_main, w_gate]
    aliases = {}
    if prev is not None:
        in_specs += [pl.BlockSpec(memory_space=pl.ANY), pl.BlockSpec(memory_space=pl.ANY)]
        args += list(prev)
        aliases = {8: 2, 9: 3}
    return pl.pallas_call(
        kern,
        grid=(n_tiles, nm // tn),
        in_specs=in_specs,
        out_specs=[
            pl.BlockSpec((tm, tn), lambda i, j: (i, j)),
            pl.BlockSpec((tm, LANES), lambda i, j: (i, 0)),
            pl.BlockSpec((None, tm * n_kv_streams, HEAD_DIM), lambda i, j: (layer, i, 0)),
            pl.BlockSpec((None, None, win_rows * n_win_streams, HEAD_DIM),
                         lambda i, j: (layer, i // tiles_per_seq, 0, 0)),
        ],
        out_shape=[jax.ShapeDtypeStruct((r, nm), F32), jax.ShapeDtypeStruct((r, LANES), F32),
                   jax.ShapeDtypeStruct((depth, r * n_kv_streams, HEAD_DIM), F32),
                   jax.ShapeDtypeStruct((depth, n_seq, win_rows * n_win_streams, HEAD_DIM), F32)],
        scratch_shapes=[pltpu.VMEM((tm, d), BF16)],
        input_output_aliases=aliases,
        compiler_params=_cparams(("arbitrary", "arbitrary")),
        name="inproj",
    )(*args)


def _conv_kernel(b_ref, c_ref, x_ref, z_ref, init_ref, w_ref, g_ref, ya_ref, st_ref, up_ref, *, tr):
    i = pl.program_id(1)
    pad = 8

    @pl.when(i == 0)
    def _():
        up_ref[pl.ds(pad - 2, 2), :] = init_ref[...]

    up_ref[pl.ds(pad, tr), :] = c_ref[...] * x_ref[...]
    w = w_ref[...]
    y = (w[0:1, :] * up_ref[pl.ds(pad - 2, tr), :]
         + w[1:2, :] * up_ref[pl.ds(pad - 1, tr), :]
         + w[2:3, :] * up_ref[pl.ds(pad, tr), :])
    ya = _silu(z_ref[...]) * (b_ref[...] * y)
    yn = ya * lax.rsqrt(jnp.mean(ya * ya, axis=-1, keepdims=True) + EPS)
    ya_ref[...] = (yn * g_ref[...]).astype(ya_ref.dtype)
    last = up_ref[pl.ds(pad + tr - 2, 2), :]
    st_ref[...] = last
    up_ref[pl.ds(pad - 2, 2), :] = last


def _short_conv(u3, init_state, conv_w, g_a, tr, out_dtype):
    b, t, _ = u3.shape
    c = conv_w.shape[1]
    col = lambda k: pl.BlockSpec((None, tr, c), lambda bi, i, k=k: (bi, i, k))
    return pl.pallas_call(
        functools.partial(_conv_kernel, tr=tr),
        grid=(b, t // tr),
        in_specs=[
            col(0), col(1), col(2), col(3),
            pl.BlockSpec((None, CONV_WIDTH - 1, c), lambda bi, i: (bi, 0, 0)),
            pl.BlockSpec((CONV_WIDTH, c), lambda bi, i: (0, 0)),
            pl.BlockSpec((1, c), lambda bi, i: (0, 0)),
        ],
        out_specs=[
            pl.BlockSpec((None, tr, c), lambda bi, i: (bi, i, 0)),
            pl.BlockSpec((None, CONV_WIDTH - 1, c), lambda bi, i: (bi, 0, 0)),
        ],
        out_shape=[jax.ShapeDtypeStruct((b, t, c), out_dtype),
                   jax.ShapeDtypeStruct((b, CONV_WIDTH - 1, c), F32)],
        scratch_shapes=[pltpu.VMEM((tr + 8, c), F32)],
        compiler_params=_cparams(("parallel", "arbitrary")),
        name="conv",
    )(u3, u3, u3, u3, init_state, conv_w, g_a)


def _conv_step_kernel(b_ref, c_ref, x_ref, z_ref, h1_ref, h2_ref, w_ref, g_ref, ya_ref, uc_ref, *, ts):
    uc = c_ref[...] * x_ref[...]
    t = lax.broadcasted_iota(jnp.int32, uc.shape, 0) % ts
    back1 = jnp.where(t < 1, h1_ref[...], pltpu.roll(uc, 1, 0))
    back2 = jnp.where(t < 2, h2_ref[...], pltpu.roll(uc, 2, 0))
    w = w_ref[...]
    y = w[0:1, :] * back2 + w[1:2, :] * back1 + w[2:3, :] * uc
    ya = _silu(z_ref[...]) * (b_ref[...] * y)
    yn = ya * lax.rsqrt(jnp.mean(ya * ya, axis=-1, keepdims=True) + EPS)
    ya_ref[...] = yn * g_ref[...]
    uc_ref[...] = uc


def _short_conv_step(u2d, init_state, conv_w, g_a, ts):
    r = u2d.shape[0]
    nseq, _, c = init_state.shape
    assert ts >= CONV_WIDTH - 1
    zero = jnp.zeros((nseq, ts - 1, c), F32)
    h1 = jnp.concatenate([init_state[:, 1:2], zero], axis=1).reshape(r, c)
    h2 = jnp.concatenate([init_state, zero[:, 1:]], axis=1).reshape(r, c)
    col = lambda k: pl.BlockSpec((r, c), lambda i, k=k: (0, k))
    full = lambda shape: pl.BlockSpec(shape, lambda i: (0,) * len(shape))
    ya, uc = pl.pallas_call(
        functools.partial(_conv_step_kernel, ts=ts),
        grid=(1,),
        in_specs=[col(0), col(1), col(2), col(3), full((r, c)), full((r, c)), full(conv_w.shape), full(g_a.shape)],
        out_specs=[full((r, c)), full((r, c))],
        out_shape=[jax.ShapeDtypeStruct((r, c), F32), jax.ShapeDtypeStruct((r, c), F32)],
        compiler_params=_cparams(("arbitrary",)),
        name="conv_step",
    )(u2d, u2d, u2d, u2d, h1, h2, conv_w, g_a)
    return ya, uc.reshape(nseq, ts, c)[:, ts - (CONV_WIDTH - 1):]


def _pe_term(pe, w1cat):
    flat = lambda lo: jnp.concatenate([pe[lo + r:lo + r + 1, :] for r in range(CMP_STRIDE)], axis=1)
    rows = jnp.concatenate([flat(0), flat(CMP_STRIDE), jnp.zeros((6, CMP_STRIDE * HEAD_DIM), F32)], axis=0)
    r = _dot(rows.astype(BF16), w1cat)
    return r[0:1, :HEAD_DIM] + r[1:2, HEAD_DIM:]


def _compress_block(load_sub, pe_term, w1cat, w2, nsp):
    a = jnp.concatenate([load_sub(r).astype(BF16) for r in range(CMP_STRIDE)], axis=1)
    pq = _dot(a, w1cat)
    pre = pq[:, :HEAD_DIM] + pltpu.roll(pq[:, HEAD_DIM:], nsp - 1, 0) + pe_term
    return _dot(_silu(pre).astype(BF16), w2)


def _compress_kernel(k_ref, pe_ref, w1_ref, w2_ref, o_ref, *, nsp):
    load_sub = lambda r: k_ref[pl.ds(r, nsp, stride=CMP_STRIDE), :]
    w1cat = w1_ref[...]
    o_ref[...] = _compress_block(load_sub, _pe_term(pe_ref[...], w1cat), w1cat, w2_ref[...], nsp)


def _compress_prompt(u3, kv_col0, cmp_pe, cmp_w1, cmp_w2):
    b, t, _ = u3.shape
    nsp = t // CMP_STRIDE
    blk0 = kv_col0 // HEAD_DIM
    return pl.pallas_call(
        functools.partial(_compress_kernel, nsp=nsp),
        grid=(b, 2 * N_KV),
        in_specs=[
            pl.BlockSpec((None, t, HEAD_DIM), lambda bi, s: (bi, 0, blk0 + s)),
            pl.BlockSpec((None, CMP_LEN, HEAD_DIM), lambda bi, s: (s // N_KV, 0, 0)),
            pl.BlockSpec((None, CMP_STRIDE * HEAD_DIM, 2 * HEAD_DIM), lambda bi, s: (s // N_KV, 0, 0)),
            pl.BlockSpec((None, HEAD_DIM, HEAD_DIM), lambda bi, s: (s // N_KV, 0, 0)),
        ],
        out_specs=pl.BlockSpec((None, None, nsp, HEAD_DIM), lambda bi, s: (bi, s, 0, 0)),
        out_shape=jax.ShapeDtypeStruct((b, 2 * N_KV, nsp, HEAD_DIM), F32),
        compiler_params=_cparams(("parallel", "parallel")),
        name="compress",
    )(u3, cmp_pe, cmp_w1, cmp_w2)


def _bias(mask):
    return jnp.where(mask, 0.0, NEG)


def _row_max(s):
    return jnp.maximum(jnp.max(s, axis=-1, keepdims=True), 0.1 * NEG)


def _softmax_parts(s):
    e = jnp.exp(s - _row_max(s))
    return e, jnp.maximum(jnp.sum(e, axis=-1, keepdims=True), 1e-30)


def _weighted_values(e, v):
    r = _dot(e.astype(BF16), jnp.concatenate([v, jnp.ones_like(v)], axis=1))
    return r[:, :HEAD_DIM], jnp.maximum(r[:, HEAD_DIM:], 1e-30)


def _fold_lanes(x, op):
    out = x[:, :LANES]
    for c in range(1, x.shape[1] // LANES):
        out = op(out, x[:, c * LANES:(c + 1) * LANES])
    return out


def _compressed_branch(qg, kc, vc, qpos_rows, nc, n_rep):
    ncp = min(kc.shape[0], -(-nc // LANES) * LANES)
    assert ncp >= nc
    kc, vc = kc[:ncp], vc[:ncp]
    cidx = lax.broadcasted_iota(jnp.int32, (1, ncp), 1)
    mask = ((cidx * CMP_STRIDE + (CMP_LEN - 1)) <= qpos_rows) & (cidx < nc)
    e, l = _softmax_parts(_dot_nt(qg, kc.astype(BF16)) + _bias(mask))
    p = e / l
    o = _dot(p.astype(BF16), vc.astype(BF16))
    rt = qg.shape[0] // n_rep
    p_tok = p[0:rt]
    for r in range(1, n_rep):
        p_tok = p_tok + p[r * rt:(r + 1) * rt]
    return o, p_tok


def _pool_matrix(rows, cols, blocks_on_rows):
    per = SEL_BLOCK // CMP_STRIDE
    r = lax.broadcasted_iota(jnp.int32, (rows, cols), 0)
    c = lax.broadcasted_iota(jnp.int32, (rows, cols), 1)
    hit = (c // per == r) if blocks_on_rows else (r // per == c)
    return jnp.where(hit, 1.0, 0.0).astype(F32)


def _top_blocks(score, blk, nb, axis):
    rank = jnp.zeros(score.shape, jnp.int32)
    for j in range(nb):
        one = score[j:j + 1, :] if axis == 0 else score[:, j:j + 1]
        beats = (one > score) | ((one == score) & (blk > j))
        rank = rank + jnp.where(beats, 1, 0)
    return jnp.where((rank < N_SELECT) & (score > 0.5 * NEG), 1.0, 0.0)


def _block_scores(imp, blk, cur):
    forced = (blk == 0) | (blk == cur) | (blk == cur - 1)
    return jnp.where(blk <= cur, jnp.where(forced, FORCE_SCORE, imp), NEG)


def _select_blocks_rows(p_tok, tok_row, nb):
    rt, ncp = p_tok.shape
    nbr = -(-nb // 8) * 8
    imp_t = lax.dot_general(_pool_matrix(nbr, ncp, True), p_tok, (((1,), (1,)), ((), ())),
                            preferred_element_type=F32, precision=lax.Precision.HIGHEST)
    blk = lax.broadcasted_iota(jnp.int32, (nbr, rt), 0)
    sel_t = _top_blocks(_block_scores(imp_t, blk, tok_row // SEL_BLOCK), blk, nb, 0)
    if nbr < LANES:
        sel_t = jnp.concatenate([sel_t, jnp.zeros((LANES - nbr, rt), F32)], axis=0)
    return sel_t.T.astype(BF16)


def _select_blocks_lanes(p_tok, tok_col, nb):
    rt, ncp = p_tok.shape
    imp = jnp.dot(p_tok, _pool_matrix(ncp, LANES, False), preferred_element_type=F32,
                  precision=lax.Precision.HIGHEST)
    blk = lax.broadcasted_iota(jnp.int32, (rt, LANES), 1)
    return _top_blocks(_block_scores(imp, blk, tok_col // SEL_BLOCK), blk, nb, 1).astype(BF16)


def _attn_prompt_kernel(q_ref, kvs_ref, kvw_ref, cmp_ref, gates_ref, z_ref, ggrp_ref, exp_ref, o_ref,
                        s_ref, *, nc, nb, gqa, att_w, t, wk):
    i = pl.program_id(1)
    s0 = i * TQ
    scale = HEAD_DIM ** -0.5
    tok = lax.broadcasted_iota(jnp.int32, (TQ, 1), 0) + s0
    tok_row = lax.broadcasted_iota(jnp.int32, (1, TQ), 1) + s0
    qpos_rows = jnp.concatenate([tok] * gqa, axis=0)
    rows = TQ * gqa
    rep = lambda a: jnp.concatenate([a] * gqa, axis=0)
    kcol = lambda g: slice(g * HEAD_DIM, (g + 1) * HEAD_DIM)
    vcol = lambda g: slice(KV_W + g * HEAD_DIM, KV_W + (g + 1) * HEAD_DIM)
    groups = range(N_KV)

    qs = [(jnp.concatenate(
        [q_ref[:, (g * gqa + r) * HEAD_DIM:(g * gqa + r + 1) * HEAD_DIM] for r in range(gqa)],
        axis=0) * scale).astype(BF16) for g in groups]

    o_cmp, sel = [], []
    for g in groups:
        o, p_tok = _compressed_branch(qs[g], cmp_ref[g], cmp_ref[N_KV + g], qpos_rows, nc, gqa)
        o_cmp.append(o)
        sel.append(_select_blocks_rows(p_tok, tok_row, nb))

    w_lo = pl.multiple_of(jnp.clip(s0 - WINDOW, 0, t - wk), TQ)
    wpos = w_lo + lax.broadcasted_iota(jnp.int32, (1, wk), 1)
    wdist = tok - wpos
    wbias = rep(_bias((wdist >= 0) & (wdist < WINDOW)))
    o_win = []
    for g in groups:
        kw = kvw_ref[pl.ds(w_lo, wk), kcol(g)].astype(BF16)
        vw = kvw_ref[pl.ds(w_lo, wk), vcol(g)].astype(BF16)
        sw = _dot_nt(qs[g], kw) + wbias
        num, den = _weighted_values(jnp.exp(sw - _row_max(sw)), vw)
        o_win.append(num / den)

    head_rows = lambda a, r: a[r * TQ:(r + 1) * TQ]
    gate = lambda h, k: gates_ref[:, 3 * h + k:3 * h + k + 1]
    partial, zact = [], []
    for g in groups:
        for r in range(gqa):
            h = g * gqa + r
            partial.append(gate(h, 0) * head_rows(o_cmp[g], r) + gate(h, 2) * head_rows(o_win[g], r))
            zact.append(_silu(z_ref[:, h * HEAD_DIM:(h + 1) * HEAD_DIM]))

    n_sel = (s0 + TQ + KC_SEL - 1) // KC_SEL

    def score_body(ci, mruns):
        k0 = pl.multiple_of(ci * KC_SEL, KC_SEL)
        causal = (k0 + lax.broadcasted_iota(jnp.int32, (1, KC_SEL), 1)) <= tok
        out = []
        for g in groups:
            k = kvs_ref[pl.ds(k0, KC_SEL), kcol(g)].astype(BF16)
            in_sel = _dot(sel[g], exp_ref[:, pl.ds(k0, KC_SEL)])
            s = _dot_nt(qs[g], k) + rep(_bias((in_sel > 0.5) & causal))
            s_ref[g, :, pl.ds(k0, KC_SEL)] = s
            out.append(jnp.maximum(mruns[g], _fold_lanes(s, jnp.maximum)))
        return tuple(out)

    mruns = lax.fori_loop(0, n_sel, score_body, tuple(jnp.full((rows, LANES), NEG, F32) for _ in groups))
    ms = [_row_max(mruns[g]) for g in groups]

    def value_body(ci, carry):
        k0 = pl.multiple_of(ci * KC_SEL, KC_SEL)
        out = []
        for g in groups:
            v = kvs_ref[pl.ds(k0, KC_SEL), vcol(g)].astype(BF16)
            e = jnp.exp(s_ref[g, :, pl.ds(k0, KC_SEL)] - ms[g])
            out.append(carry[g] + _dot(e.astype(BF16), jnp.concatenate([v, jnp.ones_like(v)], axis=1)))
        return tuple(out)

    sums = lax.fori_loop(0, n_sel, value_body, tuple(jnp.zeros((rows, 2 * HEAD_DIM), F32) for _ in groups))

    heads = []
    for g in groups:
        o_slc = sums[g][:, :HEAD_DIM] / jnp.maximum(sums[g][:, HEAD_DIM:], 1e-30)
        for r in range(gqa):
            h = g * gqa + r
            heads.append((partial[h] + gate(h, 1) * head_rows(o_slc, r)) * zact[h])

    ss = jnp.sum(heads[0] * heads[0], axis=-1, keepdims=True)
    for y in heads[1:]:
        ss = ss + jnp.sum(y * y, axis=-1, keepdims=True)
    inv = lax.rsqrt(ss / att_w + EPS)
    o_ref[...] = jnp.concatenate(
        [(y * inv) * ggrp_ref[:, h * HEAD_DIM:(h + 1) * HEAD_DIM] for h, y in enumerate(heads)],
        axis=1).astype(o_ref.dtype)


def _attention_prompt(u3, gates3, kcmp, g_b, expand, cols, att_w):
    b, t, _ = u3.shape
    gqa = att_w // HEAD_DIM // N_KV
    nsub = t // CMP_STRIDE
    nc = nsub - CMP_LEN // CMP_STRIDE + 1
    nb = max(-(-t // SEL_BLOCK), N_SELECT)
    wk = min(t, WINDOW + TQ)
    kern = functools.partial(_attn_prompt_kernel, nc=nc, nb=nb, gqa=gqa, att_w=att_w, t=t, wk=wk)
    two_kv = 2 * KV_W
    return pl.pallas_call(
        kern,
        grid=(b, t // TQ),
        in_specs=[
            pl.BlockSpec((None, TQ, att_w), lambda bi, i: (bi, i, cols["q"] // att_w)),
            pl.BlockSpec((None, t, two_kv), lambda bi, i: (bi, 0, cols["kvs"] // two_kv)),
            pl.BlockSpec((None, t, two_kv), lambda bi, i: (bi, 0, cols["kvw"] // two_kv)),
            pl.BlockSpec((None, 2 * N_KV, kcmp.shape[2], HEAD_DIM), lambda bi, i: (bi, 0, 0, 0)),
            pl.BlockSpec((None, TQ, LANES), lambda bi, i: (bi, i, 0)),
            pl.BlockSpec((None, TQ, att_w), lambda bi, i: (bi, i, cols["zb"] // att_w)),
            pl.BlockSpec((1, att_w), lambda bi, i: (0, 0)),
            pl.BlockSpec((LANES, t), lambda bi, i: (0, 0)),
        ],
        out_specs=pl.BlockSpec((None, TQ, att_w), lambda bi, i: (bi, i, 0)),
        out_shape=jax.ShapeDtypeStruct((b, t, att_w), BF16),
        scratch_shapes=[pltpu.VMEM((N_KV, TQ * gqa, t), F32)],
        compiler_params=_cparams(("parallel", "arbitrary")),
        name="attn_prompt",
    )(u3, u3, u3, kcmp, gates3, u3, g_b, expand)


def _page_copies(pt_ref, cache_ref, full_ref, sem_ref, layer, b, slot, n_pages, page):
    copies = []
    for p in range(n_pages):
        pg = pt_ref[b * n_pages + p]
        for s in range(4 * N_KV):
            copies.append(pltpu.make_async_copy(
                cache_ref.at[layer, pg, :, s, :], full_ref.at[slot, s, pl.ds(p * page, page), :],
                sem_ref.at[slot]))
    return copies


def _attn_sample_kernel(pt_ref, q_ref, gcol_ref, z_ref, ggrp_ref, newkv_ref, neww_ref, cw_ref,
                        pe_ref, w1_ref, w2_ref, exp_ref, cache_ref, *rest,
                        layer, n_pages, page, ts, wbuf, nsp, nc, nb, gqa, att_w, has_prev):
    o_ref, win_ref, full_ref, kwin_ref, pe_term_ref, sem_ref = rest[(1 if has_prev else 0):]
    n_kv_streams, n_win_streams, n_cmp_streams = 4 * N_KV, 2 * N_KV, 2 * N_KV
    past = n_pages * page
    ktot = nsp * CMP_STRIDE
    scale = HEAD_DIM ** -0.5
    rep = lambda a: jnp.concatenate([a] * gqa, axis=0)
    b = pl.program_id(0)
    n_b = pl.num_programs(0)
    slot = b % 2
    copies = lambda bb, sl: _page_copies(pt_ref, cache_ref, full_ref, sem_ref, layer, bb, sl, n_pages, page)

    @pl.when(b == 0)
    def _():
        for br in range(2):
            pe_term_ref[br] = jnp.broadcast_to(_pe_term(pe_ref[br], w1_ref[br]), (8, HEAD_DIM))
        for cp in copies(0, 0):
            cp.start()

    for parity in range(2):
        @pl.when((slot == parity) & (b + 1 < n_b))
        def _(parity=parity):
            for cp in copies(b + 1, 1 - parity):
                cp.start()

    for parity in range(2):
        @pl.when(slot == parity)
        def _(parity=parity):
            for cp in copies(b, parity):
                cp.wait()

    for s in range(n_kv_streams):
        full_ref[slot, s, pl.ds(past, ts), :] = newkv_ref[pl.ds(s, ts, stride=n_kv_streams), :]
        full_ref[slot, s, pl.ds(past + ts, ktot - past - ts), :] = jnp.zeros((ktot - past - ts, HEAD_DIM), F32)
    for s in range(n_win_streams):
        kwin_ref[s, pl.ds(0, wbuf), :] = cw_ref[pl.ds(s, wbuf, stride=n_win_streams), :]
        kwin_ref[s, pl.ds(wbuf, ts), :] = neww_ref[pl.ds(s, ts, stride=n_win_streams), :]
        kwin_ref[s, pl.ds(wbuf + ts, LANES - ts), :] = jnp.zeros((LANES - ts, HEAD_DIM), F32)
    keep = (wbuf - ts) * n_win_streams
    win_ref[pl.ds(0, keep), :] = cw_ref[pl.ds(ts * n_win_streams, keep), :]
    win_ref[pl.ds(keep, ts * n_win_streams), :] = neww_ref[...]

    tok = past + lax.broadcasted_iota(jnp.int32, (T_PAD, 1), 0)
    qpos_rows = rep(tok)
    kpos = lax.broadcasted_iota(jnp.int32, (1, ktot), 1)
    widx = lax.broadcasted_iota(jnp.int32, (1, wbuf + LANES), 1)
    wpos = past - wbuf + widx
    wdist = tok - wpos
    wbias = rep(_bias((wdist >= 0) & (wdist < WINDOW) & (wpos >= 0) & (widx < wbuf + ts)))

    groups = range(N_KV)
    cmp_kv = [[_compress_block(
        (lambda r, s=br * N_KV + g: full_ref[slot, s, pl.ds(r, nsp, stride=CMP_STRIDE), :]),
        pe_term_ref[br][0:1, :], w1_ref[br], w2_ref[br], nsp) for br in range(2)] for g in groups]
    qs = [(q_ref[g] * scale).astype(BF16) for g in groups]
    cmp_out = [_compressed_branch(qs[g], cmp_kv[g][0], cmp_kv[g][1], qpos_rows, nc, gqa) for g in groups]
    sels = [_select_blocks_lanes(cmp_out[g][1], tok, nb) for g in groups]
    sbias = [rep(_bias((_dot(sels[g], exp_ref[...]) > 0.5) & (kpos <= tok))) for g in groups]
    k_slc = lambda g: full_ref[slot, n_cmp_streams + g].astype(BF16)
    v_slc = lambda g: full_ref[slot, n_cmp_streams + N_KV + g].astype(BF16)
    slc = [_softmax_parts(_dot_nt(qs[g], k_slc(g)) + sbias[g]) for g in groups]
    o_slc = [_dot(slc[g][0].astype(BF16), v_slc(g)) / slc[g][1] for g in groups]
    win = [_softmax_parts(_dot_nt(qs[g], kwin_ref[g].astype(BF16)) + wbias) for g in groups]
    o_win = [_dot(win[g][0].astype(BF16), kwin_ref[N_KV + g].astype(BF16)) / win[g][1] for g in groups]
    ys = []
    for g in groups:
        gcol = gcol_ref[g]
        o = gcol[:, 0:1] * cmp_out[g][0] + gcol[:, 1:2] * o_slc[g] + gcol[:, 2:3] * o_win[g]
        ys.append(o * _silu(z_ref[g]))

    ss = None
    for y in ys:
        rs = jnp.sum(y * y, axis=-1, keepdims=True)
        for r in range(gqa):
            part = rs[r * T_PAD:(r + 1) * T_PAD]
            ss = part if ss is None else ss + part
    inv = rep(lax.rsqrt(ss / att_w + EPS))
    for g in range(N_KV):
        o_ref[g] = (ys[g] * inv) * ggrp_ref[g]


def _attention_sample(layer, page_table, cache_row, win_rows, newkv_rows, neww_rows, q_r, gcol_r, z_r,
                      ggrp_r, pe, w1, w2, expand, ts, att_w, win_prev):
    db = q_r.shape[0]
    n_pages = page_table.shape[1]
    n_kv_streams, n_win_streams = 4 * N_KV, 2 * N_KV
    page = cache_row.shape[2]
    past = n_pages * page
    wbuf = win_rows.shape[2] // n_win_streams
    gqa = att_w // HEAD_DIM // N_KV
    nsub = -(-(past + ts) // CMP_STRIDE)
    nc = nsub - CMP_LEN // CMP_STRIDE + 1
    nsp = -(-nsub // 8) * 8
    nb = max(-(-(past + ts) // SEL_BLOCK), N_SELECT)
    rows = gqa * T_PAD
    kern = functools.partial(_attn_sample_kernel, layer=layer, n_pages=n_pages, page=page, ts=ts, wbuf=wbuf,
                             nsp=nsp, nc=nc, nb=nb, gqa=gqa, att_w=att_w, has_prev=win_prev is not None)
    per_b = lambda shape: pl.BlockSpec((None,) + shape, lambda b, pt: (b,) + (0,) * len(shape))
    const = lambda shape: pl.BlockSpec(shape, lambda b, pt: (0,) * len(shape))
    in_specs = [
        per_b((N_KV, rows, HEAD_DIM)),
        per_b((N_KV, rows, HEAD_DIM)),
        per_b((N_KV, rows, HEAD_DIM)),
        const((N_KV, rows, HEAD_DIM)),
        per_b((ts * n_kv_streams, HEAD_DIM)),
        per_b((ts * n_win_streams, HEAD_DIM)),
        pl.BlockSpec((None, None, wbuf * n_win_streams, HEAD_DIM), lambda b, pt: (layer, b, 0, 0)),
        const((2, CMP_LEN, HEAD_DIM)),
        const((2, CMP_STRIDE * HEAD_DIM, 2 * HEAD_DIM)),
        const((2, HEAD_DIM, HEAD_DIM)),
        const((LANES, nsp * CMP_STRIDE)),
        pl.BlockSpec(memory_space=pl.ANY),
    ]
    args = [page_table.reshape(-1), q_r, gcol_r, z_r, ggrp_r, newkv_rows, neww_rows, win_rows, pe, w1, w2,
            expand, cache_row]
    aliases = {}
    if win_prev is not None:
        in_specs.append(pl.BlockSpec(memory_space=pl.ANY))
        aliases = {len(args): 1}
        args.append(win_prev)
    depth = win_rows.shape[0]
    grid_spec = pltpu.PrefetchScalarGridSpec(
        num_scalar_prefetch=1,
        grid=(db,),
        in_specs=in_specs,
        out_specs=[per_b((N_KV, rows, HEAD_DIM)),
                   pl.BlockSpec((None, None, wbuf * n_win_streams, HEAD_DIM), lambda b, pt: (layer, b, 0, 0))],
        scratch_shapes=[pltpu.VMEM((2, n_kv_streams, nsp * CMP_STRIDE, HEAD_DIM), F32),
                        pltpu.VMEM((n_win_streams, wbuf + LANES, HEAD_DIM), F32),
                        pltpu.VMEM((2, 8, HEAD_DIM), F32),
                        pltpu.SemaphoreType.DMA((2,))],
    )
    return pl.pallas_call(
        kern,
        grid_spec=grid_spec,
        out_shape=[jax.ShapeDtypeStruct((db, N_KV, rows, HEAD_DIM), F32),
                   jax.ShapeDtypeStruct((depth, db, wbuf * n_win_streams, HEAD_DIM), F32)],
        input_output_aliases=aliases,
        compiler_params=_cparams(("arbitrary",)),
        name="attn_sample",
    )(*args)


def _merge_kernel(ya_ref, yb_ref, x_ref, gate_ref, w_ref, g_ref, o_ref, *, conv_ch):
    y = (_dot(ya_ref[...].astype(BF16), w_ref[:conv_ch, :])
         + _dot(yb_ref[...].astype(BF16), w_ref[conv_ch:, :]))
    yn = y * lax.rsqrt(jnp.mean(y * y, axis=-1, keepdims=True) + EPS)
    o_ref[...] = x_ref[...] + gate_ref[...] * (yn * g_ref[...])


def _merge(ya, yb, x2d, gate, w_out, g_post, tm):
    r, d = x2d.shape
    conv_ch = ya.shape[1]
    tiles_per_gate = (r // tm) // gate.shape[0]
    s_rows = gate.shape[1]
    return pl.pallas_call(
        functools.partial(_merge_kernel, conv_ch=conv_ch),
        grid=(r // tm,),
        in_specs=[
            pl.BlockSpec((tm, conv_ch), lambda i: (i, 0)),
            pl.BlockSpec((tm, d - conv_ch), lambda i: (i, 0)),
            pl.BlockSpec((tm, d), lambda i: (i, 0)),
            pl.BlockSpec((None, s_rows, d), lambda i: (i // tiles_per_gate, 0, 0)),
            pl.BlockSpec((d, d), lambda i: (0, 0)),
            pl.BlockSpec((1, d), lambda i: (0, 0)),
        ],
        out_specs=pl.BlockSpec((tm, d), lambda i: (i, 0)),
        out_shape=jax.ShapeDtypeStruct((r, d), F32),
        compiler_params=_cparams(("parallel",)),
        name="merge",
    )(ya, yb, x2d, gate, w_out, g_post)


def _rope_tables(pos):
    half = HEAD_DIM // 2
    inv = jnp.power(ROPE_THETA, -jnp.arange(half, dtype=F32) / half)
    ang = pos.astype(F32)[:, None] * inv
    cos, sin = jnp.cos(ang), jnp.sin(ang)
    return jnp.concatenate([cos, cos], axis=-1), jnp.concatenate([-sin, sin], axis=-1)


def _expand_matrix(n_keys):
    blk = jnp.arange(n_keys, dtype=jnp.int32) // SEL_BLOCK
    return (blk[None, :] == jnp.arange(LANES, dtype=jnp.int32)[:, None]).astype(BF16)


def kernel(x_prompt, x_sample, cache_kv, cache_win, state_conv, page_table, c_prompt, c_sample,
           w_ada, b_ada, g_pre, w_in, conv_w, cmp_pe, cmp_w1, cmp_w2, g_grp, w_out, g_post):
    bp, tp, d = x_prompt.shape
    bs, ts, _ = x_sample.shape
    depth = w_in.shape[0]
    conv_ch = conv_w.shape[2]
    att_w = d - conv_ch
    n_heads = att_w // HEAD_DIM
    gqa = n_heads // N_KV
    n_pages, page = page_table.shape[1], cache_kv.shape[2]
    past = n_pages * page
    wbuf = cache_win.shape[2]
    assert tp % TQ == 0 and tp % KC_SEL == 0 and ts <= T_PAD
    assert att_w == 4 * KV_W and 3 * n_heads <= LANES

    c_q = 4 * conv_ch
    c_kv = c_q + att_w
    c_zb = c_kv + 4 * KV_W
    c_kvw = c_zb + att_w
    cols = {"q": c_q, "kv": c_kv, "kvs": c_kv + 2 * KV_W, "zb": c_zb, "kvw": c_kvw}
    o_kw = c_kv + 4 * KV_W
    o_g = o_kw + 2 * KV_W
    o_zb = o_g + 3 * n_heads
    w_main = jnp.concatenate([w_in[:, :, :o_kw], w_in[:, :, o_zb:], w_in[:, :, o_kw:o_g]], axis=-1).astype(BF16)
    w_gate = jnp.pad(w_in[:, :, o_g:o_zb], ((0, 0), (0, 0), (0, LANES - 3 * n_heads))).astype(BF16)
    w_out_b = w_out.astype(BF16)
    half = CMP_STRIDE * HEAD_DIM
    w1_b = jnp.concatenate([cmp_w1[:, :, :half], cmp_w1[:, :, half:]], axis=-1).astype(BF16)
    w2_b = cmp_w2.astype(BF16)

    mod = _modulation(jnp.concatenate([c_prompt, c_sample], axis=0), w_ada, b_ada)

    cos_p, sin_p = _rope_tables(jnp.arange(tp))
    pos_s = past + jnp.arange(ts)
    cos_s, sin_s = (jnp.tile(a, (bs, 1)) for a in _rope_tables(pos_s))
    expand_p = _expand_matrix(tp)
    nsub_s = -(-(past + ts) // CMP_STRIDE)
    nsp_s = -(-nsub_s // 8) * 8
    expand_s = _expand_matrix(nsp_s * CMP_STRIDE)

    tm_p = min(tp, TM_PROJ)
    rows_s = bs * ts
    cache_row = cache_kv.reshape(depth, cache_kv.shape[1], page, 4 * N_KV, HEAD_DIM)
    win_rows = cache_win.reshape(depth, bs, wbuf * 2 * N_KV, HEAD_DIM)

    def to_heads(a):
        a = a.reshape(bs, ts, N_KV, gqa, HEAD_DIM).transpose(0, 2, 3, 1, 4)
        a = jnp.pad(a, ((0, 0), (0, 0), (0, 0), (0, T_PAD - ts), (0, 0)))
        return a.reshape(bs, N_KV, gqa * T_PAD, HEAD_DIM)

    xp = x_prompt.reshape(bp * tp, d)
    xs = x_sample.reshape(rows_s, d)
    win_keep = min(WINDOW, tp)
    rows_p = rows_sm = win_s = None
    conv_p, conv_s = [], []
    for l in range(depth):
        shift, scale, gate = mod[l, :, :d], mod[l, :, d:2 * d], mod[l, :, 2 * d:]
        g_a, g_b = g_grp[l, None, :conv_ch], g_grp[l, None, conv_ch:]

        u, gates, *rows_p = _in_projection(xp, scale[:bp, None], shift[:bp, None], g_pre[l, None], cos_p, sin_p,
                                           w_main[l], w_gate[l], tm_p, conv_ch, att_w, l, depth, bp, win_keep,
                                           rows_p)
        u3 = u.reshape(bp, tp, -1)
        ya, cbuf = _short_conv(u3, jnp.zeros((bp, CONV_WIDTH - 1, conv_ch), F32), conv_w[l], g_a,
                               min(tp, TR_CONV), BF16)
        kcmp = _compress_prompt(u3, c_kv, cmp_pe[l], w1_b[l], w2_b[l])
        yb = _attention_prompt(u3, gates.reshape(bp, tp, LANES), kcmp, g_b, expand_p, cols, att_w)
        xp = _merge(ya.reshape(bp * tp, conv_ch), yb.reshape(bp * tp, att_w), xp, gate[:bp, None],
                    w_out_b[l], g_post[l, None], min(tp, TM_MERGE))
        conv_p.append(cbuf)

        rep = lambda a: jnp.repeat(a[bp:], ts, axis=0)[None]
        u, gates, *rows_sm = _in_projection(xs, rep(scale), rep(shift), g_pre[l, None], cos_s, sin_s,
                                            w_main[l], w_gate[l], rows_s, conv_ch, att_w, l, depth, 1, rows_s,
                                            rows_sm)
        ya, cbuf = _short_conv_step(u, state_conv[l], conv_w[l], g_a, ts)
        q_r = to_heads(u[:, c_q:c_q + att_w])
        z_r = to_heads(u[:, c_zb:c_zb + att_w])
        gcol = gates[:, :3 * n_heads].reshape(rows_s, n_heads, 3)
        gcol_r = to_heads(jnp.pad(gcol, ((0, 0), (0, 0), (0, HEAD_DIM - 3))).reshape(rows_s, att_w))
        ggrp_r = jnp.broadcast_to(g_b.reshape(N_KV, gqa, 1, HEAD_DIM),
                                  (N_KV, gqa, T_PAD, HEAD_DIM)).reshape(N_KV, gqa * T_PAD, HEAD_DIM)
        newkv_rows = rows_sm[0][l].reshape(bs, ts * 4 * N_KV, HEAD_DIM)
        neww_rows = rows_sm[1][l].reshape(bs, ts * 2 * N_KV, HEAD_DIM)
        yb_r, win_s = _attention_sample(l, page_table, cache_row, win_rows, newkv_rows, neww_rows, q_r, gcol_r,
                                        z_r, ggrp_r, cmp_pe[l], w1_b[l], w2_b[l], expand_s, ts, att_w, win_s)
        yb = yb_r.reshape(bs, N_KV, gqa, T_PAD, HEAD_DIM)[:, :, :, :ts].transpose(0, 3, 1, 2, 4)
        xs = _merge(ya, yb.reshape(rows_s, att_w), xs, rep(gate), w_out_b[l], g_post[l, None], rows_s)
        conv_s.append(cbuf)

    kv_shape = lambda b, t: (depth, b, t, 4, N_KV, HEAD_DIM)
    win_shape = lambda b, t: (depth, b, t, 2, N_KV, HEAD_DIM)
    return (xp.reshape(bp, tp, d), xs.reshape(bs, ts, d),
            rows_p[0].reshape(kv_shape(bp, tp)), rows_sm[0].reshape(kv_shape(bs, ts)),
            rows_p[1].reshape(win_shape(bp, win_keep)), win_s.reshape(win_shape(bs, wbuf)),
            jnp.stack(conv_p), jnp.stack(conv_s))
```

```python
import functools

import jax
import jax.numpy as jnp
from jax import lax
from jax.experimental import pallas as pl
from jax.experimental.pallas import tpu as pltpu

F32 = jnp.float32
BF16 = jnp.bfloat16

HEAD_DIM = 128
N_KV = 2
CONV_WIDTH = 3
CMP_LEN = 32
CMP_STRIDE = 16
SEL_BLOCK = 64
N_SELECT = 8
WINDOW = 512
ROPE_THETA = 10000.0
EPS = 1e-6
NEG = -1e30
FORCE_SCORE = 1e3
KV_W = N_KV * HEAD_DIM
LANES = 128
VMEM_LIMIT = 52 * 1024 * 1024

TM_PROJ = 1024
TN_MOD = 768
TR_CONV = 512
TM_MERGE = 512
TQ = 128
KC_SEL = 512
T_PAD = 8


def _cparams(sem):
    return pltpu.CompilerParams(dimension_semantics=sem, vmem_limit_bytes=VMEM_LIMIT)


def _silu(x):
    return x / (1.0 + jnp.exp(-x))


def _sigmoid(x):
    return 1.0 / (1.0 + jnp.exp(-x))


def _dot(a, b):
    return jnp.dot(a, b, preferred_element_type=F32)


def _dot_nt(a, b):
    return lax.dot_general(a, b, (((1,), (1,)), ((), ())), preferred_element_type=F32)


def _mod_kernel(c_ref, w_ref, b_ref, o_ref):
    a = _silu(c_ref[...]).astype(BF16)
    o_ref[...] = _dot(a, w_ref[...].astype(BF16)) + b_ref[...]


def _modulation(c_all, w_ada, b_ada):
    depth, d, n = w_ada.shape
    rows = c_all.shape[0]
    tn = TN_MOD
    return pl.pallas_call(
        _mod_kernel,
        grid=(depth, n // tn),
        in_specs=[
            pl.BlockSpec((rows, d), lambda l, j: (0, 0)),
            pl.BlockSpec((None, d, tn), lambda l, j: (l, 0, j)),
            pl.BlockSpec((None, 1, tn), lambda l, j: (l, 0, j)),
        ],
        out_specs=pl.BlockSpec((None, rows, tn), lambda l, j: (l, 0, j)),
        out_shape=jax.ShapeDtypeStruct((depth, rows, n), F32),
        compiler_params=_cparams(("parallel", "parallel")),
        name="mod",
    )(c_all, w_ada, b_ada.reshape(depth, 1, n))


def _rope_cols(v, cos, sin):
    outs = []
    for h in range(v.shape[1] // HEAD_DIM):
        xh = v[:, h * HEAD_DIM:(h + 1) * HEAD_DIM]
        outs.append(xh * cos + pltpu.roll(xh, HEAD_DIM // 2, 1) * sin)
    return outs[0] if len(outs) == 1 else jnp.concatenate(outs, axis=1)


def _inproj_kernel(*refs, full_lo, full_hi, kv_lo, win_tile, tiles_per_seq, win_rows, has_prev):
    (x_ref, scale_ref, shift_ref, g_ref, cos_ref, sin_ref, w_ref, wg_ref) = refs[:8]
    u_ref, gates_ref, kv_ref, win_ref, h_ref = refs[8 + (2 if has_prev else 0):]
    i = pl.program_id(0)
    j = pl.program_id(1)
    tm = x_ref.shape[0]
    n_kv_streams, n_win_streams = 4 * N_KV, 2 * N_KV

    @pl.when(j == 0)
    def _():
        x = x_ref[...]
        y = x * lax.rsqrt(jnp.mean(x * x, axis=-1, keepdims=True) + EPS)
        h = (y * g_ref[...]) * (1.0 + scale_ref[...]) + shift_ref[...]
        hb = h.astype(BF16)
        h_ref[...] = hb
        gates_ref[...] = _sigmoid(_dot(hb, wg_ref[...]))

    u_ref[...] = _dot(h_ref[...], w_ref[...])

    @pl.when((j >= full_lo) & (j < full_hi))
    def _():
        u_ref[...] = _rope_cols(u_ref[...], cos_ref[...], sin_ref[...])

    def rope_first_half():
        u_ref[:, :KV_W] = _rope_cols(u_ref[:, :KV_W], cos_ref[...], sin_ref[...])

    for k in range(2):
        @pl.when(j == kv_lo + k)
        def _(k=k):
            rope_first_half()
            for q in range(n_kv_streams // 2):
                kv_ref[pl.ds(k * (n_kv_streams // 2) + q, tm, stride=n_kv_streams), :] = (
                    u_ref[:, q * HEAD_DIM:(q + 1) * HEAD_DIM])

    @pl.when(j == win_tile)
    def _():
        rope_first_half()

        @pl.when(i % tiles_per_seq == tiles_per_seq - 1)
        def _():
            for q in range(n_win_streams):
                win_ref[pl.ds(q, win_rows, stride=n_win_streams), :] = (
                    u_ref[pl.ds(tm - win_rows, win_rows), q * HEAD_DIM:(q + 1) * HEAD_DIM])


def _in_projection(x2d, scale, shift, g_pre, cos, sin, w_main, w_gate, tm, conv_ch, att_w, layer, depth,
                   n_seq, win_rows, prev):
    r, d = x2d.shape
    nm = w_main.shape[1]
    tn = 2 * KV_W
    n_tiles = r // tm
    tiles_per_scale = n_tiles // scale.shape[0]
    tiles_per_seq = n_tiles // n_seq
    s_rows = scale.shape[1]
    pos_tiles = cos.shape[0] // tm
    q0 = 4 * conv_ch // tn
    q1 = q0 + att_w // tn
    assert win_rows <= tm
    kern = functools.partial(_inproj_kernel, full_lo=q0, full_hi=q1, kv_lo=q1, win_tile=q1 + 2 + att_w // tn,
                             tiles_per_seq=tiles_per_seq, win_rows=win_rows, has_prev=prev is not None)
    n_kv_streams, n_win_streams = 4 * N_KV, 2 * N_KV
    in_specs = [
        pl.BlockSpec((tm, d), lambda i, j: (i, 0)),
        pl.BlockSpec((None, s_rows, d), lambda i, j: (i // tiles_per_scale, 0, 0)),
        pl.BlockSpec((None, s_rows, d), lambda i, j: (i // tiles_per_scale, 0, 0)),
        pl.BlockSpec((1, d), lambda i, j: (0, 0)),
        pl.BlockSpec((tm, HEAD_DIM), lambda i, j: (i % pos_tiles, 0)),
        pl.BlockSpec((tm, HEAD_DIM), lambda i, j: (i % pos_tiles, 0)),
        pl.BlockSpec((d, tn), lambda i, j: (0, j)),
        pl.BlockSpec((d, LANES), lambda i, j: (0, 0)),
    ]
    args = [x2d, scale, shift, g_pre, cos, sin, w_main, w_gate]
    aliases = {}
    if prev is not None:
        in_specs += [pl.BlockSpec(memory_space=pl.ANY), pl.BlockSpec(memory_space=pl.ANY)]
        args += list(prev)
        aliases = {8: 2, 9: 3}
    return pl.pallas_call(
        kern,
        grid=(n_tiles, nm // tn),
        in_specs=in_specs,
        out_specs=[
            pl.BlockSpec((tm, tn), lambda i, j: (i, j)),
            pl.BlockSpec((tm, LANES), lambda i, j: (i, 0)),
            pl.BlockSpec((None, tm * n_kv_streams, HEAD_DIM), lambda i, j: (layer, i, 0)),
            pl.BlockSpec((None, None, win_rows * n_win_streams, HEAD_DIM),
                         lambda i, j: (layer, i // tiles_per_seq, 0, 0)),
        ],
        out_shape=[jax.ShapeDtypeStruct((r, nm), F32), jax.ShapeDtypeStruct((r, LANES), F32),
                   jax.ShapeDtypeStruct((depth, r * n_kv_streams, HEAD_DIM), F32),
                   jax.ShapeDtypeStruct((depth, n_seq, win_rows * n_win_streams, HEAD_DIM), F32)],
        scratch_shapes=[pltpu.VMEM((tm, d), BF16)],
        input_output_aliases=aliases,
        compiler_params=_cparams(("arbitrary", "arbitrary")),
        name="inproj",
    )(*args)


def _conv_kernel(b_ref, c_ref, x_ref, z_ref, init_ref, w_ref, g_ref, ya_ref, st_ref, up_ref, *, tr):
    i = pl.program_id(1)
    pad = 8

    @pl.when(i == 0)
    def _():
        up_ref[pl.ds(pad - 2, 2), :] = init_ref[...]

    up_ref[pl.ds(pad, tr), :] = c_ref[...] * x_ref[...]
    w = w_ref[...]
    y = (w[0:1, :] * up_ref[pl.ds(pad - 2, tr), :]
         + w[1:2, :] * up_ref[pl.ds(pad - 1, tr), :]
         + w[2:3, :] * up_ref[pl.ds(pad, tr), :])
    ya = _silu(z_ref[...]) * (b_ref[...] * y)
    yn = ya * lax.rsqrt(jnp.mean(ya * ya, axis=-1, keepdims=True) + EPS)
    ya_ref[...] = (yn * g_ref[...]).astype(ya_ref.dtype)
    last = up_ref[pl.ds(pad + tr - 2, 2), :]
    st_ref[...] = last
    up_ref[pl.ds(pad - 2, 2), :] = last


def _short_conv(u3, init_state, conv_w, g_a, tr, out_dtype):
    b, t, _ = u3.shape
    c = conv_w.shape[1]
    col = lambda k: pl.BlockSpec((None, tr, c), lambda bi, i, k=k: (bi, i, k))
    return pl.pallas_call(
        functools.partial(_conv_kernel, tr=tr),
        grid=(b, t // tr),
        in_specs=[
            col(0), col(1), col(2), col(3),
            pl.BlockSpec((None, CONV_WIDTH - 1, c), lambda bi, i: (bi, 0, 0)),
            pl.BlockSpec((CONV_WIDTH, c), lambda bi, i: (0, 0)),
            pl.BlockSpec((1, c), lambda bi, i: (0, 0)),
        ],
        out_specs=[
            pl.BlockSpec((None, tr, c), lambda bi, i: (bi, i, 0)),
            pl.BlockSpec((None, CONV_WIDTH - 1, c), lambda bi, i: (bi, 0, 0)),
        ],
        out_shape=[jax.ShapeDtypeStruct((b, t, c), out_dtype),
                   jax.ShapeDtypeStruct((b, CONV_WIDTH - 1, c), F32)],
        scratch_shapes=[pltpu.VMEM((tr + 8, c), F32)],
        compiler_params=_cparams(("parallel", "arbitrary")),
        name="conv",
    )(u3, u3, u3, u3, init_state, conv_w, g_a)


def _conv_step_kernel(b_ref, c_ref, x_ref, z_ref, h1_ref, h2_ref, w_ref, g_ref, ya_ref, uc_ref, *, ts):
    uc = c_ref[...] * x_ref[...]
    t = lax.broadcasted_iota(jnp.int32, uc.shape, 0) % ts
    back1 = jnp.where(t < 1, h1_ref[...], pltpu.roll(uc, 1, 0))
    back2 = jnp.where(t < 2, h2_ref[...], pltpu.roll(uc, 2, 0))
    w = w_ref[...]
    y = w[0:1, :] * back2 + w[1:2, :] * back1 + w[2:3, :] * uc
    ya = _silu(z_ref[...]) * (b_ref[...] * y)
    yn = ya * lax.rsqrt(jnp.mean(ya * ya, axis=-1, keepdims=True) + EPS)
    ya_ref[...] = yn * g_ref[...]
    uc_ref[...] = uc


def _short_conv_step(u2d, init_state, conv_w, g_a, ts):
    r = u2d.shape[0]
    nseq, _, c = init_state.shape
    assert ts >= CONV_WIDTH - 1
    zero = jnp.zeros((nseq, ts - 1, c), F32)
    h1 = jnp.concatenate([init_state[:, 1:2], zero], axis=1).reshape(r, c)
    h2 = jnp.concatenate([init_state, zero[:, 1:]], axis=1).reshape(r, c)
    col = lambda k: pl.BlockSpec((r, c), lambda i, k=k: (0, k))
    full = lambda shape: pl.BlockSpec(shape, lambda i: (0,) * len(shape))
    ya, uc = pl.pallas_call(
        functools.partial(_conv_step_kernel, ts=ts),
        grid=(1,),
        in_specs=[col(0), col(1), col(2), col(3), full((r, c)), full((r, c)), full(conv_w.shape), full(g_a.shape)],
        out_specs=[full((r, c)), full((r, c))],
        out_shape=[jax.ShapeDtypeStruct((r, c), F32), jax.ShapeDtypeStruct((r, c), F32)],
        compiler_params=_cparams(("arbitrary",)),
        name="conv_step",
    )(u2d, u2d, u2d, u2d, h1, h2, conv_w, g_a)
    return ya, uc.reshape(nseq, ts, c)[:, ts - (CONV_WIDTH - 1):]


def _pe_term(pe, w1cat):
    flat = lambda lo: jnp.concatenate([pe[lo + r:lo + r + 1, :] for r in range(CMP_STRIDE)], axis=1)
    rows = jnp.concatenate([flat(0), flat(CMP_STRIDE), jnp.zeros((6, CMP_STRIDE * HEAD_DIM), F32)], axis=0)
    r = _dot(rows.astype(BF16), w1cat)
    return r[0:1, :HEAD_DIM] + r[1:2, HEAD_DIM:]


def _compress_block(load_sub, pe_term, w1cat, w2, nsp):
    a = jnp.concatenate([load_sub(r).astype(BF16) for r in range(CMP_STRIDE)], axis=1)
    pq = _dot(a, w1cat)
    pre = pq[:, :HEAD_DIM] + pltpu.roll(pq[:, HEAD_DIM:], nsp - 1, 0) + pe_term
    return _dot(_silu(pre).astype(BF16), w2)


def _compress_kernel(k_ref, pe_ref, w1_ref, w2_ref, o_ref, *, nsp):
    load_sub = lambda r: k_ref[pl.ds(r, nsp, stride=CMP_STRIDE), :]
    w1cat = w1_ref[...]
    o_ref[...] = _compress_block(load_sub, _pe_term(pe_ref[...], w1cat), w1cat, w2_ref[...], nsp)


def _compress_prompt(u3, kv_col0, cmp_pe, cmp_w1, cmp_w2):
    b, t, _ = u3.shape
    nsp = t // CMP_STRIDE
    blk0 = kv_col0 // HEAD_DIM
    return pl.pallas_call(
        functools.partial(_compress_kernel, nsp=nsp),
        grid=(b, 2 * N_KV),
        in_specs=[
            pl.BlockSpec((None, t, HEAD_DIM), lambda bi, s: (bi, 0, blk0 + s)),
            pl.BlockSpec((None, CMP_LEN, HEAD_DIM), lambda bi, s: (s // N_KV, 0, 0)),
            pl.BlockSpec((None, CMP_STRIDE * HEAD_DIM, 2 * HEAD_DIM), lambda bi, s: (s // N_KV, 0, 0)),
            pl.BlockSpec((None, HEAD_DIM, HEAD_DIM), lambda bi, s: (s // N_KV, 0, 0)),
        ],
        out_specs=pl.BlockSpec((None, None, nsp, HEAD_DIM), lambda bi, s: (bi, s, 0, 0)),
        out_shape=jax.ShapeDtypeStruct((b, 2 * N_KV, nsp, HEAD_DIM), F32),
        compiler_params=_cparams(("parallel", "parallel")),
        name="compress",
    )(u3, cmp_pe, cmp_w1, cmp_w2)


def _bias(mask):
    return jnp.where(mask, 0.0, NEG)


def _row_max(s):
    return jnp.maximum(jnp.max(s, axis=-1, keepdims=True), 0.1 * NEG)


def _softmax_parts(s):
    e = jnp.exp(s - _row_max(s))
    return e, jnp.maximum(jnp.sum(e, axis=-1, keepdims=True), 1e-30)


def _weighted_values(e, v):
    r = _dot(e.astype(BF16), jnp.concatenate([v, jnp.ones_like(v)], axis=1))
    return r[:, :HEAD_DIM], jnp.maximum(r[:, HEAD_DIM:], 1e-30)


def _fold_lanes(x, op):
    out = x[:, :LANES]
    for c in range(1, x.shape[1] // LANES):
        out = op(out, x[:, c * LANES:(c + 1) * LANES])
    return out


def _compressed_branch(qg, kc, vc, qpos_rows, nc, n_rep):
    ncp = min(kc.shape[0], -(-nc // LANES) * LANES)
    assert ncp >= nc
    kc, vc = kc[:ncp], vc[:ncp]
    cidx = lax.broadcasted_iota(jnp.int32, (1, ncp), 1)
    mask = ((cidx * CMP_STRIDE + (CMP_LEN - 1)) <= qpos_rows) & (cidx < nc)
    e, l = _softmax_parts(_dot_nt(qg, kc.astype(BF16)) + _bias(mask))
    p = e / l
    o = _dot(p.astype(BF16), vc.astype(BF16))
    rt = qg.shape[0] // n_rep
    p_tok = p[0:rt]
    for r in range(1, n_rep):
        p_tok = p_tok + p[r * rt:(r + 1) * rt]
    return o, p_tok


def _pool_matrix(rows, cols, blocks_on_rows):
    per = SEL_BLOCK // CMP_STRIDE
    r = lax.broadcasted_iota(jnp.int32, (rows, cols), 0)
    c = lax.broadcasted_iota(jnp.int32, (rows, cols), 1)
    hit = (c // per == r) if blocks_on_rows else (r // per == c)
    return jnp.where(hit, 1.0, 0.0).astype(F32)


def _top_blocks(score, blk, nb, axis):
    rank = jnp.zeros(score.shape, jnp.int32)
    for j in range(nb):
        one = score[j:j + 1, :] if axis == 0 else score[:, j:j + 1]
        beats = (one > score) | ((one == score) & (blk > j))
        rank = rank + jnp.where(beats, 1, 0)
    return jnp.where((rank < N_SELECT) & (score > 0.5 * NEG), 1.0, 0.0)


def _block_scores(imp, blk, cur):
    forced = (blk == 0) | (blk == cur) | (blk == cur - 1)
    return jnp.where(blk <= cur, jnp.where(forced, FORCE_SCORE, imp), NEG)


def _select_blocks_rows(p_tok, tok_row, nb):
    rt, ncp = p_tok.shape
    nbr = -(-nb // 8) * 8
    imp_t = lax.dot_general(_pool_matrix(nbr, ncp, True), p_tok, (((1,), (1,)), ((), ())),
                            preferred_element_type=F32, precision=lax.Precision.HIGHEST)
    blk = lax.broadcasted_iota(jnp.int32, (nbr, rt), 0)
    sel_t = _top_blocks(_block_scores(imp_t, blk, tok_row // SEL_BLOCK), blk, nb, 0)
    if nbr < LANES:
        sel_t = jnp.concatenate([sel_t, jnp.zeros((LANES - nbr, rt), F32)], axis=0)
    return _bias(sel_t.T > 0.5).astype(BF16)


def _select_blocks_lanes(p_tok, tok_col, nb):
    rt, ncp = p_tok.shape
    imp = jnp.dot(p_tok, _pool_matrix(ncp, LANES, False), preferred_element_type=F32,
                  precision=lax.Precision.HIGHEST)
    blk = lax.broadcasted_iota(jnp.int32, (rt, LANES), 1)
    return _top_blocks(_block_scores(imp, blk, tok_col // SEL_BLOCK), blk, nb, 1).astype(BF16)


def _attn_prompt_kernel(q_ref, kvs_ref, kvw_ref, cmp_ref, gates_ref, z_ref, ggrp_ref, exp_ref, o_ref,
                        s_ref, *, nc, nb, gqa, att_w, t, wk):
    i = pl.program_id(1)
    s0 = i * TQ
    scale = HEAD_DIM ** -0.5
    tok = lax.broadcasted_iota(jnp.int32, (TQ, 1), 0) + s0
    tok_row = lax.broadcasted_iota(jnp.int32, (1, TQ), 1) + s0
    qpos_rows = jnp.concatenate([tok] * gqa, axis=0)
    rows = TQ * gqa
    rep = lambda a: jnp.concatenate([a] * gqa, axis=0)
    kcol = lambda g: slice(g * HEAD_DIM, (g + 1) * HEAD_DIM)
    vcol = lambda g: slice(KV_W + g * HEAD_DIM, KV_W + (g + 1) * HEAD_DIM)
    groups = range(N_KV)

    qs = [(jnp.concatenate(
        [q_ref[:, (g * gqa + r) * HEAD_DIM:(g * gqa + r + 1) * HEAD_DIM] for r in range(gqa)],
        axis=0) * scale).astype(BF16) for g in groups]

    o_cmp, sel = [], []
    for g in groups:
        o, p_tok = _compressed_branch(qs[g], cmp_ref[g], cmp_ref[N_KV + g], qpos_rows, nc, gqa)
        o_cmp.append(o)
        sel.append(_select_blocks_rows(p_tok, tok_row, nb))

    w_lo = pl.multiple_of(jnp.clip(s0 - WINDOW, 0, t - wk), TQ)
    wpos = w_lo + lax.broadcasted_iota(jnp.int32, (1, wk), 1)
    wdist = tok - wpos
    wbias = rep(_bias((wdist >= 0) & (wdist < WINDOW)))
    o_win = []
    for g in groups:
        kw = kvw_ref[pl.ds(w_lo, wk), kcol(g)].astype(BF16)
        vw = kvw_ref[pl.ds(w_lo, wk), vcol(g)].astype(BF16)
        sw = _dot_nt(qs[g], kw) + wbias
        num, den = _weighted_values(jnp.exp(sw - _row_max(sw)), vw)
        o_win.append(num / den)

    head_rows = lambda a, r: a[r * TQ:(r + 1) * TQ]
    gate = lambda h, k: gates_ref[:, 3 * h + k:3 * h + k + 1]
    partial, zact = [], []
    for g in groups:
        for r in range(gqa):
            h = g * gqa + r
            partial.append(gate(h, 0) * head_rows(o_cmp[g], r) + gate(h, 2) * head_rows(o_win[g], r))
            zact.append(_silu(z_ref[:, h * HEAD_DIM:(h + 1) * HEAD_DIM]))

    n_sel = (s0 + TQ + KC_SEL - 1) // KC_SEL

    qb = [jnp.concatenate([qs[g], rep(sel[g])], axis=1) for g in groups]

    def score_body(ci, mruns, causal):
        k0 = pl.multiple_of(ci * KC_SEL, KC_SEL)
        blk = exp_ref[pl.ds(k0, KC_SEL), :]
        out = []
        for g in groups:
            k = kvs_ref[pl.ds(k0, KC_SEL), kcol(g)].astype(BF16)
            s = _dot_nt(qb[g], jnp.concatenate([k, blk], axis=1))
            if causal:
                s = s + rep(_bias((k0 + lax.broadcasted_iota(jnp.int32, (1, KC_SEL), 1)) <= tok))
            s_ref[g, :, pl.ds(k0, KC_SEL)] = s
            out.append(jnp.maximum(mruns[g], _fold_lanes(s, jnp.maximum)))
        return tuple(out)

    mruns = lax.fori_loop(0, n_sel - 1, functools.partial(score_body, causal=False),
                          tuple(jnp.full((rows, LANES), NEG, F32) for _ in groups))
    mruns = score_body(n_sel - 1, mruns, True)
    ms = [_row_max(mruns[g]) for g in groups]

    def value_body(ci, carry):
        k0 = pl.multiple_of(ci * KC_SEL, KC_SEL)
        out = []
        for g in groups:
            v = kvs_ref[pl.ds(k0, KC_SEL), vcol(g)].astype(BF16)
            e = jnp.exp(s_ref[g, :, pl.ds(k0, KC_SEL)] - ms[g])
            out.append(carry[g] + _dot(e.astype(BF16), jnp.concatenate([v, jnp.ones_like(v)], axis=1)))
        return tuple(out)

    sums = lax.fori_loop(0, n_sel, value_body, tuple(jnp.zeros((rows, 2 * HEAD_DIM), F32) for _ in groups))

    heads = []
    for g in groups:
        o_slc = sums[g][:, :HEAD_DIM] / jnp.maximum(sums[g][:, HEAD_DIM:], 1e-30)
        for r in range(gqa):
            h = g * gqa + r
            heads.append((partial[h] + gate(h, 1) * head_rows(o_slc, r)) * zact[h])

    ss = jnp.sum(heads[0] * heads[0], axis=-1, keepdims=True)
    for y in heads[1:]:
        ss = ss + jnp.sum(y * y, axis=-1, keepdims=True)
    inv = lax.rsqrt(ss / att_w + EPS)
    o_ref[...] = jnp.concatenate(
        [(y * inv) * ggrp_ref[:, h * HEAD_DIM:(h + 1) * HEAD_DIM] for h, y in enumerate(heads)],
        axis=1).astype(o_ref.dtype)


def _attention_prompt(u3, gates3, kcmp, g_b, expand, cols, att_w):
    b, t, _ = u3.shape
    gqa = att_w // HEAD_DIM // N_KV
    nsub = t // CMP_STRIDE
    nc = nsub - CMP_LEN // CMP_STRIDE + 1
    nb = max(-(-t // SEL_BLOCK), N_SELECT)
    wk = min(t, WINDOW + TQ)
    kern = functools.partial(_attn_prompt_kernel, nc=nc, nb=nb, gqa=gqa, att_w=att_w, t=t, wk=wk)
    two_kv = 2 * KV_W
    return pl.pallas_call(
        kern,
        grid=(b, t // TQ),
        in_specs=[
            pl.BlockSpec((None, TQ, att_w), lambda bi, i: (bi, i, cols["q"] // att_w)),
            pl.BlockSpec((None, t, two_kv), lambda bi, i: (bi, 0, cols["kvs"] // two_kv)),
            pl.BlockSpec((None, t, two_kv), lambda bi, i: (bi, 0, cols["kvw"] // two_kv)),
            pl.BlockSpec((None, 2 * N_KV, kcmp.shape[2], HEAD_DIM), lambda bi, i: (bi, 0, 0, 0)),
            pl.BlockSpec((None, TQ, LANES), lambda bi, i: (bi, i, 0)),
            pl.BlockSpec((None, TQ, att_w), lambda bi, i: (bi, i, cols["zb"] // att_w)),
            pl.BlockSpec((1, att_w), lambda bi, i: (0, 0)),
            pl.BlockSpec((t, LANES), lambda bi, i: (0, 0)),
        ],
        out_specs=pl.BlockSpec((None, TQ, att_w), lambda bi, i: (bi, i, 0)),
        out_shape=jax.ShapeDtypeStruct((b, t, att_w), BF16),
        scratch_shapes=[pltpu.VMEM((N_KV, TQ * gqa, t), F32)],
        compiler_params=_cparams(("parallel", "arbitrary")),
        name="attn_prompt",
    )(u3, u3, u3, kcmp, gates3, u3, g_b, expand)


def _page_copies(pt_ref, cache_ref, full_ref, sem_ref, layer, b, slot, n_pages, page):
    copies = []
    for p in range(n_pages):
        pg = pt_ref[b * n_pages + p]
        for s in range(4 * N_KV):
            copies.append(pltpu.make_async_copy(
                cache_ref.at[layer, pg, :, s, :], full_ref.at[slot, s, pl.ds(p * page, page), :],
                sem_ref.at[slot]))
    return copies


def _attn_sample_kernel(pt_ref, q_ref, gcol_ref, z_ref, ggrp_ref, newkv_ref, neww_ref, cw_ref,
                        pe_ref, w1_ref, w2_ref, exp_ref, cache_ref, *rest,
                        layer, n_pages, page, ts, wbuf, nsp, nc, nb, gqa, att_w, has_prev):
    o_ref, win_ref, full_ref, kwin_ref, pe_term_ref, sem_ref = rest[(1 if has_prev else 0):]
    n_kv_streams, n_win_streams, n_cmp_streams = 4 * N_KV, 2 * N_KV, 2 * N_KV
    past = n_pages * page
    ktot = nsp * CMP_STRIDE
    scale = HEAD_DIM ** -0.5
    rep = lambda a: jnp.concatenate([a] * gqa, axis=0)
    b = pl.program_id(0)
    n_b = pl.num_programs(0)
    slot = b % 2
    copies = lambda bb, sl: _page_copies(pt_ref, cache_ref, full_ref, sem_ref, layer, bb, sl, n_pages, page)

    @pl.when(b == 0)
    def _():
        for br in range(2):
            pe_term_ref[br] = jnp.broadcast_to(_pe_term(pe_ref[br], w1_ref[br]), (8, HEAD_DIM))
        for cp in copies(0, 0):
            cp.start()

    for parity in range(2):
        @pl.when((slot == parity) & (b + 1 < n_b))
        def _(parity=parity):
            for cp in copies(b + 1, 1 - parity):
                cp.start()

    for parity in range(2):
        @pl.when(slot == parity)
        def _(parity=parity):
            for cp in copies(b, parity):
                cp.wait()

    for s in range(n_kv_streams):
        full_ref[slot, s, pl.ds(past, ts), :] = newkv_ref[pl.ds(s, ts, stride=n_kv_streams), :]
        full_ref[slot, s, pl.ds(past + ts, ktot - past - ts), :] = jnp.zeros((ktot - past - ts, HEAD_DIM), F32)
    for s in range(n_win_streams):
        kwin_ref[s, pl.ds(0, wbuf), :] = cw_ref[pl.ds(s, wbuf, stride=n_win_streams), :]
        kwin_ref[s, pl.ds(wbuf, ts), :] = neww_ref[pl.ds(s, ts, stride=n_win_streams), :]
        kwin_ref[s, pl.ds(wbuf + ts, LANES - ts), :] = jnp.zeros((LANES - ts, HEAD_DIM), F32)
    keep = (wbuf - ts) * n_win_streams
    win_ref[pl.ds(0, keep), :] = cw_ref[pl.ds(ts * n_win_streams, keep), :]
    win_ref[pl.ds(keep, ts * n_win_streams), :] = neww_ref[...]

    tok = past + lax.broadcasted_iota(jnp.int32, (T_PAD, 1), 0)
    qpos_rows = rep(tok)
    kpos = lax.broadcasted_iota(jnp.int32, (1, ktot), 1)
    widx = lax.broadcasted_iota(jnp.int32, (1, wbuf + LANES), 1)
    wpos = past - wbuf + widx
    wdist = tok - wpos
    wbias = rep(_bias((wdist >= 0) & (wdist < WINDOW) & (wpos >= 0) & (widx < wbuf + ts)))

    groups = range(N_KV)
    stacked = [_compress_block(
        (lambda r, br=br: jnp.concatenate(
            [full_ref[slot, br * N_KV + g, pl.ds(r, nsp, stride=CMP_STRIDE), :] for g in groups], axis=0)),
        pe_term_ref[br][0:1, :], w1_ref[br], w2_ref[br], N_KV * nsp) for br in range(2)]
    cmp_kv = [[stacked[br][g * nsp:(g + 1) * nsp] for br in range(2)] for g in groups]
    qs = [(q_ref[g] * scale).astype(BF16) for g in groups]
    cmp_out = [_compressed_branch(qs[g], cmp_kv[g][0], cmp_kv[g][1], qpos_rows, nc, gqa) for g in groups]
    sels = [_select_blocks_lanes(cmp_out[g][1], tok, nb) for g in groups]
    sbias = [rep(_bias((_dot(sels[g], exp_ref[...]) > 0.5) & (kpos <= tok))) for g in groups]
    k_slc = lambda g: full_ref[slot, n_cmp_streams + g].astype(BF16)
    v_slc = lambda g: full_ref[slot, n_cmp_streams + N_KV + g].astype(BF16)
    slc = [_softmax_parts(_dot_nt(qs[g], k_slc(g)) + sbias[g]) for g in groups]
    o_slc = [_dot(slc[g][0].astype(BF16), v_slc(g)) / slc[g][1] for g in groups]
    win = [_softmax_parts(_dot_nt(qs[g], kwin_ref[g].astype(BF16)) + wbias) for g in groups]
    o_win = [_dot(win[g][0].astype(BF16), kwin_ref[N_KV + g].astype(BF16)) / win[g][1] for g in groups]
    ys = []
    for g in groups:
        gcol = gcol_ref[g]
        o = gcol[:, 0:1] * cmp_out[g][0] + gcol[:, 1:2] * o_slc[g] + gcol[:, 2:3] * o_win[g]
        ys.append(o * _silu(z_ref[g]))

    ss = None
    for y in ys:
        rs = jnp.sum(y * y, axis=-1, keepdims=True)
        for r in range(gqa):
            part = rs[r * T_PAD:(r + 1) * T_PAD]
            ss = part if ss is None else ss + part
    inv = rep(lax.rsqrt(ss / att_w + EPS))
    for g in range(N_KV):
        o_ref[g] = (ys[g] * inv) * ggrp_ref[g]


def _attention_sample(layer, page_table, cache_row, win_rows, newkv_rows, neww_rows, q_r, gcol_r, z_r,
                      ggrp_r, pe, w1, w2, expand, ts, att_w, win_prev):
    db = q_r.shape[0]
    n_pages = page_table.shape[1]
    n_kv_streams, n_win_streams = 4 * N_KV, 2 * N_KV
    page = cache_row.shape[2]
    past = n_pages * page
    wbuf = win_rows.shape[2] // n_win_streams
    gqa = att_w // HEAD_DIM // N_KV
    nsub = -(-(past + ts) // CMP_STRIDE)
    nc = nsub - CMP_LEN // CMP_STRIDE + 1
    nsp = -(-nsub // 8) * 8
    nb = max(-(-(past + ts) // SEL_BLOCK), N_SELECT)
    rows = gqa * T_PAD
    kern = functools.partial(_attn_sample_kernel, layer=layer, n_pages=n_pages, page=page, ts=ts, wbuf=wbuf,
                             nsp=nsp, nc=nc, nb=nb, gqa=gqa, att_w=att_w, has_prev=win_prev is not None)
    per_b = lambda shape: pl.BlockSpec((None,) + shape, lambda b, pt: (b,) + (0,) * len(shape))
    const = lambda shape: pl.BlockSpec(shape, lambda b, pt: (0,) * len(shape))
    in_specs = [
        per_b((N_KV, rows, HEAD_DIM)),
        per_b((N_KV, rows, HEAD_DIM)),
        per_b((N_KV, rows, HEAD_DIM)),
        const((N_KV, rows, HEAD_DIM)),
        per_b((ts * n_kv_streams, HEAD_DIM)),
        per_b((ts * n_win_streams, HEAD_DIM)),
        pl.BlockSpec((None, None, wbuf * n_win_streams, HEAD_DIM), lambda b, pt: (layer, b, 0, 0)),
        const((2, CMP_LEN, HEAD_DIM)),
        const((2, CMP_STRIDE * HEAD_DIM, 2 * HEAD_DIM)),
        const((2, HEAD_DIM, HEAD_DIM)),
        const((LANES, nsp * CMP_STRIDE)),
        pl.BlockSpec(memory_space=pl.ANY),
    ]
    args = [page_table.reshape(-1), q_r, gcol_r, z_r, ggrp_r, newkv_rows, neww_rows, win_rows, pe, w1, w2,
            expand, cache_row]
    aliases = {}
    if win_prev is not None:
        in_specs.append(pl.BlockSpec(memory_space=pl.ANY))
        aliases = {len(args): 1}
        args.append(win_prev)
    depth = win_rows.shape[0]
    grid_spec = pltpu.PrefetchScalarGridSpec(
        num_scalar_prefetch=1,
        grid=(db,),
        in_specs=in_specs,
        out_specs=[per_b((N_KV, rows, HEAD_DIM)),
                   pl.BlockSpec((None, None, wbuf * n_win_streams, HEAD_DIM), lambda b, pt: (layer, b, 0, 0))],
        scratch_shapes=[pltpu.VMEM((2, n_kv_streams, nsp * CMP_STRIDE, HEAD_DIM), F32),
                        pltpu.VMEM((n_win_streams, wbuf + LANES, HEAD_DIM), F32),
                        pltpu.VMEM((2, 8, HEAD_DIM), F32),
                        pltpu.SemaphoreType.DMA((2,))],
    )
    return pl.pallas_call(
        kern,
        grid_spec=grid_spec,
        out_shape=[jax.ShapeDtypeStruct((db, N_KV, rows, HEAD_DIM), F32),
                   jax.ShapeDtypeStruct((depth, db, wbuf * n_win_streams, HEAD_DIM), F32)],
        input_output_aliases=aliases,
        compiler_params=_cparams(("arbitrary",)),
        name="attn_sample",
    )(*args)


def _merge_kernel(ya_ref, yb_ref, x_ref, gate_ref, w_ref, g_ref, o_ref, *, conv_ch):
    y = (_dot(ya_ref[...].astype(BF16), w_ref[:conv_ch, :])
         + _dot(yb_ref[...].astype(BF16), w_ref[conv_ch:, :]))
    yn = y * lax.rsqrt(jnp.mean(y * y, axis=-1, keepdims=True) + EPS)
    o_ref[...] = x_ref[...] + gate_ref[...] * (yn * g_ref[...])


def _merge(ya, yb, x2d, gate, w_out, g_post, tm):
    r, d = x2d.shape
    conv_ch = ya.shape[1]
    tiles_per_gate = (r // tm) // gate.shape[0]
    s_rows = gate.shape[1]
    return pl.pallas_call(
        functools.partial(_merge_kernel, conv_ch=conv_ch),
        grid=(r // tm,),
        in_specs=[
            pl.BlockSpec((tm, conv_ch), lambda i: (i, 0)),
            pl.BlockSpec((tm, d - conv_ch), lambda i: (i, 0)),
            pl.BlockSpec((tm, d), lambda i: (i, 0)),
            pl.BlockSpec((None, s_rows, d), lambda i: (i // tiles_per_gate, 0, 0)),
            pl.BlockSpec((d, d), lambda i: (0, 0)),
            pl.BlockSpec((1, d), lambda i: (0, 0)),
        ],
        out_specs=pl.BlockSpec((tm, d), lambda i: (i, 0)),
        out_shape=jax.ShapeDtypeStruct((r, d), F32),
        compiler_params=_cparams(("parallel",)),
        name="merge",
    )(ya, yb, x2d, gate, w_out, g_post)


def _rope_tables(pos):
    half = HEAD_DIM // 2
    inv = jnp.power(ROPE_THETA, -jnp.arange(half, dtype=F32) / half)
    ang = pos.astype(F32)[:, None] * inv
    cos, sin = jnp.cos(ang), jnp.sin(ang)
    return jnp.concatenate([cos, cos], axis=-1), jnp.concatenate([-sin, sin], axis=-1)


def _expand_matrix(n_keys):
    blk = jnp.arange(n_keys, dtype=jnp.int32) // SEL_BLOCK
    return (blk[None, :] == jnp.arange(LANES, dtype=jnp.int32)[:, None]).astype(BF16)


def kernel(x_prompt, x_sample, cache_kv, cache_win, state_conv, page_table, c_prompt, c_sample,
           w_ada, b_ada, g_pre, w_in, conv_w, cmp_pe, cmp_w1, cmp_w2, g_grp, w_out, g_post):
    bp, tp, d = x_prompt.shape
    bs, ts, _ = x_sample.shape
    depth = w_in.shape[0]
    conv_ch = conv_w.shape[2]
    att_w = d - conv_ch
    n_heads = att_w // HEAD_DIM
    gqa = n_heads // N_KV
    n_pages, page = page_table.shape[1], cache_kv.shape[2]
    past = n_pages * page
    wbuf = cache_win.shape[2]
    assert tp % TQ == 0 and tp % KC_SEL == 0 and ts <= T_PAD
    assert att_w == 4 * KV_W and 3 * n_heads <= LANES

    c_q = 4 * conv_ch
    c_kv = c_q + att_w
    c_zb = c_kv + 4 * KV_W
    c_kvw = c_zb + att_w
    cols = {"q": c_q, "kv": c_kv, "kvs": c_kv + 2 * KV_W, "zb": c_zb, "kvw": c_kvw}
    o_kw = c_kv + 4 * KV_W
    o_g = o_kw + 2 * KV_W
    o_zb = o_g + 3 * n_heads
    w_b = w_in.astype(BF16)
    w_main = jnp.concatenate([w_b[:, :, :o_kw], w_b[:, :, o_zb:], w_b[:, :, o_kw:o_g]], axis=-1)
    w_gate = jnp.pad(w_b[:, :, o_g:o_zb], ((0, 0), (0, 0), (0, LANES - 3 * n_heads)))
    w_out_b = w_out.astype(BF16)
    half = CMP_STRIDE * HEAD_DIM
    w1_b = jnp.concatenate([cmp_w1[:, :, :half], cmp_w1[:, :, half:]], axis=-1).astype(BF16)
    w2_b = cmp_w2.astype(BF16)

    mod = _modulation(jnp.concatenate([c_prompt, c_sample], axis=0), w_ada, b_ada)

    cos_p, sin_p = _rope_tables(jnp.arange(tp))
    pos_s = past + jnp.arange(ts)
    cos_s, sin_s = (jnp.tile(a, (bs, 1)) for a in _rope_tables(pos_s))
    expand_p = _expand_matrix(tp).T
    nsub_s = -(-(past + ts) // CMP_STRIDE)
    nsp_s = -(-nsub_s // 8) * 8
    expand_s = _expand_matrix(nsp_s * CMP_STRIDE)

    tm_p = min(tp, TM_PROJ)
    rows_s = bs * ts
    cache_row = cache_kv.reshape(depth, cache_kv.shape[1], page, 4 * N_KV, HEAD_DIM)
    win_rows = cache_win.reshape(depth, bs, wbuf * 2 * N_KV, HEAD_DIM)

    def to_heads(a):
        a = a.reshape(bs, ts, N_KV, gqa, HEAD_DIM).transpose(0, 2, 3, 1, 4)
        a = jnp.pad(a, ((0, 0), (0, 0), (0, 0), (0, T_PAD - ts), (0, 0)))
        return a.reshape(bs, N_KV, gqa * T_PAD, HEAD_DIM)

    xp = x_prompt.reshape(bp * tp, d)
    xs = x_sample.reshape(rows_s, d)
    win_keep = min(WINDOW, tp)
    rows_p = rows_sm = win_s = None
    conv_p, conv_s = [], []
    for l in range(depth):
        shift, scale, gate = mod[l, :, :d], mod[l, :, d:2 * d], mod[l, :, 2 * d:]
        g_a, g_b = g_grp[l, None, :conv_ch], g_grp[l, None, conv_ch:]

        u, gates, *rows_p = _in_projection(xp, scale[:bp, None], shift[:bp, None], g_pre[l, None], cos_p, sin_p,
                                           w_main[l], w_gate[l], tm_p, conv_ch, att_w, l, depth, bp, win_keep,
                                           rows_p)
        u3 = u.reshape(bp, tp, -1)
        ya, cbuf = _short_conv(u3, jnp.zeros((bp, CONV_WIDTH - 1, conv_ch), F32), conv_w[l], g_a,
                               min(tp, TR_CONV), BF16)
        kcmp = _compress_prompt(u3, c_kv, cmp_pe[l], w1_b[l], w2_b[l])
        yb = _attention_prompt(u3, gates.reshape(bp, tp, LANES), kcmp, g_b, expand_p, cols, att_w)
        xp = _merge(ya.reshape(bp * tp, conv_ch), yb.reshape(bp * tp, att_w), xp, gate[:bp, None],
                    w_out_b[l], g_post[l, None], min(tp, TM_MERGE))
        conv_p.append(cbuf)

        rep = lambda a: jnp.repeat(a[bp:], ts, axis=0)[None]
        u, gates, *rows_sm = _in_projection(xs, rep(scale), rep(shift), g_pre[l, None], cos_s, sin_s,
                                            w_main[l], w_gate[l], rows_s, conv_ch, att_w, l, depth, 1, rows_s,
                                            rows_sm)
        ya, cbuf = _short_conv_step(u, state_conv[l], conv_w[l], g_a, ts)
        q_r = to_heads(u[:, c_q:c_q + att_w])
        z_r = to_heads(u[:, c_zb:c_zb + att_w])
        gcol = gates[:, :3 * n_heads].reshape(rows_s, n_heads, 3)
        gcol_r = to_heads(jnp.pad(gcol, ((0, 0), (0, 0), (0, HEAD_DIM - 3))).reshape(rows_s, att_w))
        ggrp_r = jnp.broadcast_to(g_b.reshape(N_KV, gqa, 1, HEAD_DIM),
                                  (N_KV, gqa, T_PAD, HEAD_DIM)).reshape(N_KV, gqa * T_PAD, HEAD_DIM)
        newkv_rows = rows_sm[0][l].reshape(bs, ts * 4 * N_KV, HEAD_DIM)
        neww_rows = rows_sm[1][l].reshape(bs, ts * 2 * N_KV, HEAD_DIM)
        yb_r, win_s = _attention_sample(l, page_table, cache_row, win_rows, newkv_rows, neww_rows, q_r, gcol_r,
                                        z_r, ggrp_r, cmp_pe[l], w1_b[l], w2_b[l], expand_s, ts, att_w, win_s)
        yb = yb_r.reshape(bs, N_KV, gqa, T_PAD, HEAD_DIM)[:, :, :, :ts].transpose(0, 3, 1, 2, 4)
        xs = _merge(ya, yb.reshape(rows_s, att_w), xs, rep(gate), w_out_b[l], g_post[l, None], rows_s)
        conv_s.append(cbuf)

    kv_shape = lambda b, t: (depth, b, t, 4, N_KV, HEAD_DIM)
    win_shape = lambda b, t: (depth, b, t, 2, N_KV, HEAD_DIM)
    return (xp.reshape(bp, tp, d), xs.reshape(bs, ts, d),
            rows_p[0].reshape(kv_shape(bp, tp)), rows_sm[0].reshape(kv_shape(bs, ts)),
            rows_p[1].reshape(win_shape(bp, win_keep)), win_s.reshape(win_shape(bs, wbuf)),
            jnp.stack(conv_p), jnp.stack(conv_s))
```

```python
import functools

import jax
import jax.numpy as jnp
from jax import lax
from jax.experimental import pallas as pl
from jax.experimental.pallas import tpu as pltpu

F32 = jnp.float32
BF16 = jnp.bfloat16

HEAD_DIM = 128
N_KV = 2
CONV_WIDTH = 3
CMP_LEN = 32
CMP_STRIDE = 16
SEL_BLOCK = 64
N_SELECT = 8
WINDOW = 512
ROPE_THETA = 10000.0
EPS = 1e-6
NEG = -1e30
FORCE_SCORE = 1e3
KV_W = N_KV * HEAD_DIM
LANES = 128
VMEM_LIMIT = 52 * 1024 * 1024

TM_PROJ = 1024
TN_MOD = 768
TR_CONV = 512
TM_MERGE = 512
TQ = 256
KC_SEL = 512
T_PAD = 8


def _cparams(sem):
    return pltpu.CompilerParams(dimension_semantics=sem, vmem_limit_bytes=VMEM_LIMIT)


def _silu(x):
    return x / (1.0 + jnp.exp(-x))


def _sigmoid(x):
    return 1.0 / (1.0 + jnp.exp(-x))


def _dot(a, b):
    return jnp.dot(a, b, preferred_element_type=F32)


def _dot_nt(a, b):
    return lax.dot_general(a, b, (((1,), (1,)), ((), ())), preferred_element_type=F32)


def _mod_kernel(c_ref, w_ref, b_ref, o_ref):
    a = _silu(c_ref[...]).astype(BF16)
    o_ref[...] = _dot(a, w_ref[...].astype(BF16)) + b_ref[...]


def _modulation(c_all, w_ada, b_ada):
    depth, d, n = w_ada.shape
    rows = c_all.shape[0]
    tn = TN_MOD
    return pl.pallas_call(
        _mod_kernel,
        grid=(depth, n // tn),
        in_specs=[
            pl.BlockSpec((rows, d), lambda l, j: (0, 0)),
            pl.BlockSpec((None, d, tn), lambda l, j: (l, 0, j)),
            pl.BlockSpec((None, 1, tn), lambda l, j: (l, 0, j)),
        ],
        out_specs=pl.BlockSpec((None, rows, tn), lambda l, j: (l, 0, j)),
        out_shape=jax.ShapeDtypeStruct((depth, rows, n), F32),
        compiler_params=_cparams(("parallel", "parallel")),
        name="mod",
    )(c_all, w_ada, b_ada.reshape(depth, 1, n))


def _prep_kernel(a_ref, b_ref, c_ref, o_ref, g_ref, *, kw_tile, n_main, n_gate):
    j = pl.program_id(1)
    tn = o_ref.shape[1]

    @pl.when((j < kw_tile) | (j == n_main - 1))
    def _():
        o_ref[...] = a_ref[...].astype(BF16)

    @pl.when((j >= kw_tile) & (j < n_main - 1))
    def _():
        o_ref[...] = b_ref[0, :, n_gate:n_gate + tn].astype(BF16)

    @pl.when(j == n_main)
    def _():
        lane = lax.broadcasted_iota(jnp.int32, c_ref.shape[1:], 1)
        g_ref[...] = jnp.where(lane < n_gate, c_ref[0], 0.0).astype(BF16)


def _prep_weights(w_in, o_g, n_gate, att_w):
    depth, d, _ = w_in.shape
    tn = 2 * KV_W
    n_main = (o_g + att_w) // tn
    kw_tile = o_g // tn - 1
    src_tile = lambda j: jnp.where(j >= n_main - 1, kw_tile, jnp.minimum(j, kw_tile))
    zb_start = lambda j: pl.multiple_of(o_g + tn * jnp.clip(j - kw_tile, 0, att_w // tn - 1), LANES)
    window = lambda width, start: pl.BlockSpec((pl.Element(1), pl.Element(d), pl.Element(width)),
                                               lambda l, j: (l, 0, start(j)))
    return pl.pallas_call(
        functools.partial(_prep_kernel, kw_tile=kw_tile, n_main=n_main, n_gate=n_gate),
        grid=(depth, n_main + 1),
        in_specs=[pl.BlockSpec((None, d, tn), lambda l, j: (l, 0, src_tile(j))),
                  window(tn + LANES, zb_start),
                  window(LANES, lambda j: o_g)],
        out_specs=[pl.BlockSpec((None, d, tn), lambda l, j: (l, 0, jnp.minimum(j, n_main - 1))),
                   pl.BlockSpec((None, d, LANES), lambda l, j: (l, 0, 0))],
        out_shape=[jax.ShapeDtypeStruct((depth, d, n_main * tn), BF16),
                   jax.ShapeDtypeStruct((depth, d, LANES), BF16)],
        compiler_params=_cparams(("arbitrary", "arbitrary")),
        name="prep_weights",
    )(w_in, w_in, w_in)


def _rope_cols(v, cos, sin):
    outs = []
    for h in range(v.shape[1] // HEAD_DIM):
        xh = v[:, h * HEAD_DIM:(h + 1) * HEAD_DIM]
        outs.append(xh * cos + pltpu.roll(xh, HEAD_DIM // 2, 1) * sin)
    return outs[0] if len(outs) == 1 else jnp.concatenate(outs, axis=1)


def _inproj_kernel(*refs, full_lo, full_hi, kv_lo, win_tile, tiles_per_seq, win_rows, has_prev):
    (x_ref, scale_ref, shift_ref, g_ref, cos_ref, sin_ref, w_ref, wg_ref) = refs[:8]
    u_ref, gates_ref, kv_ref, win_ref, h_ref = refs[8 + (2 if has_prev else 0):]
    i = pl.program_id(0)
    j = pl.program_id(1)
    tm = x_ref.shape[0]
    n_kv_streams, n_win_streams = 4 * N_KV, 2 * N_KV

    @pl.when(j == 0)
    def _():
        x = x_ref[...]
        y = x * lax.rsqrt(jnp.mean(x * x, axis=-1, keepdims=True) + EPS)
        h = (y * g_ref[...]) * (1.0 + scale_ref[...]) + shift_ref[...]
        hb = h.astype(BF16)
        h_ref[...] = hb
        gates_ref[...] = _sigmoid(_dot(hb, wg_ref[...]))

    u_ref[...] = _dot(h_ref[...], w_ref[...])

    @pl.when((j >= full_lo) & (j < full_hi))
    def _():
        u_ref[...] = _rope_cols(u_ref[...], cos_ref[...], sin_ref[...])

    def rope_first_half():
        u_ref[:, :KV_W] = _rope_cols(u_ref[:, :KV_W], cos_ref[...], sin_ref[...])

    for k in range(2):
        @pl.when(j == kv_lo + k)
        def _(k=k):
            rope_first_half()
            for q in range(n_kv_streams // 2):
                kv_ref[pl.ds(k * (n_kv_streams // 2) + q, tm, stride=n_kv_streams), :] = (
                    u_ref[:, q * HEAD_DIM:(q + 1) * HEAD_DIM])

    @pl.when(j == win_tile)
    def _():
        rope_first_half()

        @pl.when(i % tiles_per_seq == tiles_per_seq - 1)
        def _():
            for q in range(n_win_streams):
                win_ref[pl.ds(q, win_rows, stride=n_win_streams), :] = (
                    u_ref[pl.ds(tm - win_rows, win_rows), q * HEAD_DIM:(q + 1) * HEAD_DIM])


def _in_projection(x2d, scale, shift, g_pre, cos, sin, w_main, w_gate, tm, conv_ch, att_w, layer, depth,
                   n_seq, win_rows, prev):
    r, d = x2d.shape
    nm = w_main.shape[1]
    tn = 2 * KV_W
    n_tiles = r // tm
    tiles_per_scale = n_tiles // scale.shape[0]
    tiles_per_seq = n_tiles // n_seq
    s_rows = scale.shape[1]
    pos_tiles = cos.shape[0] // tm
    q0 = 4 * conv_ch // tn
    q1 = q0 + att_w // tn
    assert win_rows <= tm
    kern = functools.partial(_inproj_kernel, full_lo=q0, full_hi=q1, kv_lo=q1, win_tile=q1 + 2 + att_w // tn,
                             tiles_per_seq=tiles_per_seq, win_rows=win_rows, has_prev=prev is not None)
    n_kv_streams, n_win_streams = 4 * N_KV, 2 * N_KV
    in_specs = [
        pl.BlockSpec((tm, d), lambda i, j: (i, 0)),
        pl.BlockSpec((None, s_rows, d), lambda i, j: (i // tiles_per_scale, 0, 0)),
        pl.BlockSpec((None, s_rows, d), lambda i, j: (i // tiles_per_scale, 0, 0)),
        pl.BlockSpec((1, d), lambda i, j: (0, 0)),
        pl.BlockSpec((tm, HEAD_DIM), lambda i, j: (i % pos_tiles, 0)),
        pl.BlockSpec((tm, HEAD_DIM), lambda i, j: (i % pos_tiles, 0)),
        pl.BlockSpec((d, tn), lambda i, j: (0, j)),
        pl.BlockSpec((d, LANES), lambda i, j: (0, 0)),
    ]
    args = [x2d, scale, shift, g_pre, cos, sin, w_main, w_gate]
    aliases = {}
    if prev is not None:
        in_specs += [pl.BlockSpec(memory_space=pl.ANY), pl.BlockSpec(memory_space=pl.ANY)]
        args += list(prev)
        aliases = {8: 2, 9: 3}
    return pl.pallas_call(
        kern,
        grid=(n_tiles, nm // tn),
        in_specs=in_specs,
        out_specs=[
            pl.BlockSpec((tm, tn), lambda i, j: (i, j)),
            pl.BlockSpec((tm, LANES), lambda i, j: (i, 0)),
            pl.BlockSpec((None, tm * n_kv_streams, HEAD_DIM), lambda i, j: (layer, i, 0)),
            pl.BlockSpec((None, None, win_rows * n_win_streams, HEAD_DIM),
                         lambda i, j: (layer, i // tiles_per_seq, 0, 0)),
        ],
        out_shape=[jax.ShapeDtypeStruct((r, nm), F32), jax.ShapeDtypeStruct((r, LANES), F32),
                   jax.ShapeDtypeStruct((depth, r * n_kv_streams, HEAD_DIM), F32),
                   jax.ShapeDtypeStruct((depth, n_seq, win_rows * n_win_streams, HEAD_DIM), F32)],
        scratch_shapes=[pltpu.VMEM((tm, d), BF16)],
        input_output_aliases=aliases,
        compiler_params=_cparams(("arbitrary", "arbitrary")),
        name="inproj",
    )(*args)


def _conv_kernel(b_ref, c_ref, x_ref, z_ref, init_ref, w_ref, g_ref, ya_ref, st_ref, up_ref, *, tr):
    i = pl.program_id(1)
    pad = 8

    @pl.when(i == 0)
    def _():
        up_ref[pl.ds(pad - 2, 2), :] = init_ref[...]

    up_ref[pl.ds(pad, tr), :] = c_ref[...] * x_ref[...]
    w = w_ref[...]
    y = (w[0:1, :] * up_ref[pl.ds(pad - 2, tr), :]
         + w[1:2, :] * up_ref[pl.ds(pad - 1, tr), :]
         + w[2:3, :] * up_ref[pl.ds(pad, tr), :])
    ya = _silu(z_ref[...]) * (b_ref[...] * y)
    yn = ya * lax.rsqrt(jnp.mean(ya * ya, axis=-1, keepdims=True) + EPS)
    ya_ref[...] = (yn * g_ref[...]).astype(ya_ref.dtype)
    last = up_ref[pl.ds(pad + tr - 2, 2), :]
    st_ref[...] = last
    up_ref[pl.ds(pad - 2, 2), :] = last


def _short_conv(u3, init_state, conv_w, g_a, tr, out_dtype):
    b, t, _ = u3.shape
    c = conv_w.shape[1]
    col = lambda k: pl.BlockSpec((None, tr, c), lambda bi, i, k=k: (bi, i, k))
    return pl.pallas_call(
        functools.partial(_conv_kernel, tr=tr),
        grid=(b, t // tr),
        in_specs=[
            col(0), col(1), col(2), col(3),
            pl.BlockSpec((None, CONV_WIDTH - 1, c), lambda bi, i: (bi, 0, 0)),
            pl.BlockSpec((CONV_WIDTH, c), lambda bi, i: (0, 0)),
            pl.BlockSpec((1, c), lambda bi, i: (0, 0)),
        ],
        out_specs=[
            pl.BlockSpec((None, tr, c), lambda bi, i: (bi, i, 0)),
            pl.BlockSpec((None, CONV_WIDTH - 1, c), lambda bi, i: (bi, 0, 0)),
        ],
        out_shape=[jax.ShapeDtypeStruct((b, t, c), out_dtype),
                   jax.ShapeDtypeStruct((b, CONV_WIDTH - 1, c), F32)],
        scratch_shapes=[pltpu.VMEM((tr + 8, c), F32)],
        compiler_params=_cparams(("parallel", "arbitrary")),
        name="conv",
    )(u3, u3, u3, u3, init_state, conv_w, g_a)


def _conv_step_kernel(b_ref, c_ref, x_ref, z_ref, h1_ref, h2_ref, w_ref, g_ref, ya_ref, uc_ref, *, ts):
    uc = c_ref[...] * x_ref[...]
    t = lax.broadcasted_iota(jnp.int32, uc.shape, 0) % ts
    back1 = jnp.where(t < 1, h1_ref[...], pltpu.roll(uc, 1, 0))
    back2 = jnp.where(t < 2, h2_ref[...], pltpu.roll(uc, 2, 0))
    w = w_ref[...]
    y = w[0:1, :] * back2 + w[1:2, :] * back1 + w[2:3, :] * uc
    ya = _silu(z_ref[...]) * (b_ref[...] * y)
    yn = ya * lax.rsqrt(jnp.mean(ya * ya, axis=-1, keepdims=True) + EPS)
    ya_ref[...] = yn * g_ref[...]
    uc_ref[...] = uc


def _short_conv_step(u2d, init_state, conv_w, g_a, ts):
    r = u2d.shape[0]
    nseq, _, c = init_state.shape
    assert ts >= CONV_WIDTH - 1
    zero = jnp.zeros((nseq, ts - 1, c), F32)
    h1 = jnp.concatenate([init_state[:, 1:2], zero], axis=1).reshape(r, c)
    h2 = jnp.concatenate([init_state, zero[:, 1:]], axis=1).reshape(r, c)
    col = lambda k: pl.BlockSpec((r, c), lambda i, k=k: (0, k))
    full = lambda shape: pl.BlockSpec(shape, lambda i: (0,) * len(shape))
    ya, uc = pl.pallas_call(
        functools.partial(_conv_step_kernel, ts=ts),
        grid=(1,),
        in_specs=[col(0), col(1), col(2), col(3), full((r, c)), full((r, c)), full(conv_w.shape), full(g_a.shape)],
        out_specs=[full((r, c)), full((r, c))],
        out_shape=[jax.ShapeDtypeStruct((r, c), F32), jax.ShapeDtypeStruct((r, c), F32)],
        compiler_params=_cparams(("arbitrary",)),
        name="conv_step",
    )(u2d, u2d, u2d, u2d, h1, h2, conv_w, g_a)
    return ya, uc.reshape(nseq, ts, c)[:, ts - (CONV_WIDTH - 1):]


def _pe_term(pe, w1cat):
    flat = lambda lo: jnp.concatenate([pe[lo + r:lo + r + 1, :] for r in range(CMP_STRIDE)], axis=1)
    rows = jnp.concatenate([flat(0), flat(CMP_STRIDE), jnp.zeros((6, CMP_STRIDE * HEAD_DIM), F32)], axis=0)
    r = _dot(rows.astype(BF16), w1cat)
    return r[0:1, :HEAD_DIM] + r[1:2, HEAD_DIM:]


def _compress_block(load_sub, pe_term, w1cat, w2, nsp):
    a = jnp.concatenate([load_sub(r).astype(BF16) for r in range(CMP_STRIDE)], axis=1)
    pq = _dot(a, w1cat)
    pre = pq[:, :HEAD_DIM] + pltpu.roll(pq[:, HEAD_DIM:], nsp - 1, 0) + pe_term
    return _dot(_silu(pre).astype(BF16), w2)


def _compress_kernel(k_ref, pe_ref, w1_ref, w2_ref, o_ref, *, nsp):
    load_sub = lambda r: k_ref[pl.ds(r, nsp, stride=CMP_STRIDE), :]
    w1cat = w1_ref[...]
    o_ref[...] = _compress_block(load_sub, _pe_term(pe_ref[...], w1cat), w1cat, w2_ref[...], nsp)


def _compress_prompt(u3, kv_col0, cmp_pe, cmp_w1, cmp_w2):
    b, t, _ = u3.shape
    nsp = t // CMP_STRIDE
    blk0 = kv_col0 // HEAD_DIM
    return pl.pallas_call(
        functools.partial(_compress_kernel, nsp=nsp),
        grid=(b, 2 * N_KV),
        in_specs=[
            pl.BlockSpec((None, t, HEAD_DIM), lambda bi, s: (bi, 0, blk0 + s)),
            pl.BlockSpec((None, CMP_LEN, HEAD_DIM), lambda bi, s: (s // N_KV, 0, 0)),
            pl.BlockSpec((None, CMP_STRIDE * HEAD_DIM, 2 * HEAD_DIM), lambda bi, s: (s // N_KV, 0, 0)),
            pl.BlockSpec((None, HEAD_DIM, HEAD_DIM), lambda bi, s: (s // N_KV, 0, 0)),
        ],
        out_specs=pl.BlockSpec((None, None, nsp, HEAD_DIM), lambda bi, s: (bi, s, 0, 0)),
        out_shape=jax.ShapeDtypeStruct((b, 2 * N_KV, nsp, HEAD_DIM), F32),
        compiler_params=_cparams(("parallel", "parallel")),
        name="compress",
    )(u3, cmp_pe, cmp_w1, cmp_w2)


def _bias(mask):
    return jnp.where(mask, 0.0, NEG)


def _row_max(s):
    return jnp.maximum(jnp.max(s, axis=-1, keepdims=True), 0.1 * NEG)


def _softmax_parts(s):
    e = jnp.exp(s - _row_max(s))
    return e, jnp.maximum(jnp.sum(e, axis=-1, keepdims=True), 1e-30)


def _weighted_values(e, v):
    r = _dot(e.astype(BF16), jnp.concatenate([v, jnp.ones_like(v)], axis=1))
    return r[:, :HEAD_DIM], jnp.maximum(r[:, HEAD_DIM:], 1e-30)


def _fold_lanes(x, op):
    out = x[:, :LANES]
    for c in range(1, x.shape[1] // LANES):
        out = op(out, x[:, c * LANES:(c + 1) * LANES])
    return out


def _compressed_branch(qg, kc, vc, qpos_rows, nc, n_rep):
    ncp = min(kc.shape[0], -(-nc // LANES) * LANES)
    assert ncp >= nc
    kc, vc = kc[:ncp], vc[:ncp]
    cidx = lax.broadcasted_iota(jnp.int32, (1, ncp), 1)
    mask = ((cidx * CMP_STRIDE + (CMP_LEN - 1)) <= qpos_rows) & (cidx < nc)
    e, l = _softmax_parts(_dot_nt(qg, kc.astype(BF16)) + _bias(mask))
    p = e / l
    o = _dot(p.astype(BF16), vc.astype(BF16))
    rt = qg.shape[0] // n_rep
    p_tok = p[0:rt]
    for r in range(1, n_rep):
        p_tok = p_tok + p[r * rt:(r + 1) * rt]
    return o, p_tok


def _pool_matrix(rows, cols, blocks_on_rows):
    per = SEL_BLOCK // CMP_STRIDE
    r = lax.broadcasted_iota(jnp.int32, (rows, cols), 0)
    c = lax.broadcasted_iota(jnp.int32, (rows, cols), 1)
    hit = (c // per == r) if blocks_on_rows else (r // per == c)
    return jnp.where(hit, 1.0, 0.0).astype(F32)


def _top_blocks(score, blk, nb, axis):
    rank = jnp.zeros(score.shape, jnp.int32)
    for j in range(nb):
        one = score[j:j + 1, :] if axis == 0 else score[:, j:j + 1]
        beats = (one > score) | ((one == score) & (blk > j))
        rank = rank + jnp.where(beats, 1, 0)
    return jnp.where((rank < N_SELECT) & (score > 0.5 * NEG), 1.0, 0.0)


def _block_scores(imp, blk, cur):
    forced = (blk == 0) | (blk == cur) | (blk == cur - 1)
    return jnp.where(blk <= cur, jnp.where(forced, FORCE_SCORE, imp), NEG)


def _select_blocks_rows(p_tok, tok_row, nb):
    rt, ncp = p_tok.shape
    nbr = -(-nb // 8) * 8
    imp_t = lax.dot_general(_pool_matrix(nbr, ncp, True), p_tok, (((1,), (1,)), ((), ())),
                            preferred_element_type=F32, precision=lax.Precision.HIGHEST)
    blk = lax.broadcasted_iota(jnp.int32, (nbr, rt), 0)
    sel_t = _top_blocks(_block_scores(imp_t, blk, tok_row // SEL_BLOCK), blk, nb, 0)
    if nbr < LANES:
        sel_t = jnp.concatenate([sel_t, jnp.zeros((LANES - nbr, rt), F32)], axis=0)
    return _bias(sel_t.T > 0.5).astype(BF16)


def _select_blocks_lanes(p_tok, tok_col, nb):
    rt, ncp = p_tok.shape
    imp = jnp.dot(p_tok, _pool_matrix(ncp, LANES, False), preferred_element_type=F32,
                  precision=lax.Precision.HIGHEST)
    blk = lax.broadcasted_iota(jnp.int32, (rt, LANES), 1)
    return _top_blocks(_block_scores(imp, blk, tok_col // SEL_BLOCK), blk, nb, 1).astype(BF16)


def _attn_prompt_kernel(q_ref, kvs_ref, kvw_ref, cmp_ref, gates_ref, z_ref, ggrp_ref, exp_ref, o_ref,
                        s_ref, *, nc, nb, gqa, att_w, t, wk):
    i = pl.program_id(1)
    s0 = i * TQ
    scale = HEAD_DIM ** -0.5
    tok = lax.broadcasted_iota(jnp.int32, (TQ, 1), 0) + s0
    tok_row = lax.broadcasted_iota(jnp.int32, (1, TQ), 1) + s0
    qpos_rows = jnp.concatenate([tok] * gqa, axis=0)
    rows = TQ * gqa
    rep = lambda a: jnp.concatenate([a] * gqa, axis=0)
    kcol = lambda g: slice(g * HEAD_DIM, (g + 1) * HEAD_DIM)
    vcol = lambda g: slice(KV_W + g * HEAD_DIM, KV_W + (g + 1) * HEAD_DIM)
    groups = range(N_KV)

    qs = [(jnp.concatenate(
        [q_ref[:, (g * gqa + r) * HEAD_DIM:(g * gqa + r + 1) * HEAD_DIM] for r in range(gqa)],
        axis=0) * scale).astype(BF16) for g in groups]

    o_cmp, sel = [], []
    for g in groups:
        o, p_tok = _compressed_branch(qs[g], cmp_ref[g], cmp_ref[N_KV + g], qpos_rows, nc, gqa)
        o_cmp.append(o)
        sel.append(_select_blocks_rows(p_tok, tok_row, nb))

    w_lo = pl.multiple_of(jnp.clip(s0 - WINDOW, 0, t - wk), TQ)
    wpos = w_lo + lax.broadcasted_iota(jnp.int32, (1, wk), 1)
    wdist = tok - wpos
    wbias = rep(_bias((wdist >= 0) & (wdist < WINDOW)))
    o_win = []
    for g in groups:
        kw = kvw_ref[pl.ds(w_lo, wk), kcol(g)].astype(BF16)
        vw = kvw_ref[pl.ds(w_lo, wk), vcol(g)].astype(BF16)
        sw = _dot_nt(qs[g], kw) + wbias
        num, den = _weighted_values(jnp.exp(sw - _row_max(sw)), vw)
        o_win.append(num / den)

    head_rows = lambda a, r: a[r * TQ:(r + 1) * TQ]
    gate = lambda h, k: gates_ref[:, 3 * h + k:3 * h + k + 1]
    partial, zact = [], []
    for g in groups:
        for r in range(gqa):
            h = g * gqa + r
            partial.append(gate(h, 0) * head_rows(o_cmp[g], r) + gate(h, 2) * head_rows(o_win[g], r))
            zact.append(_silu(z_ref[:, h * HEAD_DIM:(h + 1) * HEAD_DIM]))

    n_sel = (s0 + TQ + KC_SEL - 1) // KC_SEL

    qb = [jnp.concatenate([qs[g], rep(sel[g])], axis=1) for g in groups]

    def score_body(ci, mruns, causal):
        k0 = pl.multiple_of(ci * KC_SEL, KC_SEL)
        blk = exp_ref[pl.ds(k0, KC_SEL), :]
        out = []
        for g in groups:
            k = kvs_ref[pl.ds(k0, KC_SEL), kcol(g)].astype(BF16)
            s = _dot_nt(qb[g], jnp.concatenate([k, blk], axis=1))
            if causal:
                s = s + rep(_bias((k0 + lax.broadcasted_iota(jnp.int32, (1, KC_SEL), 1)) <= tok))
            s_ref[g, :, pl.ds(k0, KC_SEL)] = s
            out.append(jnp.maximum(mruns[g], _fold_lanes(s, jnp.maximum)))
        return tuple(out)

    mruns = lax.fori_loop(0, n_sel - 1, functools.partial(score_body, causal=False),
                          tuple(jnp.full((rows, LANES), NEG, F32) for _ in groups))
    mruns = score_body(n_sel - 1, mruns, True)
    ms = [_row_max(mruns[g]) for g in groups]

    def value_body(ci, carry):
        k0 = pl.multiple_of(ci * KC_SEL, KC_SEL)
        out = []
        for g in groups:
            v = kvs_ref[pl.ds(k0, KC_SEL), vcol(g)].astype(BF16)
            e = jnp.exp(s_ref[g, :, pl.ds(k0, KC_SEL)] - ms[g])
            out.append(carry[g] + _dot(e.astype(BF16), jnp.concatenate([v, jnp.ones_like(v)], axis=1)))
        return tuple(out)

    sums = lax.fori_loop(0, n_sel, value_body, tuple(jnp.zeros((rows, 2 * HEAD_DIM), F32) for _ in groups))

    heads = []
    for g in groups:
        o_slc = sums[g][:, :HEAD_DIM] / jnp.maximum(sums[g][:, HEAD_DIM:], 1e-30)
        for r in range(gqa):
            h = g * gqa + r
            heads.append((partial[h] + gate(h, 1) * head_rows(o_slc, r)) * zact[h])

    ss = jnp.sum(heads[0] * heads[0], axis=-1, keepdims=True)
    for y in heads[1:]:
        ss = ss + jnp.sum(y * y, axis=-1, keepdims=True)
    inv = lax.rsqrt(ss / att_w + EPS)
    o_ref[...] = jnp.concatenate(
        [(y * inv) * ggrp_ref[:, h * HEAD_DIM:(h + 1) * HEAD_DIM] for h, y in enumerate(heads)],
        axis=1).astype(o_ref.dtype)


def _attention_prompt(u3, gates3, kcmp, g_b, expand, cols, att_w):
    b, t, _ = u3.shape
    gqa = att_w // HEAD_DIM // N_KV
    nsub = t // CMP_STRIDE
    nc = nsub - CMP_LEN // CMP_STRIDE + 1
    nb = max(-(-t // SEL_BLOCK), N_SELECT)
    wk = min(t, WINDOW + TQ)
    kern = functools.partial(_attn_prompt_kernel, nc=nc, nb=nb, gqa=gqa, att_w=att_w, t=t, wk=wk)
    two_kv = 2 * KV_W
    return pl.pallas_call(
        kern,
        grid=(b, t // TQ),
        in_specs=[
            pl.BlockSpec((None, TQ, att_w), lambda bi, i: (bi, i, cols["q"] // att_w)),
            pl.BlockSpec((None, t, two_kv), lambda bi, i: (bi, 0, cols["kvs"] // two_kv)),
            pl.BlockSpec((None, t, two_kv), lambda bi, i: (bi, 0, cols["kvw"] // two_kv)),
            pl.BlockSpec((None, 2 * N_KV, kcmp.shape[2], HEAD_DIM), lambda bi, i: (bi, 0, 0, 0)),
            pl.BlockSpec((None, TQ, LANES), lambda bi, i: (bi, i, 0)),
            pl.BlockSpec((None, TQ, att_w), lambda bi, i: (bi, i, cols["zb"] // att_w)),
            pl.BlockSpec((1, att_w), lambda bi, i: (0, 0)),
            pl.BlockSpec((t, LANES), lambda bi, i: (0, 0)),
        ],
        out_specs=pl.BlockSpec((None, TQ, att_w), lambda bi, i: (bi, i, 0)),
        out_shape=jax.ShapeDtypeStruct((b, t, att_w), BF16),
        scratch_shapes=[pltpu.VMEM((N_KV, TQ * gqa, t), F32)],
        compiler_params=_cparams(("parallel", "arbitrary")),
        name="attn_prompt",
    )(u3, u3, u3, kcmp, gates3, u3, g_b, expand)


def _page_copies(pt_ref, cache_ref, full_ref, sem_ref, layer, b, slot, n_pages, page):
    copies = []
    for p in range(n_pages):
        pg = pt_ref[b * n_pages + p]
        for s in range(4 * N_KV):
            copies.append(pltpu.make_async_copy(
                cache_ref.at[layer, pg, :, s, :], full_ref.at[slot, s, pl.ds(p * page, page), :],
                sem_ref.at[slot]))
    return copies


def _attn_sample_kernel(pt_ref, q_ref, gcol_ref, z_ref, ggrp_ref, newkv_ref, neww_ref, cw_ref,
                        pe_ref, w1_ref, w2_ref, exp_ref, cache_ref, *rest,
                        layer, n_pages, page, ts, wbuf, nsp, nc, nb, gqa, att_w, has_prev):
    o_ref, win_ref, full_ref, kwin_ref, pe_term_ref, sem_ref = rest[(1 if has_prev else 0):]
    n_kv_streams, n_win_streams, n_cmp_streams = 4 * N_KV, 2 * N_KV, 2 * N_KV
    past = n_pages * page
    ktot = nsp * CMP_STRIDE
    scale = HEAD_DIM ** -0.5
    rep = lambda a: jnp.concatenate([a] * gqa, axis=0)
    b = pl.program_id(0)
    n_b = pl.num_programs(0)
    slot = b % 2
    copies = lambda bb, sl: _page_copies(pt_ref, cache_ref, full_ref, sem_ref, layer, bb, sl, n_pages, page)

    @pl.when(b == 0)
    def _():
        for br in range(2):
            pe_term_ref[br] = jnp.broadcast_to(_pe_term(pe_ref[br], w1_ref[br]), (8, HEAD_DIM))
        for cp in copies(0, 0):
            cp.start()

    for parity in range(2):
        @pl.when((slot == parity) & (b + 1 < n_b))
        def _(parity=parity):
            for cp in copies(b + 1, 1 - parity):
                cp.start()

    for parity in range(2):
        @pl.when(slot == parity)
        def _(parity=parity):
            for cp in copies(b, parity):
                cp.wait()

    for s in range(n_kv_streams):
        full_ref[slot, s, pl.ds(past, ts), :] = newkv_ref[pl.ds(s, ts, stride=n_kv_streams), :]
        full_ref[slot, s, pl.ds(past + ts, ktot - past - ts), :] = jnp.zeros((ktot - past - ts, HEAD_DIM), F32)
    for s in range(n_win_streams):
        kwin_ref[s, pl.ds(0, wbuf), :] = cw_ref[pl.ds(s, wbuf, stride=n_win_streams), :]
        kwin_ref[s, pl.ds(wbuf, ts), :] = neww_ref[pl.ds(s, ts, stride=n_win_streams), :]
        kwin_ref[s, pl.ds(wbuf + ts, LANES - ts), :] = jnp.zeros((LANES - ts, HEAD_DIM), F32)
    keep = (wbuf - ts) * n_win_streams
    win_ref[pl.ds(0, keep), :] = cw_ref[pl.ds(ts * n_win_streams, keep), :]
    win_ref[pl.ds(keep, ts * n_win_streams), :] = neww_ref[...]

    tok = past + lax.broadcasted_iota(jnp.int32, (T_PAD, 1), 0)
    qpos_rows = rep(tok)
    kpos = lax.broadcasted_iota(jnp.int32, (1, ktot), 1)
    widx = lax.broadcasted_iota(jnp.int32, (1, wbuf + LANES), 1)
    wpos = past - wbuf + widx
    wdist = tok - wpos
    wbias = rep(_bias((wdist >= 0) & (wdist < WINDOW) & (wpos >= 0) & (widx < wbuf + ts)))

    groups = range(N_KV)
    stacked = [_compress_block(
        (lambda r, br=br: jnp.concatenate(
            [full_ref[slot, br * N_KV + g, pl.ds(r, nsp, stride=CMP_STRIDE), :] for g in groups], axis=0)),
        pe_term_ref[br][0:1, :], w1_ref[br], w2_ref[br], N_KV * nsp) for br in range(2)]
    cmp_kv = [[stacked[br][g * nsp:(g + 1) * nsp] for br in range(2)] for g in groups]
    qs = [(q_ref[g] * scale).astype(BF16) for g in groups]
    cmp_out = [_compressed_branch(qs[g], cmp_kv[g][0], cmp_kv[g][1], qpos_rows, nc, gqa) for g in groups]
    sels = [_select_blocks_lanes(cmp_out[g][1], tok, nb) for g in groups]
    sbias = [rep(_bias((_dot(sels[g], exp_ref[...]) > 0.5) & (kpos <= tok))) for g in groups]
    k_slc = lambda g: full_ref[slot, n_cmp_streams + g].astype(BF16)
    v_slc = lambda g: full_ref[slot, n_cmp_streams + N_KV + g].astype(BF16)
    slc = [_softmax_parts(_dot_nt(qs[g], k_slc(g)) + sbias[g]) for g in groups]
    o_slc = [_dot(slc[g][0].astype(BF16), v_slc(g)) / slc[g][1] for g in groups]
    win = [_softmax_parts(_dot_nt(qs[g], kwin_ref[g].astype(BF16)) + wbias) for g in groups]
    o_win = [_dot(win[g][0].astype(BF16), kwin_ref[N_KV + g].astype(BF16)) / win[g][1] for g in groups]
    ys = []
    for g in groups:
        gcol = gcol_ref[g]
        o = gcol[:, 0:1] * cmp_out[g][0] + gcol[:, 1:2] * o_slc[g] + gcol[:, 2:3] * o_win[g]
        ys.append(o * _silu(z_ref[g]))

    ss = None
    for y in ys:
        rs = jnp.sum(y * y, axis=-1, keepdims=True)
        for r in range(gqa):
            part = rs[r * T_PAD:(r + 1) * T_PAD]
            ss = part if ss is None else ss + part
    inv = rep(lax.rsqrt(ss / att_w + EPS))
    for g in range(N_KV):
        o_ref[g] = (ys[g] * inv) * ggrp_ref[g]


def _attention_sample(layer, page_table, cache_row, win_rows, newkv_rows, neww_rows, q_r, gcol_r, z_r,
                      ggrp_r, pe, w1, w2, expand, ts, att_w, win_prev):
    db = q_r.shape[0]
    n_pages = page_table.shape[1]
    n_kv_streams, n_win_streams = 4 * N_KV, 2 * N_KV
    page = cache_row.shape[2]
    past = n_pages * page
    wbuf = win_rows.shape[2] // n_win_streams
    gqa = att_w // HEAD_DIM // N_KV
    nsub = -(-(past + ts) // CMP_STRIDE)
    nc = nsub - CMP_LEN // CMP_STRIDE + 1
    nsp = -(-nsub // 8) * 8
    nb = max(-(-(past + ts) // SEL_BLOCK), N_SELECT)
    rows = gqa * T_PAD
    kern = functools.partial(_attn_sample_kernel, layer=layer, n_pages=n_pages, page=page, ts=ts, wbuf=wbuf,
                             nsp=nsp, nc=nc, nb=nb, gqa=gqa, att_w=att_w, has_prev=win_prev is not None)
    per_b = lambda shape: pl.BlockSpec((None,) + shape, lambda b, pt: (b,) + (0,) * len(shape))
    const = lambda shape: pl.BlockSpec(shape, lambda b, pt: (0,) * len(shape))
    in_specs = [
        per_b((N_KV, rows, HEAD_DIM)),
        per_b((N_KV, rows, HEAD_DIM)),
        per_b((N_KV, rows, HEAD_DIM)),
        const((N_KV, rows, HEAD_DIM)),
        per_b((ts * n_kv_streams, HEAD_DIM)),
        per_b((ts * n_win_streams, HEAD_DIM)),
        pl.BlockSpec((None, None, wbuf * n_win_streams, HEAD_DIM), lambda b, pt: (layer, b, 0, 0)),
        const((2, CMP_LEN, HEAD_DIM)),
        const((2, CMP_STRIDE * HEAD_DIM, 2 * HEAD_DIM)),
        const((2, HEAD_DIM, HEAD_DIM)),
        const((LANES, nsp * CMP_STRIDE)),
        pl.BlockSpec(memory_space=pl.ANY),
    ]
    args = [page_table.reshape(-1), q_r, gcol_r, z_r, ggrp_r, newkv_rows, neww_rows, win_rows, pe, w1, w2,
            expand, cache_row]
    aliases = {}
    if win_prev is not None:
        in_specs.append(pl.BlockSpec(memory_space=pl.ANY))
        aliases = {len(args): 1}
        args.append(win_prev)
    depth = win_rows.shape[0]
    grid_spec = pltpu.PrefetchScalarGridSpec(
        num_scalar_prefetch=1,
        grid=(db,),
        in_specs=in_specs,
        out_specs=[per_b((N_KV, rows, HEAD_DIM)),
                   pl.BlockSpec((None, None, wbuf * n_win_streams, HEAD_DIM), lambda b, pt: (layer, b, 0, 0))],
        scratch_shapes=[pltpu.VMEM((2, n_kv_streams, nsp * CMP_STRIDE, HEAD_DIM), F32),
                        pltpu.VMEM((n_win_streams, wbuf + LANES, HEAD_DIM), F32),
                        pltpu.VMEM((2, 8, HEAD_DIM), F32),
                        pltpu.SemaphoreType.DMA((2,))],
    )
    return pl.pallas_call(
        kern,
        grid_spec=grid_spec,
        out_shape=[jax.ShapeDtypeStruct((db, N_KV, rows, HEAD_DIM), F32),
                   jax.ShapeDtypeStruct((depth, db, wbuf * n_win_streams, HEAD_DIM), F32)],
        input_output_aliases=aliases,
        compiler_params=_cparams(("arbitrary",)),
        name="attn_sample",
    )(*args)


def _merge_kernel(ya_ref, yb_ref, x_ref, gate_ref, w_ref, g_ref, o_ref, *, conv_ch):
    y = (_dot(ya_ref[...].astype(BF16), w_ref[:conv_ch, :])
         + _dot(yb_ref[...].astype(BF16), w_ref[conv_ch:, :]))
    yn = y * lax.rsqrt(jnp.mean(y * y, axis=-1, keepdims=True) + EPS)
    o_ref[...] = x_ref[...] + gate_ref[...] * (yn * g_ref[...])


def _merge(ya, yb, x2d, gate, w_out, g_post, tm):
    r, d = x2d.shape
    conv_ch = ya.shape[1]
    tiles_per_gate = (r // tm) // gate.shape[0]
    s_rows = gate.shape[1]
    return pl.pallas_call(
        functools.partial(_merge_kernel, conv_ch=conv_ch),
        grid=(r // tm,),
        in_specs=[
            pl.BlockSpec((tm, conv_ch), lambda i: (i, 0)),
            pl.BlockSpec((tm, d - conv_ch), lambda i: (i, 0)),
            pl.BlockSpec((tm, d), lambda i: (i, 0)),
            pl.BlockSpec((None, s_rows, d), lambda i: (i // tiles_per_gate, 0, 0)),
            pl.BlockSpec((d, d), lambda i: (0, 0)),
            pl.BlockSpec((1, d), lambda i: (0, 0)),
        ],
        out_specs=pl.BlockSpec((tm, d), lambda i: (i, 0)),
        out_shape=jax.ShapeDtypeStruct((r, d), F32),
        compiler_params=_cparams(("parallel",)),
        name="merge",
    )(ya, yb, x2d, gate, w_out, g_post)


def _rope_tables(pos):
    half = HEAD_DIM // 2
    inv = jnp.power(ROPE_THETA, -jnp.arange(half, dtype=F32) / half)
    ang = pos.astype(F32)[:, None] * inv
    cos, sin = jnp.cos(ang), jnp.sin(ang)
    return jnp.concatenate([cos, cos], axis=-1), jnp.concatenate([-sin, sin], axis=-1)


def _expand_matrix(n_keys):
    blk = jnp.arange(n_keys, dtype=jnp.int32) // SEL_BLOCK
    return (blk[None, :] == jnp.arange(LANES, dtype=jnp.int32)[:, None]).astype(BF16)


def kernel(x_prompt, x_sample, cache_kv, cache_win, state_conv, page_table, c_prompt, c_sample,
           w_ada, b_ada, g_pre, w_in, conv_w, cmp_pe, cmp_w1, cmp_w2, g_grp, w_out, g_post):
    bp, tp, d = x_prompt.shape
    bs, ts, _ = x_sample.shape
    depth = w_in.shape[0]
    conv_ch = conv_w.shape[2]
    att_w = d - conv_ch
    n_heads = att_w // HEAD_DIM
    gqa = n_heads // N_KV
    n_pages, page = page_table.shape[1], cache_kv.shape[2]
    past = n_pages * page
    wbuf = cache_win.shape[2]
    assert tp % TQ == 0 and tp % KC_SEL == 0 and ts <= T_PAD
    assert att_w == 4 * KV_W and 3 * n_heads <= LANES

    c_q = 4 * conv_ch
    c_kv = c_q + att_w
    c_zb = c_kv + 4 * KV_W
    c_kvw = c_zb + att_w
    cols = {"q": c_q, "kv": c_kv, "kvs": c_kv + 2 * KV_W, "zb": c_zb, "kvw": c_kvw}
    o_kw = c_kv + 4 * KV_W
    o_g = o_kw + 2 * KV_W
    o_zb = o_g + 3 * n_heads
    assert o_g % (2 * KV_W) == 0 and o_zb + att_w == w_in.shape[2]
    w_main, w_gate = _prep_weights(w_in, o_g, o_zb - o_g, att_w)
    w_out_b = w_out.astype(BF16)
    half = CMP_STRIDE * HEAD_DIM
    w1_b = jnp.concatenate([cmp_w1[:, :, :half], cmp_w1[:, :, half:]], axis=-1).astype(BF16)
    w2_b = cmp_w2.astype(BF16)

    mod = _modulation(jnp.concatenate([c_prompt, c_sample], axis=0), w_ada, b_ada)

    cos_p, sin_p = _rope_tables(jnp.arange(tp))
    pos_s = past + jnp.arange(ts)
    cos_s, sin_s = (jnp.tile(a, (bs, 1)) for a in _rope_tables(pos_s))
    expand_p = _expand_matrix(tp).T
    nsub_s = -(-(past + ts) // CMP_STRIDE)
    nsp_s = -(-nsub_s // 8) * 8
    expand_s = _expand_matrix(nsp_s * CMP_STRIDE)

    tm_p = min(tp, TM_PROJ)
    rows_s = bs * ts
    cache_row = cache_kv.reshape(depth, cache_kv.shape[1], page, 4 * N_KV, HEAD_DIM)
    win_rows = cache_win.reshape(depth, bs, wbuf * 2 * N_KV, HEAD_DIM)

    def to_heads(a):
        a = a.reshape(bs, ts, N_KV, gqa, HEAD_DIM).transpose(0, 2, 3, 1, 4)
        a = jnp.pad(a, ((0, 0), (0, 0), (0, 0), (0, T_PAD - ts), (0, 0)))
        return a.reshape(bs, N_KV, gqa * T_PAD, HEAD_DIM)

    xp = x_prompt.reshape(bp * tp, d)
    xs = x_sample.reshape(rows_s, d)
    win_keep = min(WINDOW, tp)
    rows_p = rows_sm = win_s = None
    conv_p, conv_s = [], []
    for l in range(depth):
        shift, scale, gate = mod[l, :, :d], mod[l, :, d:2 * d], mod[l, :, 2 * d:]
        g_a, g_b = g_grp[l, None, :conv_ch], g_grp[l, None, conv_ch:]

        u, gates, *rows_p = _in_projection(xp, scale[:bp, None], shift[:bp, None], g_pre[l, None], cos_p, sin_p,
                                           w_main[l], w_gate[l], tm_p, conv_ch, att_w, l, depth, bp, win_keep,
                                           rows_p)
        u3 = u.reshape(bp, tp, -1)
        ya, cbuf = _short_conv(u3, jnp.zeros((bp, CONV_WIDTH - 1, conv_ch), F32), conv_w[l], g_a,
                               min(tp, TR_CONV), BF16)
        kcmp = _compress_prompt(u3, c_kv, cmp_pe[l], w1_b[l], w2_b[l])
        yb = _attention_prompt(u3, gates.reshape(bp, tp, LANES), kcmp, g_b, expand_p, cols, att_w)
        xp = _merge(ya.reshape(bp * tp, conv_ch), yb.reshape(bp * tp, att_w), xp, gate[:bp, None],
                    w_out_b[l], g_post[l, None], min(tp, TM_MERGE))
        conv_p.append(cbuf)

        rep = lambda a: jnp.repeat(a[bp:], ts, axis=0)[None]
        u, gates, *rows_sm = _in_projection(xs, rep(scale), rep(shift), g_pre[l, None], cos_s, sin_s,
                                            w_main[l], w_gate[l], rows_s, conv_ch, att_w, l, depth, 1, rows_s,
                                            rows_sm)
        ya, cbuf = _short_conv_step(u, state_conv[l], conv_w[l], g_a, ts)
        q_r = to_heads(u[:, c_q:c_q + att_w])
        z_r = to_heads(u[:, c_zb:c_zb + att_w])
        gcol = gates[:, :3 * n_heads].reshape(rows_s, n_heads, 3)
        gcol_r = to_heads(jnp.pad(gcol, ((0, 0), (0, 0), (0, HEAD_DIM - 3))).reshape(rows_s, att_w))
        ggrp_r = jnp.broadcast_to(g_b.reshape(N_KV, gqa, 1, HEAD_DIM),
                                  (N_KV, gqa, T_PAD, HEAD_DIM)).reshape(N_KV, gqa * T_PAD, HEAD_DIM)
        newkv_rows = rows_sm[0][l].reshape(bs, ts * 4 * N_KV, HEAD_DIM)
        neww_rows = rows_sm[1][l].reshape(bs, ts * 2 * N_KV, HEAD_DIM)
        yb_r, win_s = _attention_sample(l, page_table, cache_row, win_rows, newkv_rows, neww_rows, q_r, gcol_r,
                                        z_r, ggrp_r, cmp_pe[l], w1_b[l], w2_b[l], expand_s, ts, att_w, win_s)
        yb = yb_r.reshape(bs, N_KV, gqa, T_PAD, HEAD_DIM)[:, :, :, :ts].transpose(0, 3, 1, 2, 4)
        xs = _merge(ya, yb.reshape(rows_s, att_w), xs, rep(gate), w_out_b[l], g_post[l, None], rows_s)
        conv_s.append(cbuf)

    kv_shape = lambda b, t: (depth, b, t, 4, N_KV, HEAD_DIM)
    win_shape = lambda b, t: (depth, b, t, 2, N_KV, HEAD_DIM)
    return (xp.reshape(bp, tp, d), xs.reshape(bs, ts, d),
            rows_p[0].reshape(kv_shape(bp, tp)), rows_sm[0].reshape(kv_shape(bs, ts)),
            rows_p[1].reshape(win_shape(bp, win_keep)), win_s.reshape(win_shape(bs, wbuf)),
            jnp.stack(conv_p), jnp.stack(conv_s))
```

```python
import functools

import jax
import jax.numpy as jnp
from jax import lax
from jax.experimental import pallas as pl
from jax.experimental.pallas import tpu as pltpu

F32 = jnp.float32
BF16 = jnp.bfloat16

HEAD_DIM = 128
N_KV = 2
CONV_WIDTH = 3
CMP_LEN = 32
CMP_STRIDE = 16
SEL_BLOCK = 64
N_SELECT = 8
WINDOW = 512
ROPE_THETA = 10000.0
EPS = 1e-6
NEG = -1e30
FORCE_SCORE = 1e3
KV_W = N_KV * HEAD_DIM
LANES = 128
VMEM_LIMIT = 52 * 1024 * 1024

TM_PROJ = 1024
TN_MOD = 768
TR_CONV = 512
TM_MERGE = 512
TQ = 256
KC_SEL = 512
T_PAD = 8


def _cparams(sem):
    return pltpu.CompilerParams(dimension_semantics=sem, vmem_limit_bytes=VMEM_LIMIT)


def _silu(x):
    return x / (1.0 + jnp.exp(-x))


def _sigmoid(x):
    return 1.0 / (1.0 + jnp.exp(-x))


def _dot(a, b):
    return jnp.dot(a, b, preferred_element_type=F32)


def _dot_nt(a, b):
    return lax.dot_general(a, b, (((1,), (1,)), ((), ())), preferred_element_type=F32)


def _mod_kernel(c_ref, w_ref, b_ref, o_ref):
    a = _silu(c_ref[...]).astype(BF16)
    o_ref[...] = _dot(a, w_ref[...].astype(BF16)) + b_ref[...]


def _modulation(c_all, w_ada, b_ada):
    depth, d, n = w_ada.shape
    rows = c_all.shape[0]
    tn = TN_MOD
    return pl.pallas_call(
        _mod_kernel,
        grid=(depth, n // tn),
        in_specs=[
            pl.BlockSpec((rows, d), lambda l, j: (0, 0)),
            pl.BlockSpec((None, d, tn), lambda l, j: (l, 0, j)),
            pl.BlockSpec((None, 1, tn), lambda l, j: (l, 0, j)),
        ],
        out_specs=pl.BlockSpec((None, rows, tn), lambda l, j: (l, 0, j)),
        out_shape=jax.ShapeDtypeStruct((depth, rows, n), F32),
        compiler_params=_cparams(("parallel", "parallel")),
        name="mod",
    )(c_all, w_ada, b_ada.reshape(depth, 1, n))


def _rope_cols(v, cos, sin):
    outs = []
    for h in range(v.shape[1] // HEAD_DIM):
        xh = v[:, h * HEAD_DIM:(h + 1) * HEAD_DIM]
        outs.append(xh * cos + pltpu.roll(xh, HEAD_DIM // 2, 1) * sin)
    return outs[0] if len(outs) == 1 else jnp.concatenate(outs, axis=1)


def _inproj_kernel(*refs, full_lo, full_hi, kv_lo, win_tile, tiles_per_seq, win_rows, has_prev):
    (x_ref, scale_ref, shift_ref, g_ref, cos_ref, sin_ref, w_ref, wg_ref) = refs[:8]
    u_ref, gates_ref, kv_ref, win_ref, h_ref = refs[8 + (2 if has_prev else 0):]
    i = pl.program_id(0)
    j = pl.program_id(1)
    tm = x_ref.shape[0]
    n_kv_streams, n_win_streams = 4 * N_KV, 2 * N_KV

    @pl.when(j == 0)
    def _():
        x = x_ref[...]
        y = x * lax.rsqrt(jnp.mean(x * x, axis=-1, keepdims=True) + EPS)
        h = (y * g_ref[...]) * (1.0 + scale_ref[...]) + shift_ref[...]
        hb = h.astype(BF16)
        h_ref[...] = hb
        gates_ref[...] = _sigmoid(_dot(hb, wg_ref[...]))

    u_ref[...] = _dot(h_ref[...], w_ref[...])

    @pl.when((j >= full_lo) & (j < full_hi))
    def _():
        u_ref[...] = _rope_cols(u_ref[...], cos_ref[...], sin_ref[...])

    def rope_first_half():
        u_ref[:, :KV_W] = _rope_cols(u_ref[:, :KV_W], cos_ref[...], sin_ref[...])

    for k in range(2):
        @pl.when(j == kv_lo + k)
        def _(k=k):
            rope_first_half()
            for q in range(n_kv_streams // 2):
                kv_ref[pl.ds(k * (n_kv_streams // 2) + q, tm, stride=n_kv_streams), :] = (
                    u_ref[:, q * HEAD_DIM:(q + 1) * HEAD_DIM])

    @pl.when(j == win_tile)
    def _():
        rope_first_half()

        @pl.when(i % tiles_per_seq == tiles_per_seq - 1)
        def _():
            for q in range(n_win_streams):
                win_ref[pl.ds(q, win_rows, stride=n_win_streams), :] = (
                    u_ref[pl.ds(tm - win_rows, win_rows), q * HEAD_DIM:(q + 1) * HEAD_DIM])


def _in_projection(x2d, scale, shift, g_pre, cos, sin, w_main, w_gate, tm, conv_ch, att_w, layer, depth,
                   n_seq, win_rows, prev):
    r, d = x2d.shape
    nm = w_main.shape[1]
    tn = 2 * KV_W
    n_tiles = r // tm
    tiles_per_scale = n_tiles // scale.shape[0]
    tiles_per_seq = n_tiles // n_seq
    s_rows = scale.shape[1]
    pos_tiles = cos.shape[0] // tm
    q0 = 4 * conv_ch // tn
    q1 = q0 + att_w // tn
    assert win_rows <= tm
    kern = functools.partial(_inproj_kernel, full_lo=q0, full_hi=q1, kv_lo=q1, win_tile=q1 + 2 + att_w // tn,
                             tiles_per_seq=tiles_per_seq, win_rows=win_rows, has_prev=prev is not None)
    n_kv_streams, n_win_streams = 4 * N_KV, 2 * N_KV
    in_specs = [
        pl.BlockSpec((tm, d), lambda i, j: (i, 0)),
        pl.BlockSpec((None, s_rows, d), lambda i, j: (i // tiles_per_scale, 0, 0)),
        pl.BlockSpec((None, s_rows, d), lambda i, j: (i // tiles_per_scale, 0, 0)),
        pl.BlockSpec((1, d), lambda i, j: (0, 0)),
        pl.BlockSpec((tm, HEAD_DIM), lambda i, j: (i % pos_tiles, 0)),
        pl.BlockSpec((tm, HEAD_DIM), lambda i, j: (i % pos_tiles, 0)),
        pl.BlockSpec((d, tn), lambda i, j: (0, j)),
        pl.BlockSpec((d, LANES), lambda i, j: (0, 0)),
    ]
    args = [x2d, scale, shift, g_pre, cos, sin, w_main, w_gate]
    aliases = {}
    if prev is not None:
        in_specs += [pl.BlockSpec(memory_space=pl.ANY), pl.BlockSpec(memory_space=pl.ANY)]
        args += list(prev)
        aliases = {8: 2, 9: 3}
    return pl.pallas_call(
        kern,
        grid=(n_tiles, nm // tn),
        in_specs=in_specs,
        out_specs=[
            pl.BlockSpec((tm, tn), lambda i, j: (i, j)),
            pl.BlockSpec((tm, LANES), lambda i, j: (i, 0)),
            pl.BlockSpec((None, tm * n_kv_streams, HEAD_DIM), lambda i, j: (layer, i, 0)),
            pl.BlockSpec((None, None, win_rows * n_win_streams, HEAD_DIM),
                         lambda i, j: (layer, i // tiles_per_seq, 0, 0)),
        ],
        out_shape=[jax.ShapeDtypeStruct((r, nm), F32), jax.ShapeDtypeStruct((r, LANES), F32),
                   jax.ShapeDtypeStruct((depth, r * n_kv_streams, HEAD_DIM), F32),
                   jax.ShapeDtypeStruct((depth, n_seq, win_rows * n_win_streams, HEAD_DIM), F32)],
        scratch_shapes=[pltpu.VMEM((tm, d), BF16)],
        input_output_aliases=aliases,
        compiler_params=_cparams(("arbitrary", "arbitrary")),
        name="inproj",
    )(*args)


def _conv_kernel(b_ref, c_ref, x_ref, z_ref, init_ref, w_ref, g_ref, ya_ref, st_ref, up_ref, *, tr):
    i = pl.program_id(1)
    pad = 8

    @pl.when(i == 0)
    def _():
        up_ref[pl.ds(pad - 2, 2), :] = init_ref[...]

    up_ref[pl.ds(pad, tr), :] = c_ref[...] * x_ref[...]
    w = w_ref[...]
    y = (w[0:1, :] * up_ref[pl.ds(pad - 2, tr), :]
         + w[1:2, :] * up_ref[pl.ds(pad - 1, tr), :]
         + w[2:3, :] * up_ref[pl.ds(pad, tr), :])
    ya = _silu(z_ref[...]) * (b_ref[...] * y)
    yn = ya * lax.rsqrt(jnp.mean(ya * ya, axis=-1, keepdims=True) + EPS)
    ya_ref[...] = (yn * g_ref[...]).astype(ya_ref.dtype)
    last = up_ref[pl.ds(pad + tr - 2, 2), :]
    st_ref[...] = last
    up_ref[pl.ds(pad - 2, 2), :] = last


def _short_conv(u3, init_state, conv_w, g_a, tr, out_dtype):
    b, t, _ = u3.shape
    c = conv_w.shape[1]
    col = lambda k: pl.BlockSpec((None, tr, c), lambda bi, i, k=k: (bi, i, k))
    return pl.pallas_call(
        functools.partial(_conv_kernel, tr=tr),
        grid=(b, t // tr),
        in_specs=[
            col(0), col(1), col(2), col(3),
            pl.BlockSpec((None, CONV_WIDTH - 1, c), lambda bi, i: (bi, 0, 0)),
            pl.BlockSpec((CONV_WIDTH, c), lambda bi, i: (0, 0)),
            pl.BlockSpec((1, c), lambda bi, i: (0, 0)),
        ],
        out_specs=[
            pl.BlockSpec((None, tr, c), lambda bi, i: (bi, i, 0)),
            pl.BlockSpec((None, CONV_WIDTH - 1, c), lambda bi, i: (bi, 0, 0)),
        ],
        out_shape=[jax.ShapeDtypeStruct((b, t, c), out_dtype),
                   jax.ShapeDtypeStruct((b, CONV_WIDTH - 1, c), F32)],
        scratch_shapes=[pltpu.VMEM((tr + 8, c), F32)],
        compiler_params=_cparams(("parallel", "arbitrary")),
        name="conv",
    )(u3, u3, u3, u3, init_state, conv_w, g_a)


def _conv_step_kernel(b_ref, c_ref, x_ref, z_ref, h1_ref, h2_ref, w_ref, g_ref, ya_ref, uc_ref, *, ts):
    uc = c_ref[...] * x_ref[...]
    t = lax.broadcasted_iota(jnp.int32, uc.shape, 0) % ts
    back1 = jnp.where(t < 1, h1_ref[...], pltpu.roll(uc, 1, 0))
    back2 = jnp.where(t < 2, h2_ref[...], pltpu.roll(uc, 2, 0))
    w = w_ref[...]
    y = w[0:1, :] * back2 + w[1:2, :] * back1 + w[2:3, :] * uc
    ya = _silu(z_ref[...]) * (b_ref[...] * y)
    yn = ya * lax.rsqrt(jnp.mean(ya * ya, axis=-1, keepdims=True) + EPS)
    ya_ref[...] = yn * g_ref[...]
    uc_ref[...] = uc


def _short_conv_step(u2d, init_state, conv_w, g_a, ts):
    r = u2d.shape[0]
    nseq, _, c = init_state.shape
    assert ts >= CONV_WIDTH - 1
    zero = jnp.zeros((nseq, ts - 1, c), F32)
    h1 = jnp.concatenate([init_state[:, 1:2], zero], axis=1).reshape(r, c)
    h2 = jnp.concatenate([init_state, zero[:, 1:]], axis=1).reshape(r, c)
    col = lambda k: pl.BlockSpec((r, c), lambda i, k=k: (0, k))
    full = lambda shape: pl.BlockSpec(shape, lambda i: (0,) * len(shape))
    ya, uc = pl.pallas_call(
        functools.partial(_conv_step_kernel, ts=ts),
        grid=(1,),
        in_specs=[col(0), col(1), col(2), col(3), full((r, c)), full((r, c)), full(conv_w.shape), full(g_a.shape)],
        out_specs=[full((r, c)), full((r, c))],
        out_shape=[jax.ShapeDtypeStruct((r, c), F32), jax.ShapeDtypeStruct((r, c), F32)],
        compiler_params=_cparams(("arbitrary",)),
        name="conv_step",
    )(u2d, u2d, u2d, u2d, h1, h2, conv_w, g_a)
    return ya, uc.reshape(nseq, ts, c)[:, ts - (CONV_WIDTH - 1):]


def _pe_term(pe, w1cat):
    flat = lambda lo: jnp.concatenate([pe[lo + r:lo + r + 1, :] for r in range(CMP_STRIDE)], axis=1)
    rows = jnp.concatenate([flat(0), flat(CMP_STRIDE), jnp.zeros((6, CMP_STRIDE * HEAD_DIM), F32)], axis=0)
    r = _dot(rows.astype(BF16), w1cat)
    return r[0:1, :HEAD_DIM] + r[1:2, HEAD_DIM:]


def _compress_block(load_sub, pe_term, w1cat, w2, nsp):
    a = jnp.concatenate([load_sub(r).astype(BF16) for r in range(CMP_STRIDE)], axis=1)
    pq = _dot(a, w1cat)
    pre = pq[:, :HEAD_DIM] + pltpu.roll(pq[:, HEAD_DIM:], nsp - 1, 0) + pe_term
    return _dot(_silu(pre).astype(BF16), w2)


def _compress_kernel(k_ref, pe_ref, w1_ref, w2_ref, o_ref, *, nsp):
    load_sub = lambda r: k_ref[pl.ds(r, nsp, stride=CMP_STRIDE), :]
    w1cat = w1_ref[...]
    o_ref[...] = _compress_block(load_sub, _pe_term(pe_ref[...], w1cat), w1cat, w2_ref[...], nsp)


def _compress_prompt(u3, kv_col0, cmp_pe, cmp_w1, cmp_w2):
    b, t, _ = u3.shape
    nsp = t // CMP_STRIDE
    blk0 = kv_col0 // HEAD_DIM
    return pl.pallas_call(
        functools.partial(_compress_kernel, nsp=nsp),
        grid=(b, 2 * N_KV),
        in_specs=[
            pl.BlockSpec((None, t, HEAD_DIM), lambda bi, s: (bi, 0, blk0 + s)),
            pl.BlockSpec((None, CMP_LEN, HEAD_DIM), lambda bi, s: (s // N_KV, 0, 0)),
            pl.BlockSpec((None, CMP_STRIDE * HEAD_DIM, 2 * HEAD_DIM), lambda bi, s: (s // N_KV, 0, 0)),
            pl.BlockSpec((None, HEAD_DIM, HEAD_DIM), lambda bi, s: (s // N_KV, 0, 0)),
        ],
        out_specs=pl.BlockSpec((None, None, nsp, HEAD_DIM), lambda bi, s: (bi, s, 0, 0)),
        out_shape=jax.ShapeDtypeStruct((b, 2 * N_KV, nsp, HEAD_DIM), F32),
        compiler_params=_cparams(("parallel", "parallel")),
        name="compress",
    )(u3, cmp_pe, cmp_w1, cmp_w2)


def _bias(mask):
    return jnp.where(mask, 0.0, NEG)


def _row_max(s):
    return jnp.maximum(jnp.max(s, axis=-1, keepdims=True), 0.1 * NEG)


def _softmax_parts(s):
    e = jnp.exp(s - _row_max(s))
    return e, jnp.maximum(jnp.sum(e, axis=-1, keepdims=True), 1e-30)


def _weighted_values(e, v):
    r = _dot(e.astype(BF16), jnp.concatenate([v, jnp.ones_like(v)], axis=1))
    return r[:, :HEAD_DIM], jnp.maximum(r[:, HEAD_DIM:], 1e-30)


def _fold_lanes(x, op):
    out = x[:, :LANES]
    for c in range(1, x.shape[1] // LANES):
        out = op(out, x[:, c * LANES:(c + 1) * LANES])
    return out


def _compressed_branch(qg, kc, vc, qpos_rows, nc, n_rep):
    ncp = min(kc.shape[0], -(-nc // LANES) * LANES)
    assert ncp >= nc
    kc, vc = kc[:ncp], vc[:ncp]
    cidx = lax.broadcasted_iota(jnp.int32, (1, ncp), 1)
    mask = ((cidx * CMP_STRIDE + (CMP_LEN - 1)) <= qpos_rows) & (cidx < nc)
    e, l = _softmax_parts(_dot_nt(qg, kc.astype(BF16)) + _bias(mask))
    p = e / l
    o = _dot(p.astype(BF16), vc.astype(BF16))
    rt = qg.shape[0] // n_rep
    p_tok = p[0:rt]
    for r in range(1, n_rep):
        p_tok = p_tok + p[r * rt:(r + 1) * rt]
    return o, p_tok


def _pool_matrix(rows, cols, blocks_on_rows):
    per = SEL_BLOCK // CMP_STRIDE
    r = lax.broadcasted_iota(jnp.int32, (rows, cols), 0)
    c = lax.broadcasted_iota(jnp.int32, (rows, cols), 1)
    hit = (c // per == r) if blocks_on_rows else (r // per == c)
    return jnp.where(hit, 1.0, 0.0).astype(F32)


def _top_blocks(score, blk, nb, axis):
    rank = jnp.zeros(score.shape, jnp.int32)
    for j in range(nb):
        one = score[j:j + 1, :] if axis == 0 else score[:, j:j + 1]
        beats = (one > score) | ((one == score) & (blk > j))
        rank = rank + jnp.where(beats, 1, 0)
    return jnp.where((rank < N_SELECT) & (score > 0.5 * NEG), 1.0, 0.0)


def _block_scores(imp, blk, cur):
    forced = (blk == 0) | (blk == cur) | (blk == cur - 1)
    return jnp.where(blk <= cur, jnp.where(forced, FORCE_SCORE, imp), NEG)


def _select_blocks_rows(p_tok, tok_row, nb):
    rt, ncp = p_tok.shape
    nbr = -(-nb // 8) * 8
    imp_t = lax.dot_general(_pool_matrix(nbr, ncp, True), p_tok, (((1,), (1,)), ((), ())),
                            preferred_element_type=F32, precision=lax.Precision.HIGHEST)
    blk = lax.broadcasted_iota(jnp.int32, (nbr, rt), 0)
    sel_t = _top_blocks(_block_scores(imp_t, blk, tok_row // SEL_BLOCK), blk, nb, 0)
    if nbr < LANES:
        sel_t = jnp.concatenate([sel_t, jnp.zeros((LANES - nbr, rt), F32)], axis=0)
    return _bias(sel_t.T > 0.5).astype(BF16)


def _select_blocks_lanes(p_tok, tok_col, nb):
    rt, ncp = p_tok.shape
    imp = jnp.dot(p_tok, _pool_matrix(ncp, LANES, False), preferred_element_type=F32,
                  precision=lax.Precision.HIGHEST)
    blk = lax.broadcasted_iota(jnp.int32, (rt, LANES), 1)
    return _top_blocks(_block_scores(imp, blk, tok_col // SEL_BLOCK), blk, nb, 1).astype(BF16)


def _attn_prompt_kernel(q_ref, kvs_ref, kvw_ref, cmp_ref, gates_ref, z_ref, ggrp_ref, exp_ref, o_ref,
                        s_ref, *, nc, nb, gqa, att_w, t, wk):
    i = pl.program_id(1)
    s0 = i * TQ
    scale = HEAD_DIM ** -0.5
    tok = lax.broadcasted_iota(jnp.int32, (TQ, 1), 0) + s0
    tok_row = lax.broadcasted_iota(jnp.int32, (1, TQ), 1) + s0
    qpos_rows = jnp.concatenate([tok] * gqa, axis=0)
    rows = TQ * gqa
    rep = lambda a: jnp.concatenate([a] * gqa, axis=0)
    kcol = lambda g: slice(g * HEAD_DIM, (g + 1) * HEAD_DIM)
    vcol = lambda g: slice(KV_W + g * HEAD_DIM, KV_W + (g + 1) * HEAD_DIM)
    groups = range(N_KV)

    qs = [(jnp.concatenate(
        [q_ref[:, (g * gqa + r) * HEAD_DIM:(g * gqa + r + 1) * HEAD_DIM] for r in range(gqa)],
        axis=0) * scale).astype(BF16) for g in groups]

    o_cmp, sel = [], []
    for g in groups:
        o, p_tok = _compressed_branch(qs[g], cmp_ref[g], cmp_ref[N_KV + g], qpos_rows, nc, gqa)
        o_cmp.append(o)
        sel.append(_select_blocks_rows(p_tok, tok_row, nb))

    w_lo = pl.multiple_of(jnp.clip(s0 - WINDOW, 0, t - wk), TQ)
    wpos = w_lo + lax.broadcasted_iota(jnp.int32, (1, wk), 1)
    wdist = tok - wpos
    wbias = rep(_bias((wdist >= 0) & (wdist < WINDOW)))
    o_win = []
    for g in groups:
        kw = kvw_ref[pl.ds(w_lo, wk), kcol(g)].astype(BF16)
        vw = kvw_ref[pl.ds(w_lo, wk), vcol(g)].astype(BF16)
        sw = _dot_nt(qs[g], kw) + wbias
        num, den = _weighted_values(jnp.exp(sw - _row_max(sw)), vw)
        o_win.append(num / den)

    head_rows = lambda a, r: a[r * TQ:(r + 1) * TQ]
    gate = lambda h, k: gates_ref[:, 3 * h + k:3 * h + k + 1]
    partial, zact = [], []
    for g in groups:
        for r in range(gqa):
            h = g * gqa + r
            partial.append(gate(h, 0) * head_rows(o_cmp[g], r) + gate(h, 2) * head_rows(o_win[g], r))
            zact.append(_silu(z_ref[:, h * HEAD_DIM:(h + 1) * HEAD_DIM]))

    n_sel = (s0 + TQ + KC_SEL - 1) // KC_SEL

    qb = [jnp.concatenate([qs[g], rep(sel[g])], axis=1) for g in groups]

    def score_body(ci, mruns, causal):
        k0 = pl.multiple_of(ci * KC_SEL, KC_SEL)
        blk = exp_ref[pl.ds(k0, KC_SEL), :]
        out = []
        for g in groups:
            k = kvs_ref[pl.ds(k0, KC_SEL), kcol(g)].astype(BF16)
            s = _dot_nt(qb[g], jnp.concatenate([k, blk], axis=1))
            if causal:
                s = s + rep(_bias((k0 + lax.broadcasted_iota(jnp.int32, (1, KC_SEL), 1)) <= tok))
            s_ref[g, :, pl.ds(k0, KC_SEL)] = s
            out.append(jnp.maximum(mruns[g], _fold_lanes(s, jnp.maximum)))
        return tuple(out)

    mruns = lax.fori_loop(0, n_sel - 1, functools.partial(score_body, causal=False),
                          tuple(jnp.full((rows, LANES), NEG, F32) for _ in groups))
    mruns = score_body(n_sel - 1, mruns, True)
    ms = [_row_max(mruns[g]) for g in groups]

    def value_body(ci, carry):
        k0 = pl.multiple_of(ci * KC_SEL, KC_SEL)
        out = []
        for g in groups:
            v = kvs_ref[pl.ds(k0, KC_SEL), vcol(g)].astype(BF16)
            e = jnp.exp(s_ref[g, :, pl.ds(k0, KC_SEL)] - ms[g])
            out.append(carry[g] + _dot(e.astype(BF16), jnp.concatenate([v, jnp.ones_like(v)], axis=1)))
        return tuple(out)

    sums = lax.fori_loop(0, n_sel, value_body, tuple(jnp.zeros((rows, 2 * HEAD_DIM), F32) for _ in groups))

    heads = []
    for g in groups:
        o_slc = sums[g][:, :HEAD_DIM] / jnp.maximum(sums[g][:, HEAD_DIM:], 1e-30)
        for r in range(gqa):
            h = g * gqa + r
            heads.append((partial[h] + gate(h, 1) * head_rows(o_slc, r)) * zact[h])

    ss = jnp.sum(heads[0] * heads[0], axis=-1, keepdims=True)
    for y in heads[1:]:
        ss = ss + jnp.sum(y * y, axis=-1, keepdims=True)
    inv = lax.rsqrt(ss / att_w + EPS)
    o_ref[...] = jnp.concatenate(
        [(y * inv) * ggrp_ref[:, h * HEAD_DIM:(h + 1) * HEAD_DIM] for h, y in enumerate(heads)],
        axis=1).astype(o_ref.dtype)


def _attention_prompt(u3, gates3, kcmp, g_b, expand, cols, att_w):
    b, t, _ = u3.shape
    gqa = att_w // HEAD_DIM // N_KV
    nsub = t // CMP_STRIDE
    nc = nsub - CMP_LEN // CMP_STRIDE + 1
    nb = max(-(-t // SEL_BLOCK), N_SELECT)
    wk = min(t, WINDOW + TQ)
    kern = functools.partial(_attn_prompt_kernel, nc=nc, nb=nb, gqa=gqa, att_w=att_w, t=t, wk=wk)
    two_kv = 2 * KV_W
    return pl.pallas_call(
        kern,
        grid=(b, t // TQ),
        in_specs=[
            pl.BlockSpec((None, TQ, att_w), lambda bi, i: (bi, i, cols["q"] // att_w)),
            pl.BlockSpec((None, t, two_kv), lambda bi, i: (bi, 0, cols["kvs"] // two_kv)),
            pl.BlockSpec((None, t, two_kv), lambda bi, i: (bi, 0, cols["kvw"] // two_kv)),
            pl.BlockSpec((None, 2 * N_KV, kcmp.shape[2], HEAD_DIM), lambda bi, i: (bi, 0, 0, 0)),
            pl.BlockSpec((None, TQ, LANES), lambda bi, i: (bi, i, 0)),
            pl.BlockSpec((None, TQ, att_w), lambda bi, i: (bi, i, cols["zb"] // att_w)),
            pl.BlockSpec((1, att_w), lambda bi, i: (0, 0)),
            pl.BlockSpec((t, LANES), lambda bi, i: (0, 0)),
        ],
        out_specs=pl.BlockSpec((None, TQ, att_w), lambda bi, i: (bi, i, 0)),
        out_shape=jax.ShapeDtypeStruct((b, t, att_w), BF16),
        scratch_shapes=[pltpu.VMEM((N_KV, TQ * gqa, t), F32)],
        compiler_params=_cparams(("parallel", "arbitrary")),
        name="attn_prompt",
    )(u3, u3, u3, kcmp, gates3, u3, g_b, expand)


def _page_copies(pt_ref, cache_ref, full_ref, sem_ref, layer, b, slot, n_pages, page):
    copies = []
    for p in range(n_pages):
        pg = pt_ref[b * n_pages + p]
        for s in range(4 * N_KV):
            copies.append(pltpu.make_async_copy(
                cache_ref.at[layer, pg, :, s, :], full_ref.at[slot, s, pl.ds(p * page, page), :],
                sem_ref.at[slot]))
    return copies


def _attn_sample_kernel(pt_ref, q_ref, gcol_ref, z_ref, ggrp_ref, newkv_ref, neww_ref, cw_ref,
                        pe_ref, w1_ref, w2_ref, exp_ref, cache_ref, *rest,
                        layer, n_pages, page, ts, wbuf, nsp, nc, nb, gqa, att_w, has_prev):
    o_ref, win_ref, full_ref, kwin_ref, pe_term_ref, sem_ref = rest[(1 if has_prev else 0):]
    n_kv_streams, n_win_streams, n_cmp_streams = 4 * N_KV, 2 * N_KV, 2 * N_KV
    past = n_pages * page
    ktot = nsp * CMP_STRIDE
    scale = HEAD_DIM ** -0.5
    rep = lambda a: jnp.concatenate([a] * gqa, axis=0)
    b = pl.program_id(0)
    n_b = pl.num_programs(0)
    slot = b % 2
    copies = lambda bb, sl: _page_copies(pt_ref, cache_ref, full_ref, sem_ref, layer, bb, sl, n_pages, page)

    @pl.when(b == 0)
    def _():
        for br in range(2):
            pe_term_ref[br] = jnp.broadcast_to(_pe_term(pe_ref[br], w1_ref[br]), (8, HEAD_DIM))
        for cp in copies(0, 0):
            cp.start()

    for parity in range(2):
        @pl.when((slot == parity) & (b + 1 < n_b))
        def _(parity=parity):
            for cp in copies(b + 1, 1 - parity):
                cp.start()

    for parity in range(2):
        @pl.when(slot == parity)
        def _(parity=parity):
            for cp in copies(b, parity):
                cp.wait()

    for s in range(n_kv_streams):
        full_ref[slot, s, pl.ds(past, ts), :] = newkv_ref[pl.ds(s, ts, stride=n_kv_streams), :]
        full_ref[slot, s, pl.ds(past + ts, ktot - past - ts), :] = jnp.zeros((ktot - past - ts, HEAD_DIM), F32)
    for s in range(n_win_streams):
        kwin_ref[s, pl.ds(0, wbuf), :] = cw_ref[pl.ds(s, wbuf, stride=n_win_streams), :]
        kwin_ref[s, pl.ds(wbuf, ts), :] = neww_ref[pl.ds(s, ts, stride=n_win_streams), :]
        kwin_ref[s, pl.ds(wbuf + ts, LANES - ts), :] = jnp.zeros((LANES - ts, HEAD_DIM), F32)
    keep = (wbuf - ts) * n_win_streams
    win_ref[pl.ds(0, keep), :] = cw_ref[pl.ds(ts * n_win_streams, keep), :]
    win_ref[pl.ds(keep, ts * n_win_streams), :] = neww_ref[...]

    tok = past + lax.broadcasted_iota(jnp.int32, (T_PAD, 1), 0)
    qpos_rows = rep(tok)
    kpos = lax.broadcasted_iota(jnp.int32, (1, ktot), 1)
    widx = lax.broadcasted_iota(jnp.int32, (1, wbuf + LANES), 1)
    wpos = past - wbuf + widx
    wdist = tok - wpos
    wbias = rep(_bias((wdist >= 0) & (wdist < WINDOW) & (wpos >= 0) & (widx < wbuf + ts)))

    groups = range(N_KV)
    stacked = [_compress_block(
        (lambda r, br=br: jnp.concatenate(
            [full_ref[slot, br * N_KV + g, pl.ds(r, nsp, stride=CMP_STRIDE), :] for g in groups], axis=0)),
        pe_term_ref[br][0:1, :], w1_ref[br], w2_ref[br], N_KV * nsp) for br in range(2)]
    cmp_kv = [[stacked[br][g * nsp:(g + 1) * nsp] for br in range(2)] for g in groups]
    qs = [(q_ref[g] * scale).astype(BF16) for g in groups]
    cmp_out = [_compressed_branch(qs[g], cmp_kv[g][0], cmp_kv[g][1], qpos_rows, nc, gqa) for g in groups]
    sels = [_select_blocks_lanes(cmp_out[g][1], tok, nb) for g in groups]
    sbias = [rep(_bias((_dot(sels[g], exp_ref[...]) > 0.5) & (kpos <= tok))) for g in groups]
    k_slc = lambda g: full_ref[slot, n_cmp_streams + g].astype(BF16)
    v_slc = lambda g: full_ref[slot, n_cmp_streams + N_KV + g].astype(BF16)
    slc = [_softmax_parts(_dot_nt(qs[g], k_slc(g)) + sbias[g]) for g in groups]
    o_slc = [_dot(slc[g][0].astype(BF16), v_slc(g)) / slc[g][1] for g in groups]
    win = [_softmax_parts(_dot_nt(qs[g], kwin_ref[g].astype(BF16)) + wbias) for g in groups]
    o_win = [_dot(win[g][0].astype(BF16), kwin_ref[N_KV + g].astype(BF16)) / win[g][1] for g in groups]
    ys = []
    for g in groups:
        gcol = gcol_ref[g]
        o = gcol[:, 0:1] * cmp_out[g][0] + gcol[:, 1:2] * o_slc[g] + gcol[:, 2:3] * o_win[g]
        ys.append(o * _silu(z_ref[g]))

    ss = None
    for y in ys:
        rs = jnp.sum(y * y, axis=-1, keepdims=True)
        for r in range(gqa):
            part = rs[r * T_PAD:(r + 1) * T_PAD]
            ss = part if ss is None else ss + part
    inv = rep(lax.rsqrt(ss / att_w + EPS))
    for g in range(N_KV):
        o_ref[g] = (ys[g] * inv) * ggrp_ref[g]


def _attention_sample(layer, page_table, cache_row, win_rows, newkv_rows, neww_rows, q_r, gcol_r, z_r,
                      ggrp_r, pe, w1, w2, expand, ts, att_w, win_prev):
    db = q_r.shape[0]
    n_pages = page_table.shape[1]
    n_kv_streams, n_win_streams = 4 * N_KV, 2 * N_KV
    page = cache_row.shape[2]
    past = n_pages * page
    wbuf = win_rows.shape[2] // n_win_streams
    gqa = att_w // HEAD_DIM // N_KV
    nsub = -(-(past + ts) // CMP_STRIDE)
    nc = nsub - CMP_LEN // CMP_STRIDE + 1
    nsp = -(-nsub // 8) * 8
    nb = max(-(-(past + ts) // SEL_BLOCK), N_SELECT)
    rows = gqa * T_PAD
    kern = functools.partial(_attn_sample_kernel, layer=layer, n_pages=n_pages, page=page, ts=ts, wbuf=wbuf,
                             nsp=nsp, nc=nc, nb=nb, gqa=gqa, att_w=att_w, has_prev=win_prev is not None)
    per_b = lambda shape: pl.BlockSpec((None,) + shape, lambda b, pt: (b,) + (0,) * len(shape))
    const = lambda shape: pl.BlockSpec(shape, lambda b, pt: (0,) * len(shape))
    in_specs = [
        per_b((N_KV, rows, HEAD_DIM)),
        per_b((N_KV, rows, HEAD_DIM)),
        per_b((N_KV, rows, HEAD_DIM)),
        const((N_KV, rows, HEAD_DIM)),
        per_b((ts * n_kv_streams, HEAD_DIM)),
        per_b((ts * n_win_streams, HEAD_DIM)),
        pl.BlockSpec((None, None, wbuf * n_win_streams, HEAD_DIM), lambda b, pt: (layer, b, 0, 0)),
        const((2, CMP_LEN, HEAD_DIM)),
        const((2, CMP_STRIDE * HEAD_DIM, 2 * HEAD_DIM)),
        const((2, HEAD_DIM, HEAD_DIM)),
        const((LANES, nsp * CMP_STRIDE)),
        pl.BlockSpec(memory_space=pl.ANY),
    ]
    args = [page_table.reshape(-1), q_r, gcol_r, z_r, ggrp_r, newkv_rows, neww_rows, win_rows, pe, w1, w2,
            expand, cache_row]
    aliases = {}
    if win_prev is not None:
        in_specs.append(pl.BlockSpec(memory_space=pl.ANY))
        aliases = {len(args): 1}
        args.append(win_prev)
    depth = win_rows.shape[0]
    grid_spec = pltpu.PrefetchScalarGridSpec(
        num_scalar_prefetch=1,
        grid=(db,),
        in_specs=in_specs,
        out_specs=[per_b((N_KV, rows, HEAD_DIM)),
                   pl.BlockSpec((None, None, wbuf * n_win_streams, HEAD_DIM), lambda b, pt: (layer, b, 0, 0))],
        scratch_shapes=[pltpu.VMEM((2, n_kv_streams, nsp * CMP_STRIDE, HEAD_DIM), F32),
                        pltpu.VMEM((n_win_streams, wbuf + LANES, HEAD_DIM), F32),
                        pltpu.VMEM((2, 8, HEAD_DIM), F32),
                        pltpu.SemaphoreType.DMA((2,))],
    )
    return pl.pallas_call(
        kern,
        grid_spec=grid_spec,
        out_shape=[jax.ShapeDtypeStruct((db, N_KV, rows, HEAD_DIM), F32),
                   jax.ShapeDtypeStruct((depth, db, wbuf * n_win_streams, HEAD_DIM), F32)],
        input_output_aliases=aliases,
        compiler_params=_cparams(("arbitrary",)),
        name="attn_sample",
    )(*args)


def _merge_kernel(ya_ref, yb_ref, x_ref, gate_ref, w_ref, g_ref, o_ref, *, conv_ch):
    y = (_dot(ya_ref[...].astype(BF16), w_ref[:conv_ch, :])
         + _dot(yb_ref[...].astype(BF16), w_ref[conv_ch:, :]))
    yn = y * lax.rsqrt(jnp.mean(y * y, axis=-1, keepdims=True) + EPS)
    o_ref[...] = x_ref[...] + gate_ref[...] * (yn * g_ref[...])


def _merge(ya, yb, x2d, gate, w_out, g_post, tm):
    r, d = x2d.shape
    conv_ch = ya.shape[1]
    tiles_per_gate = (r // tm) // gate.shape[0]
    s_rows = gate.shape[1]
    return pl.pallas_call(
        functools.partial(_merge_kernel, conv_ch=conv_ch),
        grid=(r // tm,),
        in_specs=[
            pl.BlockSpec((tm, conv_ch), lambda i: (i, 0)),
            pl.BlockSpec((tm, d - conv_ch), lambda i: (i, 0)),
            pl.BlockSpec((tm, d), lambda i: (i, 0)),
            pl.BlockSpec((None, s_rows, d), lambda i: (i // tiles_per_gate, 0, 0)),
            pl.BlockSpec((d, d), lambda i: (0, 0)),
            pl.BlockSpec((1, d), lambda i: (0, 0)),
        ],
        out_specs=pl.BlockSpec((tm, d), lambda i: (i, 0)),
        out_shape=jax.ShapeDtypeStruct((r, d), F32),
        compiler_params=_cparams(("parallel",)),
        name="merge",
    )(ya, yb, x2d, gate, w_out, g_post)


def _rope_tables(pos):
    half = HEAD_DIM // 2
    inv = jnp.power(ROPE_THETA, -jnp.arange(half, dtype=F32) / half)
    ang = pos.astype(F32)[:, None] * inv
    cos, sin = jnp.cos(ang), jnp.sin(ang)
    return jnp.concatenate([cos, cos], axis=-1), jnp.concatenate([-sin, sin], axis=-1)


def _expand_matrix(n_keys):
    blk = jnp.arange(n_keys, dtype=jnp.int32) // SEL_BLOCK
    return (blk[None, :] == jnp.arange(LANES, dtype=jnp.int32)[:, None]).astype(BF16)


def kernel(x_prompt, x_sample, cache_kv, cache_win, state_conv, page_table, c_prompt, c_sample,
           w_ada, b_ada, g_pre, w_in, conv_w, cmp_pe, cmp_w1, cmp_w2, g_grp, w_out, g_post):
    bp, tp, d = x_prompt.shape
    bs, ts, _ = x_sample.shape
    depth = w_in.shape[0]
    conv_ch = conv_w.shape[2]
    att_w = d - conv_ch
    n_heads = att_w // HEAD_DIM
    gqa = n_heads // N_KV
    n_pages, page = page_table.shape[1], cache_kv.shape[2]
    past = n_pages * page
    wbuf = cache_win.shape[2]
    assert tp % TQ == 0 and tp % KC_SEL == 0 and ts <= T_PAD
    assert att_w == 4 * KV_W and 3 * n_heads <= LANES

    c_q = 4 * conv_ch
    c_kv = c_q + att_w
    c_zb = c_kv + 4 * KV_W
    c_kvw = c_zb + att_w
    cols = {"q": c_q, "kv": c_kv, "kvs": c_kv + 2 * KV_W, "zb": c_zb, "kvw": c_kvw}
    o_kw = c_kv + 4 * KV_W
    o_g = o_kw + 2 * KV_W
    o_zb = o_g + 3 * n_heads
    w_main = jnp.concatenate([w_in[:, :, :o_kw], w_in[:, :, o_zb:], w_in[:, :, o_kw:o_g]], axis=-1).astype(BF16)
    w_gate = jnp.pad(w_in[:, :, o_g:o_zb], ((0, 0), (0, 0), (0, LANES - 3 * n_heads))).astype(BF16)
    w_out_b = w_out.astype(BF16)
    half = CMP_STRIDE * HEAD_DIM
    w1_b = jnp.concatenate([cmp_w1[:, :, :half], cmp_w1[:, :, half:]], axis=-1).astype(BF16)
    w2_b = cmp_w2.astype(BF16)

    mod = _modulation(jnp.concatenate([c_prompt, c_sample], axis=0), w_ada, b_ada)

    cos_p, sin_p = _rope_tables(jnp.arange(tp))
    pos_s = past + jnp.arange(ts)
    cos_s, sin_s = (jnp.tile(a, (bs, 1)) for a in _rope_tables(pos_s))
    expand_p = _expand_matrix(tp).T
    nsub_s = -(-(past + ts) // CMP_STRIDE)
    nsp_s = -(-nsub_s // 8) * 8
    expand_s = _expand_matrix(nsp_s * CMP_STRIDE)

    tm_p = min(tp, TM_PROJ)
    rows_s = bs * ts
    cache_row = cache_kv.reshape(depth, cache_kv.shape[1], page, 4 * N_KV, HEAD_DIM)
    win_rows = cache_win.reshape(depth, bs, wbuf * 2 * N_KV, HEAD_DIM)

    def to_heads(a):
        a = a.reshape(bs, ts, N_KV, gqa, HEAD_DIM).transpose(0, 2, 3, 1, 4)
        a = jnp.pad(a, ((0, 0), (0, 0), (0, 0), (0, T_PAD - ts), (0, 0)))
        return a.reshape(bs, N_KV, gqa * T_PAD, HEAD_DIM)

    xp = x_prompt.reshape(bp * tp, d)
    xs = x_sample.reshape(rows_s, d)
    win_keep = min(WINDOW, tp)
    rows_p = rows_sm = win_s = None
    conv_p, conv_s = [], []
    for l in range(depth):
        shift, scale, gate = mod[l, :, :d], mod[l, :, d:2 * d], mod[l, :, 2 * d:]
        g_a, g_b = g_grp[l, None, :conv_ch], g_grp[l, None, conv_ch:]

        u, gates, *rows_p = _in_projection(xp, scale[:bp, None], shift[:bp, None], g_pre[l, None], cos_p, sin_p,
                                           w_main[l], w_gate[l], tm_p, conv_ch, att_w, l, depth, bp, win_keep,
                                           rows_p)
        u3 = u.reshape(bp, tp, -1)
        ya, cbuf = _short_conv(u3, jnp.zeros((bp, CONV_WIDTH - 1, conv_ch), F32), conv_w[l], g_a,
                               min(tp, TR_CONV), BF16)
        kcmp = _compress_prompt(u3, c_kv, cmp_pe[l], w1_b[l], w2_b[l])
        yb = _attention_prompt(u3, gates.reshape(bp, tp, LANES), kcmp, g_b, expand_p, cols, att_w)
        xp = _merge(ya.reshape(bp * tp, conv_ch), yb.reshape(bp * tp, att_w), xp, gate[:bp, None],
                    w_out_b[l], g_post[l, None], min(tp, TM_MERGE))
        conv_p.append(cbuf)

        rep = lambda a: jnp.repeat(a[bp:], ts, axis=0)[None]
        u, gates, *rows_sm = _in_projection(xs, rep(scale), rep(shift), g_pre[l, None], cos_s, sin_s,
                                            w_main[l], w_gate[l], rows_s, conv_ch, att_w, l, depth, 1, rows_s,
                                            rows_sm)
        ya, cbuf = _short_conv_step(u, state_conv[l], conv_w[l], g_a, ts)
        q_r = to_heads(u[:, c_q:c_q + att_w])
        z_r = to_heads(u[:, c_zb:c_zb + att_w])
        gcol = gates[:, :3 * n_heads].reshape(rows_s, n_heads, 3)
        gcol_r = to_heads(jnp.pad(gcol, ((0, 0), (0, 0), (0, HEAD_DIM - 3))).reshape(rows_s, att_w))
        ggrp_r = jnp.broadcast_to(g_b.reshape(N_KV, gqa, 1, HEAD_DIM),
                                  (N_KV, gqa, T_PAD, HEAD_DIM)).reshape(N_KV, gqa * T_PAD, HEAD_DIM)
        newkv_rows = rows_sm[0][l].reshape(bs, ts * 4 * N_KV, HEAD_DIM)
        neww_rows = rows_sm[1][l].reshape(bs, ts * 2 * N_KV, HEAD_DIM)
        yb_r, win_s = _attention_sample(l, page_table, cache_row, win_rows, newkv_rows, neww_rows, q_r, gcol_r,
                                        z_r, ggrp_r, cmp_pe[l], w1_b[l], w2_b[l], expand_s, ts, att_w, win_s)
        yb = yb_r.reshape(bs, N_KV, gqa, T_PAD, HEAD_DIM)[:, :, :, :ts].transpose(0, 3, 1, 2, 4)
        xs = _merge(ya, yb.reshape(rows_s, att_w), xs, rep(gate), w_out_b[l], g_post[l, None], rows_s)
        conv_s.append(cbuf)

    kv_shape = lambda b, t: (depth, b, t, 4, N_KV, HEAD_DIM)
    win_shape = lambda b, t: (depth, b, t, 2, N_KV, HEAD_DIM)
    return (xp.reshape(bp, tp, d), xs.reshape(bs, ts, d),
            rows_p[0].reshape(kv_shape(bp, tp)), rows_sm[0].reshape(kv_shape(bs, ts)),
            rows_p[1].reshape(win_shape(bp, win_keep)), win_s.reshape(win_shape(bs, wbuf)),
            jnp.stack(conv_p), jnp.stack(conv_s))
```

```python
import functools

import jax
import jax.numpy as jnp
from jax import lax
from jax.experimental import pallas as pl
from jax.experimental.pallas import tpu as pltpu

F32 = jnp.float32
BF16 = jnp.bfloat16

HEAD_DIM = 128
N_KV = 2
CONV_WIDTH = 3
CMP_LEN = 32
CMP_STRIDE = 16
SEL_BLOCK = 64
N_SELECT = 8
WINDOW = 512
ROPE_THETA = 10000.0
EPS = 1e-6
NEG = -1e30
FORCE_SCORE = 1e3
KV_W = N_KV * HEAD_DIM
LANES = 128
VMEM_LIMIT = 52 * 1024 * 1024

TM_PROJ = 1024
TN_MOD = 768
TR_CONV = 512
TM_MERGE = 512
TQ = 256
KC_SEL = 512
T_PAD = 8


def _cparams(sem):
    return pltpu.CompilerParams(dimension_semantics=sem, vmem_limit_bytes=VMEM_LIMIT)


def _silu(x):
    return x / (1.0 + jnp.exp(-x))


def _sigmoid(x):
    return 1.0 / (1.0 + jnp.exp(-x))


def _dot(a, b):
    return jnp.dot(a, b, preferred_element_type=F32)


def _dot_nt(a, b):
    return lax.dot_general(a, b, (((1,), (1,)), ((), ())), preferred_element_type=F32)


def _mod_kernel(c_ref, w_ref, b_ref, o_ref):
    a = _silu(c_ref[...]).astype(BF16)
    o_ref[...] = _dot(a, w_ref[...].astype(BF16)) + b_ref[...]


def _modulation(c_all, w_ada, b_ada):
    depth, d, n = w_ada.shape
    rows = c_all.shape[0]
    tn = TN_MOD
    return pl.pallas_call(
        _mod_kernel,
        grid=(depth, n // tn),
        in_specs=[
            pl.BlockSpec((rows, d), lambda l, j: (0, 0)),
            pl.BlockSpec((None, d, tn), lambda l, j: (l, 0, j)),
            pl.BlockSpec((None, 1, tn), lambda l, j: (l, 0, j)),
        ],
        out_specs=pl.BlockSpec((None, rows, tn), lambda l, j: (l, 0, j)),
        out_shape=jax.ShapeDtypeStruct((depth, rows, n), F32),
        compiler_params=_cparams(("parallel", "parallel")),
        name="mod",
    )(c_all, w_ada, b_ada.reshape(depth, 1, n))


def _rope_cols(v, cos, sin):
    outs = []
    for h in range(v.shape[1] // HEAD_DIM):
        xh = v[:, h * HEAD_DIM:(h + 1) * HEAD_DIM]
        outs.append(xh * cos + pltpu.roll(xh, HEAD_DIM // 2, 1) * sin)
    return outs[0] if len(outs) == 1 else jnp.concatenate(outs, axis=1)


def _inproj_kernel(*refs, full_lo, full_hi, kv_lo, win_tile, tiles_per_seq, win_rows, has_prev):
    (x_ref, scale_ref, shift_ref, g_ref, cos_ref, sin_ref, w_ref, wg_ref) = refs[:8]
    u_ref, gates_ref, kv_ref, win_ref, h_ref = refs[8 + (2 if has_prev else 0):]
    i = pl.program_id(0)
    j = pl.program_id(1)
    tm = x_ref.shape[0]
    n_kv_streams, n_win_streams = 4 * N_KV, 2 * N_KV

    @pl.when(j == 0)
    def _():
        x = x_ref[...]
        y = x * lax.rsqrt(jnp.mean(x * x, axis=-1, keepdims=True) + EPS)
        h = (y * g_ref[...]) * (1.0 + scale_ref[...]) + shift_ref[...]
        hb = h.astype(BF16)
        h_ref[...] = hb
        gates_ref[...] = _sigmoid(_dot(hb, wg_ref[...]))

    u_ref[...] = _dot(h_ref[...], w_ref[...])

    @pl.when((j >= full_lo) & (j < full_hi))
    def _():
        u_ref[...] = _rope_cols(u_ref[...], cos_ref[...], sin_ref[...])

    def rope_first_half():
        u_ref[:, :KV_W] = _rope_cols(u_ref[:, :KV_W], cos_ref[...], sin_ref[...])

    for k in range(2):
        @pl.when(j == kv_lo + k)
        def _(k=k):
            rope_first_half()
            for q in range(n_kv_streams // 2):
                kv_ref[pl.ds(k * (n_kv_streams // 2) + q, tm, stride=n_kv_streams), :] = (
                    u_ref[:, q * HEAD_DIM:(q + 1) * HEAD_DIM])

    @pl.when(j == win_tile)
    def _():
        rope_first_half()

        @pl.when(i % tiles_per_seq == tiles_per_seq - 1)
        def _():
            for q in range(n_win_streams):
                win_ref[pl.ds(q, win_rows, stride=n_win_streams), :] = (
                    u_ref[pl.ds(tm - win_rows, win_rows), q * HEAD_DIM:(q + 1) * HEAD_DIM])


def _in_projection(x2d, scale, shift, g_pre, cos, sin, w_main, w_gate, tm, conv_ch, att_w, layer, depth,
                   n_seq, win_rows, prev):
    r, d = x2d.shape
    nm = w_main.shape[1]
    tn = 2 * KV_W
    n_tiles = r // tm
    tiles_per_scale = n_tiles // scale.shape[0]
    tiles_per_seq = n_tiles // n_seq
    s_rows = scale.shape[1]
    pos_tiles = cos.shape[0] // tm
    q0 = 4 * conv_ch // tn
    q1 = q0 + att_w // tn
    assert win_rows <= tm
    kern = functools.partial(_inproj_kernel, full_lo=q0, full_hi=q1, kv_lo=q1, win_tile=q1 + 2 + att_w // tn,
                             tiles_per_seq=tiles_per_seq, win_rows=win_rows, has_prev=prev is not None)
    n_kv_streams, n_win_streams = 4 * N_KV, 2 * N_KV
    in_specs = [
        pl.BlockSpec((tm, d), lambda i, j: (i, 0)),
        pl.BlockSpec((None, s_rows, d), lambda i, j: (i // tiles_per_scale, 0, 0)),
        pl.BlockSpec((None, s_rows, d), lambda i, j: (i // tiles_per_scale, 0, 0)),
        pl.BlockSpec((1, d), lambda i, j: (0, 0)),
        pl.BlockSpec((tm, HEAD_DIM), lambda i, j: (i % pos_tiles, 0)),
        pl.BlockSpec((tm, HEAD_DIM), lambda i, j: (i % pos_tiles, 0)),
        pl.BlockSpec((d, tn), lambda i, j: (0, j)),
        pl.BlockSpec((d, LANES), lambda i, j: (0, 0)),
    ]
    args = [x2d, scale, shift, g_pre, cos, sin, w_main, w_gate]
    aliases = {}
    if prev is not None:
        in_specs += [pl.BlockSpec(memory_space=pl.ANY), pl.BlockSpec(memory_space=pl.ANY)]
        args += list(prev)
        aliases = {8: 2, 9: 3}
    return pl.pallas_call(
        kern,
        grid=(n_tiles, nm // tn),
        in_specs=in_specs,
        out_specs=[
            pl.BlockSpec((tm, tn), lambda i, j: (i, j)),
            pl.BlockSpec((tm, LANES), lambda i, j: (i, 0)),
            pl.BlockSpec((None, tm * n_kv_streams, HEAD_DIM), lambda i, j: (layer, i, 0)),
            pl.BlockSpec((None, None, win_rows * n_win_streams, HEAD_DIM),
                         lambda i, j: (layer, i // tiles_per_seq, 0, 0)),
        ],
        out_shape=[jax.ShapeDtypeStruct((r, nm), F32), jax.ShapeDtypeStruct((r, LANES), F32),
                   jax.ShapeDtypeStruct((depth, r * n_kv_streams, HEAD_DIM), F32),
                   jax.ShapeDtypeStruct((depth, n_seq, win_rows * n_win_streams, HEAD_DIM), F32)],
        scratch_shapes=[pltpu.VMEM((tm, d), BF16)],
        input_output_aliases=aliases,
        compiler_params=_cparams(("arbitrary", "arbitrary")),
        name="inproj",
    )(*args)


def _conv_kernel(b_ref, c_ref, x_ref, z_ref, init_ref, w_ref, g_ref, ya_ref, st_ref, up_ref, *, tr):
    i = pl.program_id(1)
    pad = 8

    @pl.when(i == 0)
    def _():
        up_ref[pl.ds(pad - 2, 2), :] = init_ref[...]

    up_ref[pl.ds(pad, tr), :] = c_ref[...] * x_ref[...]
    w = w_ref[...]
    y = (w[0:1, :] * up_ref[pl.ds(pad - 2, tr), :]
         + w[1:2, :] * up_ref[pl.ds(pad - 1, tr), :]
         + w[2:3, :] * up_ref[pl.ds(pad, tr), :])
    ya = _silu(z_ref[...]) * (b_ref[...] * y)
    yn = ya * lax.rsqrt(jnp.mean(ya * ya, axis=-1, keepdims=True) + EPS)
    ya_ref[...] = (yn * g_ref[...]).astype(ya_ref.dtype)
    last = up_ref[pl.ds(pad + tr - 2, 2), :]
    st_ref[...] = last
    up_ref[pl.ds(pad - 2, 2), :] = last


def _short_conv(u3, init_state, conv_w, g_a, tr, out_dtype):
    b, t, _ = u3.shape
    c = conv_w.shape[1]
    col = lambda k: pl.BlockSpec((None, tr, c), lambda bi, i, k=k: (bi, i, k))
    return pl.pallas_call(
        functools.partial(_conv_kernel, tr=tr),
        grid=(b, t // tr),
        in_specs=[
            col(0), col(1), col(2), col(3),
            pl.BlockSpec((None, CONV_WIDTH - 1, c), lambda bi, i: (bi, 0, 0)),
            pl.BlockSpec((CONV_WIDTH, c), lambda bi, i: (0, 0)),
            pl.BlockSpec((1, c), lambda bi, i: (0, 0)),
        ],
        out_specs=[
            pl.BlockSpec((None, tr, c), lambda bi, i: (bi, i, 0)),
            pl.BlockSpec((None, CONV_WIDTH - 1, c), lambda bi, i: (bi, 0, 0)),
        ],
        out_shape=[jax.ShapeDtypeStruct((b, t, c), out_dtype),
                   jax.ShapeDtypeStruct((b, CONV_WIDTH - 1, c), F32)],
        scratch_shapes=[pltpu.VMEM((tr + 8, c), F32)],
        compiler_params=_cparams(("parallel", "arbitrary")),
        name="conv",
    )(u3, u3, u3, u3, init_state, conv_w, g_a)


def _conv_step_kernel(b_ref, c_ref, x_ref, z_ref, h1_ref, h2_ref, w_ref, g_ref, ya_ref, uc_ref, *, ts):
    uc = c_ref[...] * x_ref[...]
    t = lax.broadcasted_iota(jnp.int32, uc.shape, 0) % ts
    back1 = jnp.where(t < 1, h1_ref[...], pltpu.roll(uc, 1, 0))
    back2 = jnp.where(t < 2, h2_ref[...], pltpu.roll(uc, 2, 0))
    w = w_ref[...]
    y = w[0:1, :] * back2 + w[1:2, :] * back1 + w[2:3, :] * uc
    ya = _silu(z_ref[...]) * (b_ref[...] * y)
    yn = ya * lax.rsqrt(jnp.mean(ya * ya, axis=-1, keepdims=True) + EPS)
    ya_ref[...] = yn * g_ref[...]
    uc_ref[...] = uc


def _short_conv_step(u2d, init_state, conv_w, g_a, ts):
    r = u2d.shape[0]
    nseq, _, c = init_state.shape
    assert ts >= CONV_WIDTH - 1
    zero = jnp.zeros((nseq, ts - 1, c), F32)
    h1 = jnp.concatenate([init_state[:, 1:2], zero], axis=1).reshape(r, c)
    h2 = jnp.concatenate([init_state, zero[:, 1:]], axis=1).reshape(r, c)
    col = lambda k: pl.BlockSpec((r, c), lambda i, k=k: (0, k))
    full = lambda shape: pl.BlockSpec(shape, lambda i: (0,) * len(shape))
    ya, uc = pl.pallas_call(
        functools.partial(_conv_step_kernel, ts=ts),
        grid=(1,),
        in_specs=[col(0), col(1), col(2), col(3), full((r, c)), full((r, c)), full(conv_w.shape), full(g_a.shape)],
        out_specs=[full((r, c)), full((r, c))],
        out_shape=[jax.ShapeDtypeStruct((r, c), F32), jax.ShapeDtypeStruct((r, c), F32)],
        compiler_params=_cparams(("arbitrary",)),
        name="conv_step",
    )(u2d, u2d, u2d, u2d, h1, h2, conv_w, g_a)
    return ya, uc.reshape(nseq, ts, c)[:, ts - (CONV_WIDTH - 1):]


def _pe_term(pe, w1cat):
    flat = lambda lo: jnp.concatenate([pe[lo + r:lo + r + 1, :] for r in range(CMP_STRIDE)], axis=1)
    rows = jnp.concatenate([flat(0), flat(CMP_STRIDE), jnp.zeros((6, CMP_STRIDE * HEAD_DIM), F32)], axis=0)
    r = _dot(rows.astype(BF16), w1cat)
    return r[0:1, :HEAD_DIM] + r[1:2, HEAD_DIM:]


def _compress_block(load_sub, pe_term, w1cat, w2, nsp):
    a = jnp.concatenate([load_sub(r).astype(BF16) for r in range(CMP_STRIDE)], axis=1)
    pq = _dot(a, w1cat)
    pre = pq[:, :HEAD_DIM] + pltpu.roll(pq[:, HEAD_DIM:], nsp - 1, 0) + pe_term
    return _dot(_silu(pre).astype(BF16), w2)


def _compress_kernel(k_ref, pe_ref, w1_ref, w2_ref, o_ref, *, nsp):
    load_sub = lambda r: k_ref[pl.ds(r, nsp, stride=CMP_STRIDE), :]
    w1cat = w1_ref[...]
    o_ref[...] = _compress_block(load_sub, _pe_term(pe_ref[...], w1cat), w1cat, w2_ref[...], nsp)


def _compress_prompt(u3, kv_col0, cmp_pe, cmp_w1, cmp_w2):
    b, t, _ = u3.shape
    nsp = t // CMP_STRIDE
    blk0 = kv_col0 // HEAD_DIM
    return pl.pallas_call(
        functools.partial(_compress_kernel, nsp=nsp),
        grid=(b, 2 * N_KV),
        in_specs=[
            pl.BlockSpec((None, t, HEAD_DIM), lambda bi, s: (bi, 0, blk0 + s)),
            pl.BlockSpec((None, CMP_LEN, HEAD_DIM), lambda bi, s: (s // N_KV, 0, 0)),
            pl.BlockSpec((None, CMP_STRIDE * HEAD_DIM, 2 * HEAD_DIM), lambda bi, s: (s // N_KV, 0, 0)),
            pl.BlockSpec((None, HEAD_DIM, HEAD_DIM), lambda bi, s: (s // N_KV, 0, 0)),
        ],
        out_specs=pl.BlockSpec((None, None, nsp, HEAD_DIM), lambda bi, s: (bi, s, 0, 0)),
        out_shape=jax.ShapeDtypeStruct((b, 2 * N_KV, nsp, HEAD_DIM), F32),
        compiler_params=_cparams(("parallel", "parallel")),
        name="compress",
    )(u3, cmp_pe, cmp_w1, cmp_w2)


def _bias(mask):
    return jnp.where(mask, 0.0, NEG)


def _row_max(s):
    return jnp.maximum(jnp.max(s, axis=-1, keepdims=True), 0.1 * NEG)


def _softmax_parts(s):
    e = jnp.exp(s - _row_max(s))
    return e, jnp.maximum(jnp.sum(e, axis=-1, keepdims=True), 1e-30)


def _weighted_values(e, v):
    r = _dot(e.astype(BF16), jnp.concatenate([v, jnp.ones_like(v)], axis=1))
    return r[:, :HEAD_DIM], jnp.maximum(r[:, HEAD_DIM:], 1e-30)


def _fold_lanes(x, op):
    out = x[:, :LANES]
    for c in range(1, x.shape[1] // LANES):
        out = op(out, x[:, c * LANES:(c + 1) * LANES])
    return out


def _compressed_branch(qg, kc, vc, qpos_rows, nc, n_rep):
    ncp = min(kc.shape[0], -(-nc // LANES) * LANES)
    assert ncp >= nc
    kc, vc = kc[:ncp], vc[:ncp]
    cidx = lax.broadcasted_iota(jnp.int32, (1, ncp), 1)
    mask = ((cidx * CMP_STRIDE + (CMP_LEN - 1)) <= qpos_rows) & (cidx < nc)
    e, l = _softmax_parts(_dot_nt(qg, kc.astype(BF16)) + _bias(mask))
    p = e / l
    o = _dot(p.astype(BF16), vc.astype(BF16))
    rt = qg.shape[0] // n_rep
    p_tok = p[0:rt]
    for r in range(1, n_rep):
        p_tok = p_tok + p[r * rt:(r + 1) * rt]
    return o, p_tok


def _pool_matrix(rows, cols, blocks_on_rows):
    per = SEL_BLOCK // CMP_STRIDE
    r = lax.broadcasted_iota(jnp.int32, (rows, cols), 0)
    c = lax.broadcasted_iota(jnp.int32, (rows, cols), 1)
    hit = (c // per == r) if blocks_on_rows else (r // per == c)
    return jnp.where(hit, 1.0, 0.0).astype(F32)


def _top_blocks(score, blk, nb, axis):
    rank = jnp.zeros(score.shape, jnp.int32)
    for j in range(nb):
        one = score[j:j + 1, :] if axis == 0 else score[:, j:j + 1]
        beats = (one > score) | ((one == score) & (blk > j))
        rank = rank + jnp.where(beats, 1, 0)
    return jnp.where((rank < N_SELECT) & (score > 0.5 * NEG), 1.0, 0.0)


def _block_scores(imp, blk, cur):
    forced = (blk == 0) | (blk == cur) | (blk == cur - 1)
    return jnp.where(blk <= cur, jnp.where(forced, FORCE_SCORE, imp), NEG)


def _select_blocks_rows(p_tok, tok_row, nb):
    rt, ncp = p_tok.shape
    nbr = -(-nb // 8) * 8
    imp_t = lax.dot_general(_pool_matrix(nbr, ncp, True), p_tok, (((1,), (1,)), ((), ())),
                            preferred_element_type=F32, precision=lax.Precision.HIGHEST)
    blk = lax.broadcasted_iota(jnp.int32, (nbr, rt), 0)
    sel_t = _top_blocks(_block_scores(imp_t, blk, tok_row // SEL_BLOCK), blk, nb, 0)
    if nbr < LANES:
        sel_t = jnp.concatenate([sel_t, jnp.zeros((LANES - nbr, rt), F32)], axis=0)
    return _bias(sel_t.T > 0.5).astype(BF16)


def _select_blocks_lanes(p_tok, tok_col, nb):
    rt, ncp = p_tok.shape
    imp = jnp.dot(p_tok, _pool_matrix(ncp, LANES, False), preferred_element_type=F32,
                  precision=lax.Precision.HIGHEST)
    blk = lax.broadcasted_iota(jnp.int32, (rt, LANES), 1)
    return _top_blocks(_block_scores(imp, blk, tok_col // SEL_BLOCK), blk, nb, 1).astype(BF16)


def _attn_prompt_kernel(q_ref, kvs_ref, kvw_ref, cmp_ref, gates_ref, z_ref, ggrp_ref, exp_ref, o_ref,
                        s_ref, *, nc, nb, gqa, att_w, t, wk):
    i = pl.program_id(1)
    s0 = i * TQ
    scale = HEAD_DIM ** -0.5
    tok = lax.broadcasted_iota(jnp.int32, (TQ, 1), 0) + s0
    tok_row = lax.broadcasted_iota(jnp.int32, (1, TQ), 1) + s0
    qpos_rows = jnp.concatenate([tok] * gqa, axis=0)
    rows = TQ * gqa
    rep = lambda a: jnp.concatenate([a] * gqa, axis=0)
    kcol = lambda g: slice(g * HEAD_DIM, (g + 1) * HEAD_DIM)
    vcol = lambda g: slice(KV_W + g * HEAD_DIM, KV_W + (g + 1) * HEAD_DIM)
    groups = range(N_KV)

    qs = [(jnp.concatenate(
        [q_ref[:, (g * gqa + r) * HEAD_DIM:(g * gqa + r + 1) * HEAD_DIM] for r in range(gqa)],
        axis=0) * scale).astype(BF16) for g in groups]

    o_cmp, sel = [], []
    for g in groups:
        o, p_tok = _compressed_branch(qs[g], cmp_ref[g], cmp_ref[N_KV + g], qpos_rows, nc, gqa)
        o_cmp.append(o)
        sel.append(_select_blocks_rows(p_tok, tok_row, nb))

    w_lo = pl.multiple_of(jnp.clip(s0 - WINDOW, 0, t - wk), TQ)
    wpos = w_lo + lax.broadcasted_iota(jnp.int32, (1, wk), 1)
    wdist = tok - wpos
    wbias = rep(_bias((wdist >= 0) & (wdist < WINDOW)))
    o_win = []
    for g in groups:
        kw = kvw_ref[pl.ds(w_lo, wk), kcol(g)].astype(BF16)
        vw = kvw_ref[pl.ds(w_lo, wk), vcol(g)].astype(BF16)
        sw = _dot_nt(qs[g], kw) + wbias
        num, den = _weighted_values(jnp.exp(sw - _row_max(sw)), vw)
        o_win.append(num / den)

    head_rows = lambda a, r: a[r * TQ:(r + 1) * TQ]
    gate = lambda h, k: gates_ref[:, 3 * h + k:3 * h + k + 1]
    partial, zact = [], []
    for g in groups:
        for r in range(gqa):
            h = g * gqa + r
            partial.append(gate(h, 0) * head_rows(o_cmp[g], r) + gate(h, 2) * head_rows(o_win[g], r))
            zact.append(_silu(z_ref[:, h * HEAD_DIM:(h + 1) * HEAD_DIM]))

    n_sel = (s0 + TQ + KC_SEL - 1) // KC_SEL

    qb = [jnp.concatenate([qs[g], rep(sel[g])], axis=1) for g in groups]

    def score_body(ci, mruns, causal):
        k0 = pl.multiple_of(ci * KC_SEL, KC_SEL)
        blk = exp_ref[pl.ds(k0, KC_SEL), :]
        out = []
        for g in groups:
            k = kvs_ref[pl.ds(k0, KC_SEL), kcol(g)].astype(BF16)
            s = _dot_nt(qb[g], jnp.concatenate([k, blk], axis=1))
            if causal:
                s = s + rep(_bias((k0 + lax.broadcasted_iota(jnp.int32, (1, KC_SEL), 1)) <= tok))
            s_ref[g, :, pl.ds(k0, KC_SEL)] = s
            out.append(jnp.maximum(mruns[g], _fold_lanes(s, jnp.maximum)))
        return tuple(out)

    mruns = lax.fori_loop(0, n_sel - 1, functools.partial(score_body, causal=False),
                          tuple(jnp.full((rows, LANES), NEG, F32) for _ in groups))
    mruns = score_body(n_sel - 1, mruns, True)
    ms = [_row_max(mruns[g]) for g in groups]

    def value_body(ci, carry):
        k0 = pl.multiple_of(ci * KC_SEL, KC_SEL)
        out = []
        for g in groups:
            v = kvs_ref[pl.ds(k0, KC_SEL), vcol(g)].astype(BF16)
            e = jnp.exp(s_ref[g, :, pl.ds(k0, KC_SEL)] - ms[g])
            out.append(carry[g] + _dot(e.astype(BF16), jnp.concatenate([v, jnp.ones_like(v)], axis=1)))
        return tuple(out)

    sums = lax.fori_loop(0, n_sel, value_body, tuple(jnp.zeros((rows, 2 * HEAD_DIM), F32) for _ in groups))

    heads = []
    for g in groups:
        o_slc = sums[g][:, :HEAD_DIM] / jnp.maximum(sums[g][:, HEAD_DIM:], 1e-30)
        for r in range(gqa):
            h = g * gqa + r
            heads.append((partial[h] + gate(h, 1) * head_rows(o_slc, r)) * zact[h])

    ss = jnp.sum(heads[0] * heads[0], axis=-1, keepdims=True)
    for y in heads[1:]:
        ss = ss + jnp.sum(y * y, axis=-1, keepdims=True)
    inv = lax.rsqrt(ss / att_w + EPS)
    o_ref[...] = jnp.concatenate(
        [(y * inv) * ggrp_ref[:, h * HEAD_DIM:(h + 1) * HEAD_DIM] for h, y in enumerate(heads)],
        axis=1).astype(o_ref.dtype)


def _attention_prompt(u3, gates3, kcmp, g_b, expand, cols, att_w):
    b, t, _ = u3.shape
    gqa = att_w // HEAD_DIM // N_KV
    nsub = t // CMP_STRIDE
    nc = nsub - CMP_LEN // CMP_STRIDE + 1
    nb = max(-(-t // SEL_BLOCK), N_SELECT)
    wk = min(t, WINDOW + TQ)
    kern = functools.partial(_attn_prompt_kernel, nc=nc, nb=nb, gqa=gqa, att_w=att_w, t=t, wk=wk)
    two_kv = 2 * KV_W
    return pl.pallas_call(
        kern,
        grid=(b, t // TQ),
        in_specs=[
            pl.BlockSpec((None, TQ, att_w), lambda bi, i: (bi, i, cols["q"] // att_w)),
            pl.BlockSpec((None, t, two_kv), lambda bi, i: (bi, 0, cols["kvs"] // two_kv)),
            pl.BlockSpec((None, t, two_kv), lambda bi, i: (bi, 0, cols["kvw"] // two_kv)),
            pl.BlockSpec((None, 2 * N_KV, kcmp.shape[2], HEAD_DIM), lambda bi, i: (bi, 0, 0, 0)),
            pl.BlockSpec((None, TQ, LANES), lambda bi, i: (bi, i, 0)),
            pl.BlockSpec((None, TQ, att_w), lambda bi, i: (bi, i, cols["zb"] // att_w)),
            pl.BlockSpec((1, att_w), lambda bi, i: (0, 0)),
            pl.BlockSpec((t, LANES), lambda bi, i: (0, 0)),
        ],
        out_specs=pl.BlockSpec((None, TQ, att_w), lambda bi, i: (bi, i, 0)),
        out_shape=jax.ShapeDtypeStruct((b, t, att_w), BF16),
        scratch_shapes=[pltpu.VMEM((N_KV, TQ * gqa, t), F32)],
        compiler_params=_cparams(("parallel", "arbitrary")),
        name="attn_prompt",
    )(u3, u3, u3, kcmp, gates3, u3, g_b, expand)


def _page_copies(pt_ref, cache_ref, full_ref, sem_ref, layer, b, slot, n_pages, page):
    copies = []
    for p in range(n_pages):
        pg = pt_ref[b * n_pages + p]
        for s in range(4 * N_KV):
            copies.append(pltpu.make_async_copy(
                cache_ref.at[layer, pg, :, s, :], full_ref.at[slot, s, pl.ds(p * page, page), :],
                sem_ref.at[slot]))
    return copies


def _attn_sample_kernel(pt_ref, q_ref, gcol_ref, z_ref, ggrp_ref, newkv_ref, neww_ref, cw_ref,
                        pe_ref, w1_ref, w2_ref, exp_ref, cache_ref, *rest,
                        layer, n_pages, page, ts, wbuf, nsp, nc, nb, gqa, att_w, has_prev):
    o_ref, win_ref, full_ref, kwin_ref, pe_term_ref, sem_ref = rest[(1 if has_prev else 0):]
    n_kv_streams, n_win_streams, n_cmp_streams = 4 * N_KV, 2 * N_KV, 2 * N_KV
    past = n_pages * page
    ktot = nsp * CMP_STRIDE
    scale = HEAD_DIM ** -0.5
    rep = lambda a: jnp.concatenate([a] * gqa, axis=0)
    b = pl.program_id(0)
    n_b = pl.num_programs(0)
    slot = b % 2
    copies = lambda bb, sl: _page_copies(pt_ref, cache_ref, full_ref, sem_ref, layer, bb, sl, n_pages, page)

    @pl.when(b == 0)
    def _():
        for br in range(2):
            pe_term_ref[br] = jnp.broadcast_to(_pe_term(pe_ref[br], w1_ref[br]), (8, HEAD_DIM))
        for k, cp in enumerate(copies(0, 0)):
            cp.start(priority=k % 2)

    for parity in range(2):
        @pl.when((slot == parity) & (b + 1 < n_b))
        def _(parity=parity):
            for k, cp in enumerate(copies(b + 1, 1 - parity)):
                cp.start(priority=k % 2)

    for parity in range(2):
        @pl.when(slot == parity)
        def _(parity=parity):
            for cp in copies(b, parity):
                cp.wait()

    for s in range(n_kv_streams):
        full_ref[slot, s, pl.ds(past, ts), :] = newkv_ref[pl.ds(s, ts, stride=n_kv_streams), :]
        full_ref[slot, s, pl.ds(past + ts, ktot - past - ts), :] = jnp.zeros((ktot - past - ts, HEAD_DIM), F32)
    for s in range(n_win_streams):
        kwin_ref[s, pl.ds(0, wbuf), :] = cw_ref[pl.ds(s, wbuf, stride=n_win_streams), :]
        kwin_ref[s, pl.ds(wbuf, ts), :] = neww_ref[pl.ds(s, ts, stride=n_win_streams), :]
        kwin_ref[s, pl.ds(wbuf + ts, LANES - ts), :] = jnp.zeros((LANES - ts, HEAD_DIM), F32)
    keep = (wbuf - ts) * n_win_streams
    win_ref[pl.ds(0, keep), :] = cw_ref[pl.ds(ts * n_win_streams, keep), :]
    win_ref[pl.ds(keep, ts * n_win_streams), :] = neww_ref[...]

    tok = past + lax.broadcasted_iota(jnp.int32, (T_PAD, 1), 0)
    qpos_rows = rep(tok)
    kpos = lax.broadcasted_iota(jnp.int32, (1, ktot), 1)
    widx = lax.broadcasted_iota(jnp.int32, (1, wbuf + LANES), 1)
    wpos = past - wbuf + widx
    wdist = tok - wpos
    wbias = rep(_bias((wdist >= 0) & (wdist < WINDOW) & (wpos >= 0) & (widx < wbuf + ts)))

    groups = range(N_KV)
    stacked = [_compress_block(
        (lambda r, br=br: jnp.concatenate(
            [full_ref[slot, br * N_KV + g, pl.ds(r, nsp, stride=CMP_STRIDE), :] for g in groups], axis=0)),
        pe_term_ref[br][0:1, :], w1_ref[br], w2_ref[br], N_KV * nsp) for br in range(2)]
    cmp_kv = [[stacked[br][g * nsp:(g + 1) * nsp] for br in range(2)] for g in groups]
    qs = [(q_ref[g] * scale).astype(BF16) for g in groups]
    cmp_out = [_compressed_branch(qs[g], cmp_kv[g][0], cmp_kv[g][1], qpos_rows, nc, gqa) for g in groups]
    sels = [_select_blocks_lanes(cmp_out[g][1], tok, nb) for g in groups]
    sbias = [rep(_bias((_dot(sels[g], exp_ref[...]) > 0.5) & (kpos <= tok))) for g in groups]
    k_slc = lambda g: full_ref[slot, n_cmp_streams + g].astype(BF16)
    v_slc = lambda g: full_ref[slot, n_cmp_streams + N_KV + g].astype(BF16)
    slc = [_softmax_parts(_dot_nt(qs[g], k_slc(g)) + sbias[g]) for g in groups]
    o_slc = [_dot(slc[g][0].astype(BF16), v_slc(g)) / slc[g][1] for g in groups]
    win = [_softmax_parts(_dot_nt(qs[g], kwin_ref[g].astype(BF16)) + wbias) for g in groups]
    o_win = [_dot(win[g][0].astype(BF16), kwin_ref[N_KV + g].astype(BF16)) / win[g][1] for g in groups]
    ys = []
    for g in groups:
        gcol = gcol_ref[g]
        o = gcol[:, 0:1] * cmp_out[g][0] + gcol[:, 1:2] * o_slc[g] + gcol[:, 2:3] * o_win[g]
        ys.append(o * _silu(z_ref[g]))

    ss = None
    for y in ys:
        rs = jnp.sum(y * y, axis=-1, keepdims=True)
        for r in range(gqa):
            part = rs[r * T_PAD:(r + 1) * T_PAD]
            ss = part if ss is None else ss + part
    inv = rep(lax.rsqrt(ss / att_w + EPS))
    for g in range(N_KV):
        o_ref[g] = (ys[g] * inv) * ggrp_ref[g]


def _attention_sample(layer, page_table, cache_row, win_rows, newkv_rows, neww_rows, q_r, gcol_r, z_r,
                      ggrp_r, pe, w1, w2, expand, ts, att_w, win_prev):
    db = q_r.shape[0]
    n_pages = page_table.shape[1]
    n_kv_streams, n_win_streams = 4 * N_KV, 2 * N_KV
    page = cache_row.shape[2]
    past = n_pages * page
    wbuf = win_rows.shape[2] // n_win_streams
    gqa = att_w // HEAD_DIM // N_KV
    nsub = -(-(past + ts) // CMP_STRIDE)
    nc = nsub - CMP_LEN // CMP_STRIDE + 1
    nsp = -(-nsub // 8) * 8
    nb = max(-(-(past + ts) // SEL_BLOCK), N_SELECT)
    rows = gqa * T_PAD
    kern = functools.partial(_attn_sample_kernel, layer=layer, n_pages=n_pages, page=page, ts=ts, wbuf=wbuf,
                             nsp=nsp, nc=nc, nb=nb, gqa=gqa, att_w=att_w, has_prev=win_prev is not None)
    per_b = lambda shape: pl.BlockSpec((None,) + shape, lambda b, pt: (b,) + (0,) * len(shape))
    const = lambda shape: pl.BlockSpec(shape, lambda b, pt: (0,) * len(shape))
    in_specs = [
        per_b((N_KV, rows, HEAD_DIM)),
        per_b((N_KV, rows, HEAD_DIM)),
        per_b((N_KV, rows, HEAD_DIM)),
        const((N_KV, rows, HEAD_DIM)),
        per_b((ts * n_kv_streams, HEAD_DIM)),
        per_b((ts * n_win_streams, HEAD_DIM)),
        pl.BlockSpec((None, None, wbuf * n_win_streams, HEAD_DIM), lambda b, pt: (layer, b, 0, 0)),
        const((2, CMP_LEN, HEAD_DIM)),
        const((2, CMP_STRIDE * HEAD_DIM, 2 * HEAD_DIM)),
        const((2, HEAD_DIM, HEAD_DIM)),
        const((LANES, nsp * CMP_STRIDE)),
        pl.BlockSpec(memory_space=pl.ANY),
    ]
    args = [page_table.reshape(-1), q_r, gcol_r, z_r, ggrp_r, newkv_rows, neww_rows, win_rows, pe, w1, w2,
            expand, cache_row]
    aliases = {}
    if win_prev is not None:
        in_specs.append(pl.BlockSpec(memory_space=pl.ANY))
        aliases = {len(args): 1}
        args.append(win_prev)
    depth = win_rows.shape[0]
    grid_spec = pltpu.PrefetchScalarGridSpec(
        num_scalar_prefetch=1,
        grid=(db,),
        in_specs=in_specs,
        out_specs=[per_b((N_KV, rows, HEAD_DIM)),
                   pl.BlockSpec((None, None, wbuf * n_win_streams, HEAD_DIM), lambda b, pt: (layer, b, 0, 0))],
        scratch_shapes=[pltpu.VMEM((2, n_kv_streams, nsp * CMP_STRIDE, HEAD_DIM), F32),
                        pltpu.VMEM((n_win_streams, wbuf + LANES, HEAD_DIM), F32),
                        pltpu.VMEM((2, 8, HEAD_DIM), F32),
                        pltpu.SemaphoreType.DMA((2,))],
    )
    return pl.pallas_call(
        kern,
        grid_spec=grid_spec,
        out_shape=[jax.ShapeDtypeStruct((db, N_KV, rows, HEAD_DIM), F32),
                   jax.ShapeDtypeStruct((depth, db, wbuf * n_win_streams, HEAD_DIM), F32)],
        input_output_aliases=aliases,
        compiler_params=_cparams(("arbitrary",)),
        name="attn_sample",
    )(*args)


def _merge_kernel(ya_ref, yb_ref, x_ref, gate_ref, w_ref, g_ref, o_ref, *, conv_ch):
    y = (_dot(ya_ref[...].astype(BF16), w_ref[:conv_ch, :])
         + _dot(yb_ref[...].astype(BF16), w_ref[conv_ch:, :]))
    yn = y * lax.rsqrt(jnp.mean(y * y, axis=-1, keepdims=True) + EPS)
    o_ref[...] = x_ref[...] + gate_ref[...] * (yn * g_ref[...])


def _merge(ya, yb, x2d, gate, w_out, g_post, tm):
    r, d = x2d.shape
    conv_ch = ya.shape[1]
    tiles_per_gate = (r // tm) // gate.shape[0]
    s_rows = gate.shape[1]
    return pl.pallas_call(
        functools.partial(_merge_kernel, conv_ch=conv_ch),
        grid=(r // tm,),
        in_specs=[
            pl.BlockSpec((tm, conv_ch), lambda i: (i, 0)),
            pl.BlockSpec((tm, d - conv_ch), lambda i: (i, 0)),
            pl.BlockSpec((tm, d), lambda i: (i, 0)),
            pl.BlockSpec((None, s_rows, d), lambda i: (i // tiles_per_gate, 0, 0)),
            pl.BlockSpec((d, d), lambda i: (0, 0)),
            pl.BlockSpec((1, d), lambda i: (0, 0)),
        ],
        out_specs=pl.BlockSpec((tm, d), lambda i: (i, 0)),
        out_shape=jax.ShapeDtypeStruct((r, d), F32),
        compiler_params=_cparams(("parallel",)),
        name="merge",
    )(ya, yb, x2d, gate, w_out, g_post)


def _rope_tables(pos):
    half = HEAD_DIM // 2
    inv = jnp.power(ROPE_THETA, -jnp.arange(half, dtype=F32) / half)
    ang = pos.astype(F32)[:, None] * inv
    cos, sin = jnp.cos(ang), jnp.sin(ang)
    return jnp.concatenate([cos, cos], axis=-1), jnp.concatenate([-sin, sin], axis=-1)


def _expand_matrix(n_keys):
    blk = jnp.arange(n_keys, dtype=jnp.int32) // SEL_BLOCK
    return (blk[None, :] == jnp.arange(LANES, dtype=jnp.int32)[:, None]).astype(BF16)


def kernel(x_prompt, x_sample, cache_kv, cache_win, state_conv, page_table, c_prompt, c_sample,
           w_ada, b_ada, g_pre, w_in, conv_w, cmp_pe, cmp_w1, cmp_w2, g_grp, w_out, g_post):
    bp, tp, d = x_prompt.shape
    bs, ts, _ = x_sample.shape
    depth = w_in.shape[0]
    conv_ch = conv_w.shape[2]
    att_w = d - conv_ch
    n_heads = att_w // HEAD_DIM
    gqa = n_heads // N_KV
    n_pages, page = page_table.shape[1], cache_kv.shape[2]
    past = n_pages * page
    wbuf = cache_win.shape[2]
    assert tp % TQ == 0 and tp % KC_SEL == 0 and ts <= T_PAD
    assert att_w == 4 * KV_W and 3 * n_heads <= LANES

    c_q = 4 * conv_ch
    c_kv = c_q + att_w
    c_zb = c_kv + 4 * KV_W
    c_kvw = c_zb + att_w
    cols = {"q": c_q, "kv": c_kv, "kvs": c_kv + 2 * KV_W, "zb": c_zb, "kvw": c_kvw}
    o_kw = c_kv + 4 * KV_W
    o_g = o_kw + 2 * KV_W
    o_zb = o_g + 3 * n_heads
    w_main = jnp.concatenate([w_in[:, :, :o_kw], w_in[:, :, o_zb:], w_in[:, :, o_kw:o_g]], axis=-1).astype(BF16)
    w_gate = jnp.pad(w_in[:, :, o_g:o_zb], ((0, 0), (0, 0), (0, LANES - 3 * n_heads))).astype(BF16)
    w_out_b = w_out.astype(BF16)
    half = CMP_STRIDE * HEAD_DIM
    w1_b = jnp.concatenate([cmp_w1[:, :, :half], cmp_w1[:, :, half:]], axis=-1).astype(BF16)
    w2_b = cmp_w2.astype(BF16)

    mod = _modulation(jnp.concatenate([c_prompt, c_sample], axis=0), w_ada, b_ada)

    cos_p, sin_p = _rope_tables(jnp.arange(tp))
    pos_s = past + jnp.arange(ts)
    cos_s, sin_s = (jnp.tile(a, (bs, 1)) for a in _rope_tables(pos_s))
    expand_p = _expand_matrix(tp).T
    nsub_s = -(-(past + ts) // CMP_STRIDE)
    nsp_s = -(-nsub_s // 8) * 8
    expand_s = _expand_matrix(nsp_s * CMP_STRIDE)

    tm_p = min(tp, TM_PROJ)
    rows_s = bs * ts
    cache_row = cache_kv.reshape(depth, cache_kv.shape[1], page, 4 * N_KV, HEAD_DIM)
    win_rows = cache_win.reshape(depth, bs, wbuf * 2 * N_KV, HEAD_DIM)

    def to_heads(a):
        a = a.reshape(bs, ts, N_KV, gqa, HEAD_DIM).transpose(0, 2, 3, 1, 4)
        a = jnp.pad(a, ((0, 0), (0, 0), (0, 0), (0, T_PAD - ts), (0, 0)))
        return a.reshape(bs, N_KV, gqa * T_PAD, HEAD_DIM)

    xp = x_prompt.reshape(bp * tp, d)
    xs = x_sample.reshape(rows_s, d)
    win_keep = min(WINDOW, tp)
    rows_p = rows_sm = win_s = None
    conv_p, conv_s = [], []
    for l in range(depth):
        shift, scale, gate = mod[l, :, :d], mod[l, :, d:2 * d], mod[l, :, 2 * d:]
        g_a, g_b = g_grp[l, None, :conv_ch], g_grp[l, None, conv_ch:]

        u, gates, *rows_p = _in_projection(xp, scale[:bp, None], shift[:bp, None], g_pre[l, None], cos_p, sin_p,
                                           w_main[l], w_gate[l], tm_p, conv_ch, att_w, l, depth, bp, win_keep,
                                           rows_p)
        u3 = u.reshape(bp, tp, -1)
        ya, cbuf = _short_conv(u3, jnp.zeros((bp, CONV_WIDTH - 1, conv_ch), F32), conv_w[l], g_a,
                               min(tp, TR_CONV), BF16)
        kcmp = _compress_prompt(u3, c_kv, cmp_pe[l], w1_b[l], w2_b[l])
        yb = _attention_prompt(u3, gates.reshape(bp, tp, LANES), kcmp, g_b, expand_p, cols, att_w)
        xp = _merge(ya.reshape(bp * tp, conv_ch), yb.reshape(bp * tp, att_w), xp, gate[:bp, None],
                    w_out_b[l], g_post[l, None], min(tp, TM_MERGE))
        conv_p.append(cbuf)

        rep = lambda a: jnp.repeat(a[bp:], ts, axis=0)[None]
        u, gates, *rows_sm = _in_projection(xs, rep(scale), rep(shift), g_pre[l, None], cos_s, sin_s,
                                            w_main[l], w_gate[l], rows_s, conv_ch, att_w, l, depth, 1, rows_s,
                                            rows_sm)
        ya, cbuf = _short_conv_step(u, state_conv[l], conv_w[l], g_a, ts)
        q_r = to_heads(u[:, c_q:c_q + att_w])
        z_r = to_heads(u[:, c_zb:c_zb + att_w])
        gcol = gates[:, :3 * n_heads].reshape(rows_s, n_heads, 3)
        gcol_r = to_heads(jnp.pad(gcol, ((0, 0), (0, 0), (0, HEAD_DIM - 3))).reshape(rows_s, att_w))
        ggrp_r = jnp.broadcast_to(g_b.reshape(N_KV, gqa, 1, HEAD_DIM),
                                  (N_KV, gqa, T_PAD, HEAD_DIM)).reshape(N_KV, gqa * T_PAD, HEAD_DIM)
        newkv_rows = rows_sm[0][l].reshape(bs, ts * 4 * N_KV, HEAD_DIM)
        neww_rows = rows_sm[1][l].reshape(bs, ts * 2 * N_KV, HEAD_DIM)
        yb_r, win_s = _attention_sample(l, page_table, cache_row, win_rows, newkv_rows, neww_rows, q_r, gcol_r,
                                        z_r, ggrp_r, cmp_pe[l], w1_b[l], w2_b[l], expand_s, ts, att_w, win_s)
        yb = yb_r.reshape(bs, N_KV, gqa, T_PAD, HEAD_DIM)[:, :, :, :ts].transpose(0, 3, 1, 2, 4)
        xs = _merge(ya, yb.reshape(rows_s, att_w), xs, rep(gate), w_out_b[l], g_post[l, None], rows_s)
        conv_s.append(cbuf)

    kv_shape = lambda b, t: (depth, b, t, 4, N_KV, HEAD_DIM)
    win_shape = lambda b, t: (depth, b, t, 2, N_KV, HEAD_DIM)
    return (xp.reshape(bp, tp, d), xs.reshape(bs, ts, d),
            rows_p[0].reshape(kv_shape(bp, tp)), rows_sm[0].reshape(kv_shape(bs, ts)),
            rows_p[1].reshape(win_shape(bp, win_keep)), win_s.reshape(win_shape(bs, wbuf)),
            jnp.stack(conv_p), jnp.stack(conv_s))
```
